```python
import math
import jax, jax.numpy as jnp
from jax import lax
import numpy as np

D_MODEL = 2048
BATCH = 1
SEQ = 16384
DEPTH = 1

GRID_W = 64
CTX_LEN = 256
EPS = 1e-6

A_HEADS = 16
A_KV_HEADS = 2
A_GROUP = A_HEADS // A_KV_HEADS
A_HEAD_DIM = 64
A_WIDTH = A_HEADS * A_HEAD_DIM
A_KV_WIDTH = A_KV_HEADS * A_HEAD_DIM
WINDOW = 128
A_BLOCK = 128
ROPE_BASE = 10000.0
ROPE_PAIRS_PER_AXIS = A_HEAD_DIM // 4

B_HEADS = 8
B_DK = 128
B_DV = 128
B_KEY_WIDTH = B_HEADS * B_DK
B_WIDTH = B_HEADS * B_DV
B_CHUNK = 64

BRANCH_WIDTH = D_MODEL // 2
N_BRANCHES = 2

IN_SIZES = (A_WIDTH, A_KV_WIDTH, A_KV_WIDTH, B_KEY_WIDTH, B_KEY_WIDTH, B_KEY_WIDTH,
            B_WIDTH, B_WIDTH, D_MODEL, D_MODEL)
IN_COLS = sum(IN_SIZES)
IN_SPLITS = tuple(int(s) for s in np.cumsum(IN_SIZES)[:-1])

N_EXPERTS = 32
TOP_K = 4
EXPERT_FF = 2048
SWIGLU_LIMIT = 7.0
SWIGLU_ALPHA = 1.702
MOE_BLOCK = 128

kernel_name = "hybrid_gqa_hgrn2_moe_dit_block"


def rms_norm(x, g):
    xf = x.astype(jnp.float32)
    y = xf * lax.rsqrt(jnp.mean(xf * xf, axis=-1, keepdims=True) + EPS)
    return (y * g.astype(jnp.float32)).astype(x.dtype)


def modulate(h, shift, scale):
    return h * (1.0 + scale) + shift


def axial_rope_tables(n_tokens):
    rows = n_tokens // GRID_W
    row = jnp.repeat(jnp.arange(rows, dtype=jnp.float32), GRID_W)
    col = jnp.tile(jnp.arange(GRID_W, dtype=jnp.float32), rows)
    inv_freq = ROPE_BASE ** (-jnp.arange(ROPE_PAIRS_PER_AXIS, dtype=jnp.float32) / ROPE_PAIRS_PER_AXIS)
    ang = jnp.concatenate([row[:, None] * inv_freq, col[:, None] * inv_freq], axis=-1)
    return jnp.cos(ang)[:, None, :], jnp.sin(ang)[:, None, :]


def apply_rope(t, cos, sin):
    tf = t.astype(jnp.float32)
    t1, t2 = jnp.split(tf, 2, axis=-1)
    return jnp.concatenate([t1 * cos - t2 * sin, t2 * cos + t1 * sin], axis=-1).astype(t.dtype)


def windowed_gqa_with_sink(q, k, v, k_ctx, v_ctx, sink):
    bsz, n, _, hd = q.shape
    n_ctx = k_ctx.shape[1]
    nblk = n // A_BLOCK
    scale = A_HEAD_DIM ** -0.5
    qg = q.reshape(bsz, nblk, A_BLOCK, A_KV_HEADS, A_GROUP, hd).transpose(1, 0, 2, 3, 4, 5)

    def band(t):
        tp = jnp.pad(t, ((0, 0), (A_BLOCK, A_BLOCK), (0, 0), (0, 0)))
        tp = tp.reshape(bsz, nblk + 2, A_BLOCK, A_KV_HEADS, hd)
        return jnp.concatenate([tp[:, :-2], tp[:, 1:-1], tp[:, 2:]], axis=2).transpose(1, 0, 2, 3, 4)

    k_band, v_band = band(k), band(v)
    sink_l = sink.astype(jnp.float32).reshape(1, A_KV_HEADS, A_GROUP, 1, 1)
    rel = jnp.arange(3 * A_BLOCK)[None, :] - A_BLOCK - jnp.arange(A_BLOCK)[:, None]

    def one_block(args):
        qb, kb, vb, i = args
        key_pos = (i - 1) * A_BLOCK + jnp.arange(3 * A_BLOCK)
        mask = (jnp.abs(rel) <= WINDOW) & (key_pos >= 0)[None, :] & (key_pos < n)[None, :]
        s_ctx = jnp.einsum('bqkgd,bjkd->bkgqj', qb, k_ctx).astype(jnp.float32) * scale
        s_lat = jnp.einsum('bqkgd,bjkd->bkgqj', qb, kb).astype(jnp.float32) * scale
        s_lat = jnp.where(mask, s_lat, -jnp.inf)
        s_sink = jnp.broadcast_to(sink_l, s_ctx.shape[:-1] + (1,))
        p = jax.nn.softmax(jnp.concatenate([s_ctx, s_lat, s_sink], axis=-1), axis=-1).astype(v.dtype)
        return (jnp.einsum('bkgqj,bjkd->bqkgd', p[..., :n_ctx], v_ctx)
                + jnp.einsum('bkgqj,bjkd->bqkgd', p[..., n_ctx:-1], vb))

    o = lax.map(one_block, (qg, k_band, v_band, jnp.arange(nblk)))
    return o.transpose(1, 0, 2, 3, 4, 5).reshape(bsz, n, A_WIDTH)


def context_attention(q, k, v, sink):
    bsz, m = q.shape[:2]
    qg = q.reshape(bsz, m, A_KV_HEADS, A_GROUP, A_HEAD_DIM)
    s = jnp.einsum('bqkgd,bjkd->bkgqj', qg, k).astype(jnp.float32) * (A_HEAD_DIM ** -0.5)
    s_sink = jnp.broadcast_to(sink.astype(jnp.float32).reshape(1, A_KV_HEADS, A_GROUP, 1, 1), s.shape[:-1] + (1,))
    p = jax.nn.softmax(jnp.concatenate([s, s_sink], axis=-1), axis=-1)[..., :-1].astype(v.dtype)
    return jnp.einsum('bkgqj,bjkd->bqkgd', p, v).reshape(bsz, m, A_WIDTH)


def hgrn2_chunk_scan(q, k, v, logf, s0):
    bsz, h, n, _ = q.shape
    dv = v.shape[-1]
    nc = n // B_CHUNK

    def chunks(t):
        return jnp.moveaxis(t.astype(jnp.float32).reshape(bsz, h, nc, B_CHUNK, t.shape[-1]), 2, 0)

    tril = jnp.tril(jnp.ones((B_CHUNK, B_CHUNK), dtype=bool))[:, :, None]

    def step(state, inp):
        qc, kc, vc, lf = inp
        b = jnp.cumsum(lf, axis=2)
        b_last = b[:, :, -1:, :]
        o_inter = jnp.einsum('bhtd,bhde->bhte', qc * jnp.exp(b), state)
        decay = jnp.where(tril, jnp.exp(jnp.minimum(b[:, :, :, None, :] - b[:, :, None, :, :], 0.0)), 0.0)
        scores = jnp.einsum('bhtd,bhsd,bhtsd->bhts', qc, kc, decay)
        o = o_inter + jnp.einsum('bhts,bhse->bhte', scores, vc)
        state = (jnp.exp(b_last[:, :, 0, :])[..., None] * state
                 + jnp.einsum('bhsd,bhse->bhde', kc * jnp.exp(b_last - b), vc))
        return state, o

    s_final, o = lax.scan(step, s0, (chunks(q), chunks(k), chunks(v), chunks(logf)))
    return jnp.moveaxis(o, 0, 2).reshape(bsz, h, n, dv), s_final


def hgrn2_direction(q_c, z_c, v_c, q_l, z_l, v_l, lb, reverse):
    bsz = q_l.shape[0]
    lbh = lb.reshape(1, B_HEADS, 1, B_DK)

    def gates(z):
        zf = z.astype(jnp.float32)
        logf = jnp.log(lbh + (1.0 - lbh) * jax.nn.sigmoid(zf))
        k = (1.0 - lbh) * jax.nn.sigmoid(-zf)
        return k, logf

    def flip(t):
        return jnp.flip(t, axis=2) if reverse else t

    k_c, lf_c = gates(z_c)
    k_l, lf_l = gates(z_l)
    s0 = jnp.zeros((bsz, B_HEADS, B_DK, B_DV), jnp.float32)
    o_c, s_c = hgrn2_chunk_scan(flip(q_c), flip(k_c), flip(v_c), flip(lf_c), s0)
    o_l, _ = hgrn2_chunk_scan(flip(q_l), flip(k_l), flip(v_l), flip(lf_l), s_c)
    return flip(o_l), flip(o_c)


def hybrid_mixer(h_lat, h_ctx, w_in, sink, lb_f, lb_b, hgrn_g, w_branch, w_out, cos, sin, need_ctx_out):
    bsz, n, _ = h_lat.shape
    n_ctx = h_ctx.shape[1]
    lat = jnp.split(h_lat @ w_in, IN_SPLITS, axis=-1)
    ctx = jnp.split(h_ctx @ w_in, IN_SPLITS, axis=-1)

    def attn_heads(parts, m):
        return (parts[0].reshape(bsz, m, A_HEADS, A_HEAD_DIM),
                parts[1].reshape(bsz, m, A_KV_HEADS, A_HEAD_DIM),
                parts[2].reshape(bsz, m, A_KV_HEADS, A_HEAD_DIM))

    q_l, k_l, v_l = attn_heads(lat, n)
    q_c, k_c, v_c = attn_heads(ctx, n_ctx)
    q_l = apply_rope(q_l, cos, sin)
    k_l = apply_rope(k_l, cos, sin)
    ya_lat = windowed_gqa_with_sink(q_l, k_l, v_l, k_c, v_c, sink)

    def rec_heads(parts, m):
        def to_h(t, d):
            return t.reshape(bsz, m, B_HEADS, d).transpose(0, 2, 1, 3)
        return (jax.nn.silu(to_h(parts[3], B_DK)), to_h(parts[4], B_DK), to_h(parts[5], B_DK),
                to_h(parts[6], B_DV), parts[7])

    rq_l, zf_l, zb_l, rv_l, rg_l = rec_heads(lat, n)
    rq_c, zf_c, zb_c, rv_c, rg_c = rec_heads(ctx, n_ctx)
    of_l, of_c = hgrn2_direction(rq_c, zf_c, rv_c, rq_l, zf_l, rv_l, lb_f, False)
    ob_l, ob_c = hgrn2_direction(rq_c, zb_c, rv_c, rq_l, zb_l, rv_l, lb_b, True)

    def rec_out(o, g, m):
        o = rms_norm(o.transpose(0, 2, 1, 3), hgrn_g.reshape(B_HEADS, B_DV))
        return o.reshape(bsz, m, B_WIDTH).astype(g.dtype) * jax.nn.silu(g)

    yb_lat = rec_out(of_l + ob_l, rg_l, n)

    def merge(ya, yb, parts):
        z = jnp.einsum('rbnc,rcd->rbnd', jnp.stack([ya, yb.astype(ya.dtype)]), w_branch)
        merged = jax.nn.sigmoid(parts[8]) * z[0] + jax.nn.sigmoid(parts[9]) * z[1]
        return merged @ w_out

    y_lat = merge(ya_lat, yb_lat, lat)
    if need_ctx_out:
        ya_c = context_attention(q_c, k_c, v_c, sink)
        yb_c = rec_out(of_c + ob_c, rg_c, n_ctx)
        y_ctx = merge(ya_c, yb_c, ctx)
    else:
        y_ctx = None
    return y_lat, y_ctx


def moe_ffn(h, w_router, b_router, w_e_gate, b_e_gate, w_e_up, b_e_up, w_e_down, b_e_down):
    bsz, n, d = h.shape
    xt = h.reshape(-1, d)
    t = xt.shape[0]
    logits = (xt @ w_router + b_router).astype(jnp.float32)
    top_val, top_idx = lax.top_k(logits, TOP_K)
    gates = jax.nn.softmax(top_val, axis=-1)
    n_assign = t * TOP_K
    flat_e = top_idx.reshape(-1)
    order = jnp.argsort(flat_e)
    e_sorted = flat_e[order]
    tok_sorted = (order // TOP_K).astype(jnp.int32)
    gate_sorted = gates.reshape(-1)[order]
    counts = jnp.zeros((N_EXPERTS,), jnp.int32).at[flat_e].add(1)
    padded = ((counts + MOE_BLOCK - 1) // MOE_BLOCK) * MOE_BLOCK
    start = jnp.cumsum(counts) - counts
    pend = jnp.cumsum(padded)
    pstart = pend - padded
    dest = pstart[e_sorted] + (jnp.arange(n_assign, dtype=jnp.int32) - start[e_sorted])
    n_blocks = -(-n_assign // MOE_BLOCK) + N_EXPERTS
    n_slots = n_blocks * MOE_BLOCK
    slot_tok = jnp.full((n_slots,), t, jnp.int32).at[dest].set(tok_sorted)
    slot_gate = jnp.zeros((n_slots,), jnp.float32).at[dest].set(gate_sorted)
    block_expert = jnp.minimum(jnp.searchsorted(pend, jnp.arange(n_blocks) * MOE_BLOCK, side='right'),
                               N_EXPERTS - 1)
    x_pad = jnp.concatenate([xt, jnp.zeros((1, d), xt.dtype)], axis=0)

    def one_block(args):
        tok_b, gate_b, e = args
        xb = x_pad[tok_b]
        g = jnp.minimum(xb @ w_e_gate[e] + b_e_gate[e], SWIGLU_LIMIT)
        u = jnp.clip(xb @ w_e_up[e] + b_e_up[e], -SWIGLU_LIMIT, SWIGLU_LIMIT)
        act = g * jax.nn.sigmoid(SWIGLU_ALPHA * g) * (u + 1.0)
        y = act @ w_e_down[e] + b_e_down[e]
        return y * gate_b[:, None].astype(y.dtype)

    yb = lax.map(one_block, (slot_tok.reshape(n_blocks, MOE_BLOCK), slot_gate.reshape(n_blocks, MOE_BLOCK),
                             block_expert))
    y = jax.ops.segment_sum(yb.reshape(-1, d), slot_tok, num_segments=t + 1)[:t]
    return y.reshape(bsz, n, d)


def setup_inputs(seed: int = 0) -> dict:
    key = jax.random.key(seed)
    ks = jax.random.split(key, 24)
    nrm = jax.random.normal
    f32 = jnp.float32
    D = D_MODEL
    return {
        "x": nrm(ks[0], (BATCH, SEQ, D), f32),
        "c": nrm(ks[1], (BATCH, D), f32),
        "ctx": nrm(ks[2], (BATCH, CTX_LEN, D), f32),
        "c_ctx": nrm(ks[3], (D,), f32),
        "norm_mix_g": 1.0 + 0.01 * nrm(ks[4], (DEPTH, D), f32),
        "norm_ffn_g": 1.0 + 0.01 * nrm(ks[5], (DEPTH, D), f32),
        "w_mod": nrm(ks[6], (DEPTH, D, 6 * D), f32) * (0.3 * D ** -0.5),
        "b_mod": 0.01 * nrm(ks[7], (DEPTH, 6 * D), f32),
        "w_in": nrm(ks[8], (DEPTH, D, IN_COLS), f32) * D ** -0.5,
        "attn_sinks": nrm(ks[9], (DEPTH, A_HEADS), f32),
        "lb_fwd_logits": nrm(ks[10], (DEPTH + 1, B_KEY_WIDTH), f32),
        "lb_bwd_logits": nrm(ks[11], (DEPTH + 1, B_KEY_WIDTH), f32),
        "hgrn_norm_g": 1.0 + 0.01 * nrm(ks[12], (DEPTH, B_WIDTH), f32),
        "w_branch": nrm(ks[13], (DEPTH, N_BRANCHES, BRANCH_WIDTH, D), f32) * BRANCH_WIDTH ** -0.5,
        "w_out": nrm(ks[14], (DEPTH, D, D), f32) * D ** -0.5,
        "w_router": nrm(ks[15], (DEPTH, D, N_EXPERTS), f32) * D ** -0.5,
        "b_router": 0.01 * nrm(ks[16], (DEPTH, N_EXPERTS), f32),
        "w_e_gate": nrm(ks[17], (DEPTH, N_EXPERTS, D, EXPERT_FF), f32) * D ** -0.5,
        "b_e_gate": 0.01 * nrm(ks[18], (DEPTH, N_EXPERTS, EXPERT_FF), f32),
        "w_e_up": nrm(ks[19], (DEPTH, N_EXPERTS, D, EXPERT_FF), f32) * D ** -0.5,
        "b_e_up": 0.01 * nrm(ks[20], (DEPTH, N_EXPERTS, EXPERT_FF), f32),
        "w_e_down": nrm(ks[21], (DEPTH, N_EXPERTS, EXPERT_FF, D), f32) * EXPERT_FF ** -0.5,
        "b_e_down": 0.01 * nrm(ks[22], (DEPTH, N_EXPERTS, D), f32),
        "final_norm_g": 1.0 + 0.01 * nrm(ks[23], (D,), f32),
    }


def reference(x, c, ctx, c_ctx, norm_mix_g, norm_ffn_g, w_mod, b_mod, w_in, attn_sinks,
              lb_fwd_logits, lb_bwd_logits, hgrn_norm_g, w_branch, w_out, w_router, b_router,
              w_e_gate, b_e_gate, w_e_up, b_e_up, w_e_down, b_e_down, final_norm_g):
    bsz, n, _ = x.shape
    cos, sin = axial_rope_tables(n)
    lb_f_all = jnp.cumsum(jax.nn.softmax(lb_fwd_logits.astype(jnp.float32), axis=0), axis=0)
    lb_b_all = jnp.cumsum(jax.nn.softmax(lb_bwd_logits.astype(jnp.float32), axis=0), axis=0)
    for l in range(DEPTH):
        last = l == DEPTH - 1
        mod_lat = (jax.nn.silu(c) @ w_mod[l] + b_mod[l]).reshape(bsz, 6, 1, D_MODEL)
        mod_ctx = (jax.nn.silu(c_ctx) @ w_mod[l] + b_mod[l]).reshape(6, 1, 1, D_MODEL)
        h_l = modulate(rms_norm(x, norm_mix_g[l]), mod_lat[:, 0], mod_lat[:, 1])
        h_c = modulate(rms_norm(ctx, norm_mix_g[l]), mod_ctx[0], mod_ctx[1])
        y_l, y_c = hybrid_mixer(h_l, h_c, w_in[l], attn_sinks[l], lb_f_all[l], lb_b_all[l], hgrn_norm_g[l],
                                w_branch[l], w_out[l], cos, sin, not last)
        x = x + mod_lat[:, 2] * y_l
        h2 = modulate(rms_norm(x, norm_ffn_g[l]), mod_lat[:, 3], mod_lat[:, 4])
        x = x + mod_lat[:, 5] * moe_ffn(h2, w_router[l], b_router[l], w_e_gate[l], b_e_gate[l],
                                        w_e_up[l], b_e_up[l], w_e_down[l], b_e_down[l])
        if not last:
            ctx = ctx + mod_ctx[2] * y_c
            h2c = modulate(rms_norm(ctx, norm_ffn_g[l]), mod_ctx[3], mod_ctx[4])
            ctx = ctx + mod_ctx[5] * moe_ffn(h2c, w_router[l], b_router[l], w_e_gate[l], b_e_gate[l],
                                             w_e_up[l], b_e_up[l], w_e_down[l], b_e_down[l])
    return rms_norm(x, final_norm_g)
```

```python
import functools

import jax
import jax.numpy as jnp
from jax import lax
from jax.experimental import pallas as pl
from jax.experimental.pallas import tpu as pltpu

D_MODEL = 2048
GRID_W = 64
EPS = 1e-6
A_HEADS = 16
A_KV_HEADS = 2
A_GROUP = A_HEADS // A_KV_HEADS
A_HEAD_DIM = 64
A_WIDTH = A_HEADS * A_HEAD_DIM
A_KV_WIDTH = A_KV_HEADS * A_HEAD_DIM
WINDOW = 128
A_BLOCK = 128
ROPE_BASE = 10000.0
ROPE_PAIRS_PER_AXIS = A_HEAD_DIM // 4
B_HEADS = 8
B_DK = 128
B_DV = 128
B_WIDTH = B_HEADS * B_DV
N_EXPERTS = 32
TOP_K = 4
EXPERT_FF = 2048
SWIGLU_LIMIT = 7.0
SWIGLU_ALPHA = 1.702

LANES = 128
SUB = 8
HG_CHUNK = 64
HG_BLOCK = 256
MOE_TM = 256
NEG = -1e30
VMEM_LIMIT = 56 * 1024 * 1024

_f32 = jnp.float32
_bf16 = jnp.bfloat16


def _cparams(sem):
    return pltpu.CompilerParams(dimension_semantics=sem, vmem_limit_bytes=VMEM_LIMIT)


def _sigmoid(x):
    return 1.0 / (1.0 + jnp.exp(-x))


def _mod_kernel(s_ref, w_ref, b_ref, o_ref):
    tn = w_ref.shape[1]
    for r in range(2):
        s = s_ref[r]
        s = s * _sigmoid(s)
        for j in range(tn // LANES):
            sl = slice(j * LANES, (j + 1) * LANES)
            acc = jnp.sum(w_ref[:, sl] * s, axis=0, keepdims=True)
            o_ref[r:r + 1, sl] = acc + b_ref[:, sl]


def _mod_vectors(c, c_ctx, w_mod, b_mod):
    d, n_out = w_mod.shape
    tn = 1024
    s = jnp.stack([c.reshape(d), c_ctx.reshape(d)]).astype(_f32)
    s = jnp.broadcast_to(s[:, :, None], (2, d, LANES))
    return pl.pallas_call(
        _mod_kernel,
        out_shape=jax.ShapeDtypeStruct((2, n_out), _f32),
        grid=(n_out // tn,),
        in_specs=[pl.BlockSpec((2, d, LANES), lambda j: (0, 0, 0)),
                  pl.BlockSpec((d, tn), lambda j: (0, j)),
                  pl.BlockSpec((1, tn), lambda j: (0, j))],
        out_specs=pl.BlockSpec((2, tn), lambda j: (0, j)),
        compiler_params=_cparams(("arbitrary",)),
        name="mod_vectors",
    )(s, w_mod, b_mod.reshape(1, n_out))


def _rms_mod(xf, g, shift, scale):
    y = xf * lax.rsqrt(jnp.mean(xf * xf, axis=-1, keepdims=True) + EPS)
    return (y * g) * (1.0 + scale) + shift


def _norm_kernel(x_ref, ctx_ref, g_ref, mod_ref, o_ref, *, n_ctx_tiles):
    i = pl.program_id(0)
    d = x_ref.shape[1]

    @pl.when(i < n_ctx_tiles)
    def _():
        o_ref[...] = _rms_mod(ctx_ref[...], g_ref[...], mod_ref[1:2, 0:d],
                              mod_ref[1:2, d:2 * d]).astype(o_ref.dtype)

    @pl.when(i >= n_ctx_tiles)
    def _():
        o_ref[...] = _rms_mod(x_ref[...], g_ref[...], mod_ref[0:1, 0:d],
                              mod_ref[0:1, d:2 * d]).astype(o_ref.dtype)


def _norm_all(x2, ctx2, g, mod):
    n, d = x2.shape
    l = ctx2.shape[0]
    tm = 256
    nct = l // tm
    return pl.pallas_call(
        functools.partial(_norm_kernel, n_ctx_tiles=nct),
        out_shape=jax.ShapeDtypeStruct((l + n, d), _bf16),
        grid=((l + n) // tm,),
        in_specs=[pl.BlockSpec((tm, d), lambda i: (jnp.maximum(i - nct, 0), 0)),
                  pl.BlockSpec((tm, d), lambda i: (jnp.minimum(i, nct - 1), 0)),
                  pl.BlockSpec((1, d), lambda i: (0, 0)),
                  pl.BlockSpec(mod.shape, lambda i: (0, 0))],
        out_specs=pl.BlockSpec((tm, d), lambda i: (i, 0)),
        compiler_params=_cparams(("arbitrary",)),
        name="adaln_norm",
    )(x2, ctx2, g.reshape(1, d), mod)


def _rope(a, cos, sin):
    lane = lax.broadcasted_iota(jnp.int32, a.shape, 1)
    first = (lane % A_HEAD_DIM) < (A_HEAD_DIM // 2)
    rot = jnp.where(first, pltpu.roll(a, LANES - 32, 1), pltpu.roll(a, 32, 1))
    return a * cos + rot * sin


def _proj_qkv_kernel(h_ref, w_ref, cos_ref, sin_ref, q_ref, k_ref, v_ref):
    acc = jnp.dot(h_ref[...], w_ref[...], preferred_element_type=_f32)
    cos = cos_ref[...]
    sin = sin_ref[...]
    scale = A_HEAD_DIM ** -0.5
    for j in range(A_WIDTH // LANES):
        sl = slice(j * LANES, (j + 1) * LANES)
        q_ref[:, sl] = (_rope(acc[:, sl], cos, sin) * scale).astype(q_ref.dtype)
    k_ref[...] = _rope(acc[:, A_WIDTH:A_WIDTH + LANES], cos, sin).astype(k_ref.dtype)
    v_ref[...] = acc[:, A_WIDTH + LANES:A_WIDTH + 2 * LANES].astype(v_ref.dtype)


def _proj_heads_kernel(h_ref, w_ref, o_ref, *, act):
    acc = jnp.dot(h_ref[...], w_ref[...], preferred_element_type=_f32)
    if act == "silu":
        acc = acc * _sigmoid(acc)
    for hh in range(o_ref.shape[0]):
        o_ref[hh] = acc[:, hh * LANES:(hh + 1) * LANES].astype(o_ref.dtype)


def _proj_gate_kernel(h_ref, w_ref, lb_ref, lf_ref, kk_ref):
    z = jnp.dot(h_ref[...], w_ref[0], preferred_element_type=_f32)
    lb = lb_ref[0]
    sg = _sigmoid(z)
    logf = jnp.log(lb + (1.0 - lb) * sg)
    kk = (1.0 - lb) * _sigmoid(-z)
    for hh in range(lf_ref.shape[1]):
        sl = slice(hh * LANES, (hh + 1) * LANES)
        lf_ref[0, hh] = logf[:, sl]
        kk_ref[0, hh] = kk[:, sl].astype(kk_ref.dtype)


def _proj_sigmoid_kernel(h_ref, w_ref, o_ref):
    acc = jnp.dot(h_ref[...], w_ref[...], preferred_element_type=_f32)
    o_ref[...] = _sigmoid(acc).astype(o_ref.dtype)


def _row_tile(t):
    for tm in (1280, 640, 256, 128):
        if t % tm == 0:
            return tm
    raise ValueError(t)


def _proj_qkv(h, w, cos_t, sin_t):
    t, d = h.shape
    tm = _row_tile(t)
    ncol = w.shape[1]
    return pl.pallas_call(
        _proj_qkv_kernel,
        out_shape=(jax.ShapeDtypeStruct((t, A_WIDTH), _bf16),
                   jax.ShapeDtypeStruct((t, A_KV_WIDTH), _bf16),
                   jax.ShapeDtypeStruct((t, A_KV_WIDTH), _bf16)),
        grid=(t // tm,),
        in_specs=[pl.BlockSpec((tm, d), lambda i: (i, 0)),
                  pl.BlockSpec((d, ncol), lambda i: (0, 0)),
                  pl.BlockSpec((tm, LANES), lambda i: (i, 0)),
                  pl.BlockSpec((tm, LANES), lambda i: (i, 0))],
        out_specs=(pl.BlockSpec((tm, A_WIDTH), lambda i: (i, 0)),
                   pl.BlockSpec((tm, A_KV_WIDTH), lambda i: (i, 0)),
                   pl.BlockSpec((tm, A_KV_WIDTH), lambda i: (i, 0))),
        compiler_params=_cparams(("arbitrary",)),
        name="proj_qkv",
    )(h, w, cos_t, sin_t)


def _proj_heads(h, w, act):
    t, d = h.shape
    tm = _row_tile(t)
    tn = 512
    nh = tn // LANES
    return pl.pallas_call(
        functools.partial(_proj_heads_kernel, act=act),
        out_shape=jax.ShapeDtypeStruct((w.shape[1] // LANES, t, LANES), _bf16),
        grid=(w.shape[1] // tn, t // tm),
        in_specs=[pl.BlockSpec((tm, d), lambda j, i: (i, 0)),
                  pl.BlockSpec((d, tn), lambda j, i: (0, j))],
        out_specs=pl.BlockSpec((nh, tm, LANES), lambda j, i: (j, i, 0)),
        compiler_params=_cparams(("arbitrary", "arbitrary")),
        name="proj_heads_" + act,
    )(h, w)


def _proj_gate(h, w2, lb2):
    t, d = h.shape
    tm = _row_tile(t)
    tn = 512
    nh = tn // LANES
    ncb = w2.shape[2] // tn
    out_sds = lambda dt: jax.ShapeDtypeStruct((2, w2.shape[2] // LANES, t, LANES), dt)
    return pl.pallas_call(
        _proj_gate_kernel,
        out_shape=(out_sds(_f32), out_sds(_bf16)),
        grid=(2, ncb, t // tm),
        in_specs=[pl.BlockSpec((tm, d), lambda r, j, i: (i, 0)),
                  pl.BlockSpec((1, d, tn), lambda r, j, i: (r, 0, j)),
                  pl.BlockSpec((1, 1, tn), lambda r, j, i: (r, 0, j))],
        out_specs=(pl.BlockSpec((1, nh, tm, LANES), lambda r, j, i: (r, j, i, 0)),
                   pl.BlockSpec((1, nh, tm, LANES), lambda r, j, i: (r, j, i, 0))),
        compiler_params=_cparams(("arbitrary", "arbitrary", "arbitrary")),
        name="proj_gate",
    )(h, w2, lb2)


def _proj_sigmoid(h, w):
    t, d = h.shape
    tm = _row_tile(t)
    tn = 512
    return pl.pallas_call(
        _proj_sigmoid_kernel,
        out_shape=jax.ShapeDtypeStruct((t, w.shape[1]), _bf16),
        grid=(w.shape[1] // tn, t // tm),
        in_specs=[pl.BlockSpec((tm, d), lambda j, i: (i, 0)),
                  pl.BlockSpec((d, tn), lambda j, i: (0, j))],
        out_specs=pl.BlockSpec((tm, tn), lambda j, i: (i, j)),
        compiler_params=_cparams(("arbitrary", "arbitrary")),
        name="proj_sigmoid",
    )(h, w)


def _attn_kernel(q_ref, kp_ref, kc_ref, kn_ref, vp_ref, vc_ref, vn_ref,
                 kx_ref, vx_ref, bias_ref, sink_ref, o_ref):
    hd = A_HEAD_DIM
    nt = (((1,), (1,)), ((), ()))
    for g in range(A_KV_HEADS):
        gs = slice(g * hd, (g + 1) * hd)
        qg = jnp.concatenate(
            [q_ref[:, (g * A_GROUP + h) * hd:(g * A_GROUP + h + 1) * hd] for h in range(A_GROUP)],
            axis=0)
        kb = jnp.concatenate([kp_ref[:, gs], kc_ref[:, gs], kn_ref[:, gs]], axis=0)
        vb = jnp.concatenate([vp_ref[:, gs], vc_ref[:, gs], vn_ref[:, gs]], axis=0)
        s_ctx = lax.dot_general(qg, kx_ref[:, gs], nt, preferred_element_type=_f32)
        s_lat = lax.dot_general(qg, kb, nt, preferred_element_type=_f32) + bias_ref[0]
        sink = sink_ref[g]
        m = jnp.maximum(jnp.maximum(jnp.max(s_ctx, axis=-1, keepdims=True),
                                    jnp.max(s_lat, axis=-1, keepdims=True)), sink)
        p_ctx = jnp.exp(s_ctx - m)
        p_lat = jnp.exp(s_lat - m)
        denom = (jnp.sum(p_ctx, axis=-1, keepdims=True) + jnp.sum(p_lat, axis=-1, keepdims=True)
                 + jnp.exp(sink - m))
        o = (jnp.dot(p_ctx.astype(_bf16), vx_ref[:, gs], preferred_element_type=_f32)
             + jnp.dot(p_lat.astype(_bf16), vb, preferred_element_type=_f32))
        o = o / denom
        for h in range(A_GROUP):
            c0 = (g * A_GROUP + h) * hd
            o_ref[:, c0:c0 + hd] = o[h * A_BLOCK:(h + 1) * A_BLOCK, :].astype(o_ref.dtype)


def _attention(q, k, v, sinks, n, l):
    blk = A_BLOCK
    nblk = n // blk
    off = l // blk
    rows = A_GROUP * blk
    r = jnp.arange(rows) % blk
    cidx = jnp.arange(3 * blk)
    rel = cidx[None, :] - blk - r[:, None]
    win = jnp.abs(rel) <= WINDOW
    part = cidx // blk
    variants = [win & (part != 0)[None, :], win, win & (part != 2)[None, :]]
    bias = jnp.stack([jnp.where(m, 0.0, NEG).astype(_f32) for m in variants])
    sink_col = jnp.repeat(sinks.astype(_f32).reshape(A_KV_HEADS, A_GROUP), blk, axis=1)
    sink_col = sink_col.reshape(A_KV_HEADS, rows, 1)

    def bias_idx(i):
        return (jnp.where(i == 0, 0, jnp.where(i == nblk - 1, 2, 1)), 0, 0)

    kv_spec = lambda f: pl.BlockSpec((blk, A_KV_WIDTH), f)
    prev = lambda i: (jnp.maximum(i - 1, 0) + off, 0)
    cur = lambda i: (i + off, 0)
    nxt = lambda i: (jnp.minimum(i + 1, nblk - 1) + off, 0)
    return pl.pallas_call(
        _attn_kernel,
        out_shape=jax.ShapeDtypeStruct((n, A_WIDTH), _bf16),
        grid=(nblk,),
        in_specs=[pl.BlockSpec((blk, A_WIDTH), cur),
                  kv_spec(prev), kv_spec(cur), kv_spec(nxt),
                  kv_spec(prev), kv_spec(cur), kv_spec(nxt),
                  pl.BlockSpec((l, A_KV_WIDTH), lambda i: (0, 0)),
                  pl.BlockSpec((l, A_KV_WIDTH), lambda i: (0, 0)),
                  pl.BlockSpec((1, rows, 3 * blk), bias_idx),
                  pl.BlockSpec((A_KV_HEADS, rows, 1), lambda i: (0, 0, 0))],
        out_specs=pl.BlockSpec((blk, A_WIDTH), lambda i: (i, 0)),
        compiler_params=_cparams(("arbitrary",)),
        name="window_attention",
    )(q, k, k, k, v, v, v, k, v, bias, sink_col)


def _split3(x):
    hi = x.astype(_bf16)
    r1 = x - hi.astype(_f32)
    mid = r1.astype(_bf16)
    lo = (r1 - mid.astype(_f32)).astype(_bf16)
    return hi, mid, lo


def _hgrn_chunk(q, k, v, lf, st, tri, ones, reverse):
    c = HG_CHUNK
    nsb = c // SUB
    nt = (((1,), (1,)), ((), ()))
    tn = (((0,), (0,)), ((), ()))
    hi, mid, lo = _split3(lf)
    b = (jnp.dot(tri, hi, preferred_element_type=_f32)
         + jnp.dot(tri, mid, preferred_element_type=_f32)
         + jnp.dot(tri, lo, preferred_element_type=_f32))
    qb = (q * jnp.exp(b)).astype(_bf16)
    o = lax.dot_general(qb, st.astype(_bf16), nt, preferred_element_type=_f32)
    b_tot = b[0:1, :] if reverse else b[c - 1:c, :]
    kend = (k * jnp.exp(b_tot - b)).astype(_bf16)

    lane = lax.broadcasted_iota(jnp.int32, (SUB, LANES), 1)
    row = lax.broadcasted_iota(jnp.int32, (SUB, LANES), 0)
    zero_row = jnp.zeros((1, LANES), _f32)
    diag_tmp = []
    off_rows = []
    for ib in range(nsb):
        r0 = ib * SUB
        bi = b[r0:r0 + SUB]
        qi = q[r0:r0 + SUB]
        for s in range(SUB):
            bs = b[r0 + s:r0 + s + 1]
            ks = k[r0 + s:r0 + s + 1]
            dec = jnp.exp(jnp.minimum(bi - bs, 0.0))
            diag_tmp.append((qi * ks * dec).astype(_bf16))
        if reverse:
            lo_r, hi_r = r0 + SUB, c
            ref_b = b[r0 + SUB:r0 + SUB + 1] if ib < nsb - 1 else zero_row
        else:
            lo_r, hi_r = 0, r0
            ref_b = b[r0 - 1:r0] if ib > 0 else zero_row
        if hi_r > lo_r:
            qp = (qi * jnp.exp(bi - ref_b)).astype(_bf16)
            kp = (k[lo_r:hi_r] * jnp.exp(ref_b - b[lo_r:hi_r])).astype(_bf16)
            pieces = []
            if lo_r > 0:
                pieces.append(jnp.zeros((lo_r, LANES), _bf16))
            pieces.append(kp)
            if hi_r < c:
                pieces.append(jnp.zeros((c - hi_r, LANES), _bf16))
            kp_full = jnp.concatenate(pieces, axis=0) if len(pieces) > 1 else pieces[0]
            off_rows.append(lax.dot_general(qp, kp_full, nt, preferred_element_type=_f32))
        else:
            off_rows.append(jnp.zeros((SUB, c), _f32))
    rsum = jnp.dot(jnp.concatenate(diag_tmp, axis=0), ones, preferred_element_type=_f32)
    diag_rows = []
    for ib in range(nsb):
        acc = jnp.zeros((SUB, LANES), _f32)
        for s in range(SUB):
            idx = ib * SUB + s
            keep = (lane == idx) & ((row <= s) if reverse else (row >= s))
            acc = jnp.where(keep, rsum[idx * SUB:(idx + 1) * SUB], acc)
        diag_rows.append(acc[:, 0:c])
    a = jnp.concatenate(off_rows, axis=0) + jnp.concatenate(diag_rows, axis=0)
    o = o + jnp.dot(a.astype(_bf16), v, preferred_element_type=_f32)
    st_new = st * jnp.exp(b_tot) + lax.dot_general(v, kend, tn, preferred_element_type=_f32)
    return o, st_new


def _hgrn_kernel(q_ref, k_ref, v_ref, lf_ref, o_ref, st_ref, *, reverse):
    c = HG_CHUNK
    nch = HG_BLOCK // c
    nh = q_ref.shape[0]

    @pl.when(pl.program_id(0) == 0)
    def _():
        st_ref[...] = jnp.zeros(st_ref.shape, st_ref.dtype)

    ri = lax.broadcasted_iota(jnp.int32, (c, c), 0)
    ci = lax.broadcasted_iota(jnp.int32, (c, c), 1)
    tri = jnp.where((ci >= ri) if reverse else (ci <= ri), 1.0, 0.0).astype(_bf16)
    ones = jnp.ones((LANES, LANES), _bf16)

    def body(it, carry):
        cc = it // nh
        h = it % nh
        chunk = (nch - 1 - cc) if reverse else cc
        r0 = pl.multiple_of(chunk * c, c)
        q = q_ref[h, pl.ds(r0, c), :].astype(_f32)
        k = k_ref[0, h, pl.ds(r0, c), :].astype(_f32)
        v = v_ref[h, pl.ds(r0, c), :]
        lf = lf_ref[0, h, pl.ds(r0, c), :]
        o, st_new = _hgrn_chunk(q, k, v, lf, st_ref[h], tri, ones, reverse)
        o_ref[h, pl.ds(r0, c), :] = o.astype(o_ref.dtype)
        st_ref[h] = st_new
        return carry

    lax.fori_loop(0, nch * nh, body, 0)


def _hgrn_scan(rq, kk, rv, lf, direction, l):
    nh, t, _ = rq.shape
    br = HG_BLOCK
    nb = t // br
    nc = l // br
    reverse = direction == 1
    if reverse:
        def blk(s):
            return jnp.where(s < nc, nc - 1 - s, nb - 1 - (s - nc))
    else:
        def blk(s):
            return s
    spec3 = pl.BlockSpec((nh, br, LANES), lambda s: (0, blk(s), 0))
    spec4 = pl.BlockSpec((1, nh, br, LANES), lambda s: (direction, 0, blk(s), 0))
    return pl.pallas_call(
        functools.partial(_hgrn_kernel, reverse=reverse),
        out_shape=jax.ShapeDtypeStruct((nh, t, LANES), _bf16),
        grid=(nb,),
        in_specs=[spec3, spec4, spec3, spec4],
        out_specs=spec3,
        scratch_shapes=[pltpu.VMEM((nh, B_DV, B_DK), _f32)],
        compiler_params=_cparams(("arbitrary",)),
        name="hgrn2_scan_" + ("bwd" if reverse else "fwd"),
    )(rq, kk, rv, lf)


def _merge_kernel(ya_ref, of_ref, ob_ref, rg_ref, ga_ref, gb_ref, x_ref, wa_ref, wb_ref, wo_ref,
                  wrh_ref, wrl_ref, br_ref, hg_ref, g2_ref, mod_ref,
                  x1_ref, h2_ref, route_ref, cnt_ref, run_ref):
    i = pl.program_id(0)
    tm, d = x_ref.shape

    @pl.when(i == 0)
    def _():
        run_ref[...] = jnp.zeros(run_ref.shape, run_ref.dtype)

    parts = []
    for h in range(B_HEADS):
        o = of_ref[h].astype(_f32) + ob_ref[h].astype(_f32)
        o = o * lax.rsqrt(jnp.mean(o * o, axis=-1, keepdims=True) + EPS)
        o = o * hg_ref[:, h * LANES:(h + 1) * LANES]
        parts.append((o * rg_ref[h].astype(_f32)).astype(_bf16))
    yb = jnp.concatenate(parts, axis=1)
    z0 = jnp.dot(ya_ref[...], wa_ref[...], preferred_element_type=_f32)
    z1 = jnp.dot(yb, wb_ref[...], preferred_element_type=_f32)
    merged = ga_ref[...].astype(_f32) * z0 + gb_ref[...].astype(_f32) * z1
    y = jnp.dot(merged.astype(_bf16), wo_ref[...], preferred_element_type=_f32)
    x1 = x_ref[...] + mod_ref[0:1, 2 * d:3 * d] * y
    x1_ref[...] = x1
    h2 = _rms_mod(x1, g2_ref[...], mod_ref[0:1, 3 * d:4 * d], mod_ref[0:1, 4 * d:5 * d])
    h2_ref[...] = h2

    hh = h2.astype(_bf16)
    hl = (h2 - hh.astype(_f32)).astype(_bf16)
    logits = (jnp.dot(hh, wrh_ref[...], preferred_element_type=_f32)
              + jnp.dot(hl, wrh_ref[...], preferred_element_type=_f32)
              + jnp.dot(hh, wrl_ref[...], preferred_element_type=_f32)) + br_ref[...]
    lane = lax.broadcasted_iota(jnp.int32, (tm, LANES), 1)
    work = jnp.where(lane < N_EXPERTS, logits, NEG)
    vals, idxs = [], []
    onehot = jnp.zeros((tm, LANES), _f32)
    for _ in range(TOP_K):
        m = jnp.max(work, axis=-1, keepdims=True)
        idx = jnp.min(jnp.where(work == m, lane, LANES), axis=-1, keepdims=True)
        sel = lane == idx
        vals.append(m)
        idxs.append(idx)
        onehot = jnp.where(sel, 1.0, onehot)
        work = jnp.where(sel, NEG, work)
    es = [jnp.exp(vv - vals[0]) for vv in vals]
    tot = es[0] + es[1] + es[2] + es[3]
    ri = lax.broadcasted_iota(jnp.int32, (tm, tm), 0)
    ci = lax.broadcasted_iota(jnp.int32, (tm, tm), 1)
    ltri = jnp.where(ci < ri, 1.0, 0.0).astype(_bf16)
    prefix = jnp.dot(ltri, onehot.astype(_bf16), preferred_element_type=_f32) + run_ref[...]
    route = jnp.zeros((tm, LANES), _f32)
    for j in range(TOP_K):
        rank = jnp.sum(jnp.where(lane == idxs[j], prefix, 0.0), axis=-1, keepdims=True)
        route = jnp.where(lane == j, idxs[j].astype(_f32), route)
        route = jnp.where(lane == TOP_K + j, es[j] / tot, route)
        route = jnp.where(lane == 2 * TOP_K + j, rank, route)
    route_ref[...] = route
    run_new = run_ref[...] + jnp.sum(onehot, axis=0, keepdims=True)
    run_ref[...] = run_new
    cnt_ref[...] = run_new


def _merge_route(ya, o_f, o_b, rg, gates, x2, wa, wb, wo, wr_hi, wr_lo, b_router, hg, g2, mod, l):
    n, d = x2.shape
    tm = 256
    off = l // tm
    const = lambda shape: pl.BlockSpec(shape, lambda i: (0,) * len(shape),
                                       pipeline_mode=pl.Buffered(1))
    head_spec = pl.BlockSpec((B_HEADS, tm, LANES), lambda i: (0, i + off, 0))
    return pl.pallas_call(
        _merge_kernel,
        out_shape=(jax.ShapeDtypeStruct((n, d), _f32),
                   jax.ShapeDtypeStruct((n, d), _f32),
                   jax.ShapeDtypeStruct((n, LANES), _f32),
                   jax.ShapeDtypeStruct((1, LANES), _f32)),
        grid=(n // tm,),
        in_specs=[pl.BlockSpec((tm, A_WIDTH), lambda i: (i, 0)),
                  head_spec, head_spec, head_spec,
                  pl.BlockSpec((tm, d), lambda i: (i + off, 0)),
                  pl.BlockSpec((tm, d), lambda i: (i + off, 1)),
                  pl.BlockSpec((tm, d), lambda i: (i, 0)),
                  const(wa.shape), const(wb.shape), const(wo.shape),
                  const(wr_hi.shape), const(wr_lo.shape), const((1, LANES)),
                  const((1, B_WIDTH)), const((1, d)), const(mod.shape)],
        out_specs=(pl.BlockSpec((tm, d), lambda i: (i, 0)),
                   pl.BlockSpec((tm, d), lambda i: (i, 0)),
                   pl.BlockSpec((tm, LANES), lambda i: (i, 0)),
                   pl.BlockSpec((1, LANES), lambda i: (0, 0))),
        scratch_shapes=[pltpu.VMEM((1, LANES), _f32)],
        compiler_params=_cparams(("arbitrary",)),
        name="merge_route",
    )(ya, o_f, o_b, rg, gates, gates, x2, wa, wb, wo, wr_hi, wr_lo, b_router, hg, g2, mod)


def _dispatch_kernel(dest_ref, h2_ref, xs_in_ref, xs_ref, sem):
    del xs_in_ref
    tm = h2_ref.shape[0]

    def issue(t, carry):
        for j in range(TOP_K):
            dst = dest_ref[t * TOP_K + j]
            pltpu.make_async_copy(h2_ref.at[pl.ds(t, 1)], xs_ref.at[pl.ds(dst, 1)], sem).start()
        return carry

    lax.fori_loop(0, tm, issue, 0)
    for _ in range(TOP_K):
        pltpu.make_async_copy(h2_ref, xs_ref.at[pl.ds(0, tm)], sem).wait()


def _dispatch(h2, dest_flat, n_slots):
    n, d = h2.shape
    tm = 256
    xs0 = jnp.zeros((n_slots, d), h2.dtype)
    return pl.pallas_call(
        _dispatch_kernel,
        out_shape=jax.ShapeDtypeStruct((n_slots, d), h2.dtype),
        grid=(n // tm,),
        in_specs=[pl.BlockSpec((tm * TOP_K,), lambda i: (i,), memory_space=pltpu.SMEM),
                  pl.BlockSpec((tm, d), lambda i: (i, 0)),
                  pl.BlockSpec(memory_space=pl.ANY)],
        out_specs=pl.BlockSpec(memory_space=pl.ANY),
        scratch_shapes=[pltpu.SemaphoreType.DMA],
        input_output_aliases={2: 0},
        compiler_params=_cparams(("arbitrary",)),
        name="moe_dispatch",
    )(dest_flat, h2, xs0)


def _moe_up_kernel(ie_ref, ic_ref, it_ref, x_ref, wg_ref, wu_ref, bg_ref, bu_ref, a_ref):
    del ie_ref, ic_ref, it_ref
    x = x_ref[...].astype(_bf16)
    g = jnp.dot(x, wg_ref[0], preferred_element_type=_f32) + bg_ref[0]
    u = jnp.dot(x, wu_ref[0], preferred_element_type=_f32) + bu_ref[0]
    g = jnp.minimum(g, SWIGLU_LIMIT)
    u = jnp.clip(u, -SWIGLU_LIMIT, SWIGLU_LIMIT)
    a_ref[...] = (g * _sigmoid(SWIGLU_ALPHA * g) * (u + 1.0)).astype(a_ref.dtype)


def _moe_up(xs, wg, wu, bg, bu, item_e, item_c, item_t):
    n_slots, d = xs.shape
    ff = wg.shape[2]
    fc = ff // 2
    tm = MOE_TM
    n_items = item_e.shape[0]
    w_spec = pl.BlockSpec((1, d, fc), lambda i, ie, ic, it: (ie[i], 0, ic[i]))
    b_spec = pl.BlockSpec((1, 1, fc), lambda i, ie, ic, it: (ie[i], 0, ic[i]))
    return pl.pallas_call(
        _moe_up_kernel,
        out_shape=jax.ShapeDtypeStruct((n_slots, ff), _bf16),
        grid_spec=pltpu.PrefetchScalarGridSpec(
            num_scalar_prefetch=3,
            grid=(n_items,),
            in_specs=[pl.BlockSpec((tm, d), lambda i, ie, ic, it: (it[i], 0)),
                      w_spec, w_spec, b_spec, b_spec],
            out_specs=pl.BlockSpec((tm, fc), lambda i, ie, ic, it: (it[i], ic[i]))),
        compiler_params=_cparams(("arbitrary",)),
        name="moe_gate_up",
    )(item_e, item_c, item_t, xs, wg, wu, bg, bu)


def _moe_down_kernel(te_ref, a_ref, wd_ref, bd_ref, y_ref):
    del te_ref
    y_ref[...] = jnp.dot(a_ref[...], wd_ref[0], preferred_element_type=_f32) + bd_ref[0]


def _moe_down(act, wd, bd, tile_e):
    n_slots, ff = act.shape
    d = wd.shape[2]
    tm = MOE_TM
    return pl.pallas_call(
        _moe_down_kernel,
        out_shape=jax.ShapeDtypeStruct((n_slots, d), _f32),
        grid_spec=pltpu.PrefetchScalarGridSpec(
            num_scalar_prefetch=1,
            grid=(n_slots // tm,),
            in_specs=[pl.BlockSpec((tm, ff), lambda t, te: (t, 0)),
                      pl.BlockSpec((1, ff, d), lambda t, te: (te[t], 0, 0)),
                      pl.BlockSpec((1, 1, d), lambda t, te: (te[t], 0, 0))],
            out_specs=pl.BlockSpec((tm, d), lambda t, te: (t, 0))),
        compiler_params=_cparams(("arbitrary",)),
        name="moe_down",
    )(tile_e, act, wd, bd)


def _combine_kernel(dest_ref, ys_ref, route_ref, x1_ref, mod_ref, gf_ref, o_ref, buf_ref, sem):
    tm, d = x1_ref.shape

    def issue(t, carry):
        for j in range(TOP_K):
            src = dest_ref[t * TOP_K + j]
            pltpu.make_async_copy(ys_ref.at[pl.ds(src, 1)], buf_ref.at[j, pl.ds(t, 1)], sem).start()
        return carry

    lax.fori_loop(0, tm, issue, 0)
    for j in range(TOP_K):
        pltpu.make_async_copy(ys_ref.at[pl.ds(0, tm)], buf_ref.at[j], sem).wait()
    acc = jnp.zeros((tm, d), _f32)
    for j in range(TOP_K):
        acc = acc + route_ref[:, TOP_K + j:TOP_K + j + 1] * buf_ref[j]
    x2 = x1_ref[...] + mod_ref[0:1, 5 * d:6 * d] * acc
    y = x2 * lax.rsqrt(jnp.mean(x2 * x2, axis=-1, keepdims=True) + EPS)
    o_ref[...] = y * gf_ref[...]


def _combine(ys, dest_flat, route, x1, mod, gf):
    n, d = x1.shape
    tm = 256
    return pl.pallas_call(
        _combine_kernel,
        out_shape=jax.ShapeDtypeStruct((n, d), _f32),
        grid=(n // tm,),
        in_specs=[pl.BlockSpec((tm * TOP_K,), lambda i: (i,), memory_space=pltpu.SMEM),
                  pl.BlockSpec(memory_space=pl.ANY),
                  pl.BlockSpec((tm, LANES), lambda i: (i, 0)),
                  pl.BlockSpec((tm, d), lambda i: (i, 0)),
                  pl.BlockSpec(mod.shape, lambda i: (0, 0)),
                  pl.BlockSpec((1, d), lambda i: (0, 0))],
        out_specs=pl.BlockSpec((tm, d), lambda i: (i, 0)),
        scratch_shapes=[pltpu.VMEM((TOP_K, tm, d), _f32), pltpu.SemaphoreType.DMA],
        compiler_params=_cparams(("arbitrary",)),
        name="moe_combine",
    )(dest_flat, ys, route, x1, mod, gf.reshape(1, d))


def _rope_tables(n, l):
    rows = n // GRID_W
    row = jnp.repeat(jnp.arange(rows, dtype=_f32), GRID_W)
    col = jnp.tile(jnp.arange(GRID_W, dtype=_f32), rows)
    inv_freq = ROPE_BASE ** (-jnp.arange(ROPE_PAIRS_PER_AXIS, dtype=_f32) / ROPE_PAIRS_PER_AXIS)
    ang = jnp.concatenate([row[:, None] * inv_freq, col[:, None] * inv_freq], axis=-1)
    cos, sin = jnp.cos(ang), jnp.sin(ang)
    cos_t = jnp.tile(cos, (1, LANES // 32))
    sin_t = jnp.tile(jnp.concatenate([-sin, sin], axis=-1), (1, LANES // A_HEAD_DIM))
    cos_t = jnp.concatenate([jnp.ones((l, LANES), _f32), cos_t], axis=0)
    sin_t = jnp.concatenate([jnp.zeros((l, LANES), _f32), sin_t], axis=0)
    return cos_t, sin_t


def _routing_tables(route, counts, n_tiles):
    tm = MOE_TM
    idx = route[:, 0:TOP_K].astype(jnp.int32)
    rank = route[:, 2 * TOP_K:3 * TOP_K].astype(jnp.int32)
    cnt = counts[0, :N_EXPERTS].astype(jnp.int32)
    ntile = (cnt + tm - 1) // tm
    tend = jnp.cumsum(ntile)
    tstart = tend - ntile
    dest = (tstart * tm)[idx] + rank
    tiles = jnp.arange(n_tiles, dtype=jnp.int32)
    e_raw = jnp.searchsorted(tend, tiles, side="right").astype(jnp.int32)
    tile_e = jnp.minimum(e_raw, N_EXPERTS - 1)
    tstart_x = jnp.concatenate([tstart, tend[-1:]])
    ntile_x = jnp.concatenate([ntile, (n_tiles - tend[-1])[None]])
    base = tstart_x[e_raw]
    pos0 = 2 * base + (tiles - base)
    pos1 = pos0 + ntile_x[e_raw]
    pos = jnp.concatenate([pos0, pos1])
    item_t = jnp.zeros((2 * n_tiles,), jnp.int32).at[pos].set(jnp.concatenate([tiles, tiles]))
    item_c = jnp.zeros((2 * n_tiles,), jnp.int32).at[pos1].set(1)
    item_e = tile_e[item_t]
    return dest.reshape(-1), tile_e, item_e, item_c, item_t


def kernel(x, c, ctx, c_ctx, norm_mix_g, norm_ffn_g, w_mod, b_mod, w_in, attn_sinks,
           lb_fwd_logits, lb_bwd_logits, hgrn_norm_g, w_branch, w_out, w_router, b_router,
           w_e_gate, b_e_gate, w_e_up, b_e_up, w_e_down, b_e_down, final_norm_g):
    bsz, n, d = x.shape
    l = ctx.shape[1]
    assert bsz == 1 and d == D_MODEL and norm_mix_g.shape[0] == 1
    assert n % HG_BLOCK == 0 and l % HG_BLOCK == 0 and n // A_BLOCK >= 2
    x2 = x.reshape(n, d)
    ctx2 = ctx.reshape(l, d)

    mod = _mod_vectors(c, c_ctx, w_mod[0], b_mod[0])
    h_all = _norm_all(x2, ctx2, norm_mix_g[0], mod)

    w = w_in[0].astype(_bf16)
    c0 = 0
    segs = []
    for width in (A_WIDTH + 2 * A_KV_WIDTH, B_WIDTH, B_WIDTH, B_WIDTH, B_WIDTH, B_WIDTH, 2 * d):
        segs.append(w[:, c0:c0 + width])
        c0 += width
    w_qkv, w_rq, w_zf, w_zb, w_rv, w_rg, w_gates = segs
    cos_t, sin_t = _rope_tables(n, l)
    q, k, v = _proj_qkv(h_all, w_qkv, cos_t, sin_t)
    rq = _proj_heads(h_all, w_rq, "silu")
    rv = _proj_heads(h_all, w_rv, "none")
    rg = _proj_heads(h_all, w_rg, "silu")
    lb_f = jax.nn.softmax(lb_fwd_logits.astype(_f32), axis=0)[0]
    lb_b = jax.nn.softmax(lb_bwd_logits.astype(_f32), axis=0)[0]
    lb2 = jnp.stack([lb_f, lb_b]).reshape(2, 1, B_WIDTH)
    lf, kk = _proj_gate(h_all, jnp.stack([w_zf, w_zb]), lb2)
    gates = _proj_sigmoid(h_all, w_gates)

    ya = _attention(q, k, v, attn_sinks[0], n, l)
    o_f = _hgrn_scan(rq, kk, rv, lf, 0, l)
    o_b = _hgrn_scan(rq, kk, rv, lf, 1, l)

    wr = jnp.zeros((d, LANES), _f32).at[:, :N_EXPERTS].set(w_router[0].astype(_f32))
    wr_hi = wr.astype(_bf16)
    wr_lo = (wr - wr_hi.astype(_f32)).astype(_bf16)
    br = jnp.zeros((1, LANES), _f32).at[0, :N_EXPERTS].set(b_router[0].astype(_f32))
    x1, h2, route, counts = _merge_route(
        ya, o_f, o_b, rg, gates, x2,
        w_branch[0, 0].astype(_bf16), w_branch[0, 1].astype(_bf16), w_out[0].astype(_bf16),
        wr_hi, wr_lo, br, hgrn_norm_g[0].reshape(1, B_WIDTH).astype(_f32),
        norm_ffn_g[0].reshape(1, d), mod, l)

    n_tiles = (n * TOP_K) // MOE_TM + N_EXPERTS
    dest, tile_e, item_e, item_c, item_t = _routing_tables(route, counts, n_tiles)
    xs = _dispatch(h2, dest, n_tiles * MOE_TM)
    act = _moe_up(xs, w_e_gate[0].astype(_bf16), w_e_up[0].astype(_bf16),
                  b_e_gate[0].reshape(N_EXPERTS, 1, EXPERT_FF), b_e_up[0].reshape(N_EXPERTS, 1, EXPERT_FF),
                  item_e, item_c, item_t)
    ys = _moe_down(act, w_e_down[0].astype(_bf16), b_e_down[0].reshape(N_EXPERTS, 1, d), tile_e)
    out = _combine(ys, dest, route, x1, mod, final_norm_g)
    return out.reshape(bsz, n, d)
```

```python
import functools

import jax
import jax.numpy as jnp
from jax import lax
from jax.experimental import pallas as pl
from jax.experimental.pallas import tpu as pltpu

D_MODEL = 2048
GRID_W = 64
EPS = 1e-6
A_HEADS = 16
A_KV_HEADS = 2
A_GROUP = A_HEADS // A_KV_HEADS
A_HEAD_DIM = 64
A_WIDTH = A_HEADS * A_HEAD_DIM
A_KV_WIDTH = A_KV_HEADS * A_HEAD_DIM
WINDOW = 128
A_BLOCK = 128
ROPE_BASE = 10000.0
ROPE_PAIRS_PER_AXIS = A_HEAD_DIM // 4
B_HEADS = 8
B_DK = 128
B_DV = 128
B_WIDTH = B_HEADS * B_DV
N_EXPERTS = 32
TOP_K = 4
EXPERT_FF = 2048
SWIGLU_LIMIT = 7.0
SWIGLU_ALPHA = 1.702

LANES = 128
SUB = 8
HG_CHUNK = 64
HG_BLOCK = 256
MOE_TM = 256
NEG = -1e30
VMEM_LIMIT = 56 * 1024 * 1024

_f32 = jnp.float32
_bf16 = jnp.bfloat16


def _cparams(sem):
    return pltpu.CompilerParams(dimension_semantics=sem, vmem_limit_bytes=VMEM_LIMIT)


def _sigmoid(x):
    return 1.0 / (1.0 + jnp.exp(-x))


def _mod_kernel(s_ref, w_ref, b_ref, o_ref):
    tn = w_ref.shape[1]
    for r in range(2):
        s = s_ref[r]
        s = s * _sigmoid(s)
        for j in range(tn // LANES):
            sl = slice(j * LANES, (j + 1) * LANES)
            acc = jnp.sum(w_ref[:, sl] * s, axis=0, keepdims=True)
            o_ref[r:r + 1, sl] = acc + b_ref[:, sl]


def _mod_vectors(c, c_ctx, w_mod, b_mod):
    d, n_out = w_mod.shape
    tn = 1024
    s = jnp.stack([c.reshape(d), c_ctx.reshape(d)]).astype(_f32)
    s = jnp.broadcast_to(s[:, :, None], (2, d, LANES))
    return pl.pallas_call(
        _mod_kernel,
        out_shape=jax.ShapeDtypeStruct((2, n_out), _f32),
        grid=(n_out // tn,),
        in_specs=[pl.BlockSpec((2, d, LANES), lambda j: (0, 0, 0)),
                  pl.BlockSpec((d, tn), lambda j: (0, j)),
                  pl.BlockSpec((1, tn), lambda j: (0, j))],
        out_specs=pl.BlockSpec((2, tn), lambda j: (0, j)),
        compiler_params=_cparams(("arbitrary",)),
        name="mod_vectors",
    )(s, w_mod, b_mod.reshape(1, n_out))


def _rms_mod(xf, g, shift, scale):
    y = xf * lax.rsqrt(jnp.mean(xf * xf, axis=-1, keepdims=True) + EPS)
    return (y * g) * (1.0 + scale) + shift


def _norm_kernel(x_ref, ctx_ref, g_ref, mod_ref, o_ref, *, n_ctx_tiles):
    i = pl.program_id(0)
    d = x_ref.shape[1]

    @pl.when(i < n_ctx_tiles)
    def _():
        o_ref[...] = _rms_mod(ctx_ref[...], g_ref[...], mod_ref[1:2, 0:d],
                              mod_ref[1:2, d:2 * d]).astype(o_ref.dtype)

    @pl.when(i >= n_ctx_tiles)
    def _():
        o_ref[...] = _rms_mod(x_ref[...], g_ref[...], mod_ref[0:1, 0:d],
                              mod_ref[0:1, d:2 * d]).astype(o_ref.dtype)


def _norm_all(x2, ctx2, g, mod):
    n, d = x2.shape
    l = ctx2.shape[0]
    tm = 256
    nct = l // tm
    return pl.pallas_call(
        functools.partial(_norm_kernel, n_ctx_tiles=nct),
        out_shape=jax.ShapeDtypeStruct((l + n, d), _bf16),
        grid=((l + n) // tm,),
        in_specs=[pl.BlockSpec((tm, d), lambda i: (jnp.maximum(i - nct, 0), 0)),
                  pl.BlockSpec((tm, d), lambda i: (jnp.minimum(i, nct - 1), 0)),
                  pl.BlockSpec((1, d), lambda i: (0, 0)),
                  pl.BlockSpec(mod.shape, lambda i: (0, 0))],
        out_specs=pl.BlockSpec((tm, d), lambda i: (i, 0)),
        compiler_params=_cparams(("arbitrary",)),
        name="adaln_norm",
    )(x2, ctx2, g.reshape(1, d), mod)


def _rope(a, cos, sin):
    lane = lax.broadcasted_iota(jnp.int32, a.shape, 1)
    first = (lane % A_HEAD_DIM) < (A_HEAD_DIM // 2)
    rot = jnp.where(first, pltpu.roll(a, LANES - 32, 1), pltpu.roll(a, 32, 1))
    return a * cos + rot * sin


def _proj_qkv_kernel(h_ref, w_ref, cos_ref, sin_ref, q_ref, k_ref, v_ref):
    acc = jnp.dot(h_ref[...], w_ref[...], preferred_element_type=_f32)
    cos = cos_ref[...]
    sin = sin_ref[...]
    scale = A_HEAD_DIM ** -0.5
    for j in range(A_WIDTH // LANES):
        sl = slice(j * LANES, (j + 1) * LANES)
        q_ref[:, sl] = (_rope(acc[:, sl], cos, sin) * scale).astype(q_ref.dtype)
    k_ref[...] = _rope(acc[:, A_WIDTH:A_WIDTH + LANES], cos, sin).astype(k_ref.dtype)
    v_ref[...] = acc[:, A_WIDTH + LANES:A_WIDTH + 2 * LANES].astype(v_ref.dtype)


def _proj_heads_kernel(h_ref, w_ref, o_ref, *, act):
    acc = jnp.dot(h_ref[...], w_ref[...], preferred_element_type=_f32)
    if act == "silu":
        acc = acc * _sigmoid(acc)
    for hh in range(o_ref.shape[0]):
        o_ref[hh] = acc[:, hh * LANES:(hh + 1) * LANES].astype(o_ref.dtype)


def _proj_gate_kernel(h_ref, w_ref, lb_ref, lf_ref, kk_ref):
    z = jnp.dot(h_ref[...], w_ref[0], preferred_element_type=_f32)
    lb = lb_ref[0]
    sg = _sigmoid(z)
    logf = jnp.log2(lb + (1.0 - lb) * sg)
    kk = (1.0 - lb) * _sigmoid(-z)
    for hh in range(lf_ref.shape[1]):
        sl = slice(hh * LANES, (hh + 1) * LANES)
        lf_ref[0, hh] = logf[:, sl]
        kk_ref[0, hh] = kk[:, sl].astype(kk_ref.dtype)


def _proj_sigmoid_kernel(h_ref, w_ref, o_ref):
    acc = jnp.dot(h_ref[...], w_ref[...], preferred_element_type=_f32)
    o_ref[...] = _sigmoid(acc).astype(o_ref.dtype)


def _row_tile(t):
    for tm in (1280, 640, 256, 128):
        if t % tm == 0:
            return tm
    raise ValueError(t)


def _proj_qkv(h, w, cos_t, sin_t):
    t, d = h.shape
    tm = _row_tile(t)
    ncol = w.shape[1]
    return pl.pallas_call(
        _proj_qkv_kernel,
        out_shape=(jax.ShapeDtypeStruct((t, A_WIDTH), _bf16),
                   jax.ShapeDtypeStruct((t, A_KV_WIDTH), _bf16),
                   jax.ShapeDtypeStruct((t, A_KV_WIDTH), _bf16)),
        grid=(t // tm,),
        in_specs=[pl.BlockSpec((tm, d), lambda i: (i, 0)),
                  pl.BlockSpec((d, ncol), lambda i: (0, 0)),
                  pl.BlockSpec((tm, LANES), lambda i: (i, 0)),
                  pl.BlockSpec((tm, LANES), lambda i: (i, 0))],
        out_specs=(pl.BlockSpec((tm, A_WIDTH), lambda i: (i, 0)),
                   pl.BlockSpec((tm, A_KV_WIDTH), lambda i: (i, 0)),
                   pl.BlockSpec((tm, A_KV_WIDTH), lambda i: (i, 0))),
        compiler_params=_cparams(("arbitrary",)),
        name="proj_qkv",
    )(h, w, cos_t, sin_t)


def _proj_heads(h, w, act):
    t, d = h.shape
    tm = _row_tile(t)
    tn = 512
    nh = tn // LANES
    return pl.pallas_call(
        functools.partial(_proj_heads_kernel, act=act),
        out_shape=jax.ShapeDtypeStruct((w.shape[1] // LANES, t, LANES), _bf16),
        grid=(w.shape[1] // tn, t // tm),
        in_specs=[pl.BlockSpec((tm, d), lambda j, i: (i, 0)),
                  pl.BlockSpec((d, tn), lambda j, i: (0, j))],
        out_specs=pl.BlockSpec((nh, tm, LANES), lambda j, i: (j, i, 0)),
        compiler_params=_cparams(("arbitrary", "arbitrary")),
        name="proj_heads_" + act,
    )(h, w)


def _proj_gate(h, w2, lb2):
    t, d = h.shape
    tm = _row_tile(t)
    tn = 512
    nh = tn // LANES
    ncb = w2.shape[2] // tn
    out_sds = lambda dt: jax.ShapeDtypeStruct((2, w2.shape[2] // LANES, t, LANES), dt)
    return pl.pallas_call(
        _proj_gate_kernel,
        out_shape=(out_sds(_f32), out_sds(_bf16)),
        grid=(2, ncb, t // tm),
        in_specs=[pl.BlockSpec((tm, d), lambda r, j, i: (i, 0)),
                  pl.BlockSpec((1, d, tn), lambda r, j, i: (r, 0, j)),
                  pl.BlockSpec((1, 1, tn), lambda r, j, i: (r, 0, j))],
        out_specs=(pl.BlockSpec((1, nh, tm, LANES), lambda r, j, i: (r, j, i, 0)),
                   pl.BlockSpec((1, nh, tm, LANES), lambda r, j, i: (r, j, i, 0))),
        compiler_params=_cparams(("arbitrary", "arbitrary", "arbitrary")),
        name="proj_gate",
    )(h, w2, lb2)


def _proj_sigmoid(h, w):
    t, d = h.shape
    tm = _row_tile(t)
    tn = 512
    return pl.pallas_call(
        _proj_sigmoid_kernel,
        out_shape=jax.ShapeDtypeStruct((t, w.shape[1]), _bf16),
        grid=(w.shape[1] // tn, t // tm),
        in_specs=[pl.BlockSpec((tm, d), lambda j, i: (i, 0)),
                  pl.BlockSpec((d, tn), lambda j, i: (0, j))],
        out_specs=pl.BlockSpec((tm, tn), lambda j, i: (i, j)),
        compiler_params=_cparams(("arbitrary", "arbitrary")),
        name="proj_sigmoid",
    )(h, w)


def _attn_kernel(q_ref, kp_ref, kc_ref, kn_ref, vp_ref, vc_ref, vn_ref,
                 kx_ref, vx_ref, bias_ref, sink_ref, o_ref):
    hd = A_HEAD_DIM
    nt = (((1,), (1,)), ((), ()))
    for g in range(A_KV_HEADS):
        gs = slice(g * hd, (g + 1) * hd)
        qg = jnp.concatenate(
            [q_ref[:, (g * A_GROUP + h) * hd:(g * A_GROUP + h + 1) * hd] for h in range(A_GROUP)],
            axis=0)
        kb = jnp.concatenate([kp_ref[:, gs], kc_ref[:, gs], kn_ref[:, gs]], axis=0)
        vb = jnp.concatenate([vp_ref[:, gs], vc_ref[:, gs], vn_ref[:, gs]], axis=0)
        s_ctx = lax.dot_general(qg, kx_ref[:, gs], nt, preferred_element_type=_f32)
        s_lat = lax.dot_general(qg, kb, nt, preferred_element_type=_f32) + bias_ref[0]
        sink = sink_ref[g]
        m = jnp.maximum(jnp.maximum(jnp.max(s_ctx, axis=-1, keepdims=True),
                                    jnp.max(s_lat, axis=-1, keepdims=True)), sink)
        p_ctx = jnp.exp(s_ctx - m)
        p_lat = jnp.exp(s_lat - m)
        denom = (jnp.sum(p_ctx, axis=-1, keepdims=True) + jnp.sum(p_lat, axis=-1, keepdims=True)
                 + jnp.exp(sink - m))
        o = (jnp.dot(p_ctx.astype(_bf16), vx_ref[:, gs], preferred_element_type=_f32)
             + jnp.dot(p_lat.astype(_bf16), vb, preferred_element_type=_f32))
        o = o / denom
        for h in range(A_GROUP):
            c0 = (g * A_GROUP + h) * hd
            o_ref[:, c0:c0 + hd] = o[h * A_BLOCK:(h + 1) * A_BLOCK, :].astype(o_ref.dtype)


def _attention(q, k, v, sinks, n, l):
    blk = A_BLOCK
    nblk = n // blk
    off = l // blk
    rows = A_GROUP * blk
    r = jnp.arange(rows) % blk
    cidx = jnp.arange(3 * blk)
    rel = cidx[None, :] - blk - r[:, None]
    win = jnp.abs(rel) <= WINDOW
    part = cidx // blk
    variants = [win & (part != 0)[None, :], win, win & (part != 2)[None, :]]
    bias = jnp.stack([jnp.where(m, 0.0, NEG).astype(_f32) for m in variants])
    sink_col = jnp.repeat(sinks.astype(_f32).reshape(A_KV_HEADS, A_GROUP), blk, axis=1)
    sink_col = sink_col.reshape(A_KV_HEADS, rows, 1)

    def bias_idx(i):
        return (jnp.where(i == 0, 0, jnp.where(i == nblk - 1, 2, 1)), 0, 0)

    kv_spec = lambda f: pl.BlockSpec((blk, A_KV_WIDTH), f)
    prev = lambda i: (jnp.maximum(i - 1, 0) + off, 0)
    cur = lambda i: (i + off, 0)
    nxt = lambda i: (jnp.minimum(i + 1, nblk - 1) + off, 0)
    return pl.pallas_call(
        _attn_kernel,
        out_shape=jax.ShapeDtypeStruct((n, A_WIDTH), _bf16),
        grid=(nblk,),
        in_specs=[pl.BlockSpec((blk, A_WIDTH), cur),
                  kv_spec(prev), kv_spec(cur), kv_spec(nxt),
                  kv_spec(prev), kv_spec(cur), kv_spec(nxt),
                  pl.BlockSpec((l, A_KV_WIDTH), lambda i: (0, 0)),
                  pl.BlockSpec((l, A_KV_WIDTH), lambda i: (0, 0)),
                  pl.BlockSpec((1, rows, 3 * blk), bias_idx),
                  pl.BlockSpec((A_KV_HEADS, rows, 1), lambda i: (0, 0, 0))],
        out_specs=pl.BlockSpec((blk, A_WIDTH), lambda i: (i, 0)),
        compiler_params=_cparams(("arbitrary",)),
        name="window_attention",
    )(q, k, k, k, v, v, v, k, v, bias, sink_col)


def _split3(x):
    hi = x.astype(_bf16)
    r1 = x - hi.astype(_f32)
    mid = r1.astype(_bf16)
    lo = (r1 - mid.astype(_f32)).astype(_bf16)
    return hi, mid, lo


def _hgrn_intra_chunk(q, k, v, b, ones, reverse):
    c = HG_CHUNK
    nsb = c // SUB
    nt = (((1,), (1,)), ((), ()))
    qb = (q * jnp.exp2(b)).astype(_bf16)
    b_tot = b[0:1, :] if reverse else b[c - 1:c, :]
    kend = (k * jnp.exp2(b_tot - b)).astype(_bf16)

    lane = lax.broadcasted_iota(jnp.int32, (SUB, LANES), 1)
    row = lax.broadcasted_iota(jnp.int32, (SUB, LANES), 0)
    zero_row = jnp.zeros((1, LANES), _f32)
    diag_tmp = []
    off_rows = []
    for ib in range(nsb):
        r0 = ib * SUB
        bi = b[r0:r0 + SUB]
        qi = q[r0:r0 + SUB]
        for s in range(SUB):
            bs = b[r0 + s:r0 + s + 1]
            ks = k[r0 + s:r0 + s + 1]
            dec = jnp.exp2(jnp.minimum(bi - bs, 0.0))
            diag_tmp.append((qi * ks * dec).astype(_bf16))
        if reverse:
            lo_r, hi_r = r0 + SUB, c
            ref_b = b[r0 + SUB:r0 + SUB + 1] if ib < nsb - 1 else zero_row
        else:
            lo_r, hi_r = 0, r0
            ref_b = b[r0 - 1:r0] if ib > 0 else zero_row
        if hi_r > lo_r:
            qp = (qi * jnp.exp2(bi - ref_b)).astype(_bf16)
            kp = (k[lo_r:hi_r] * jnp.exp2(ref_b - b[lo_r:hi_r])).astype(_bf16)
            pieces = []
            if lo_r > 0:
                pieces.append(jnp.zeros((lo_r, LANES), _bf16))
            pieces.append(kp)
            if hi_r < c:
                pieces.append(jnp.zeros((c - hi_r, LANES), _bf16))
            kp_full = jnp.concatenate(pieces, axis=0) if len(pieces) > 1 else pieces[0]
            off_rows.append(lax.dot_general(qp, kp_full, nt, preferred_element_type=_f32))
        else:
            off_rows.append(jnp.zeros((SUB, c), _f32))
    rsum = jnp.dot(jnp.concatenate(diag_tmp, axis=0), ones, preferred_element_type=_f32)
    diag_rows = []
    for ib in range(nsb):
        acc = jnp.zeros((SUB, LANES), _f32)
        for s in range(SUB):
            idx = ib * SUB + s
            acc = jnp.where(lane == idx, rsum[idx * SUB:(idx + 1) * SUB], acc)
        tri_keep = (lane - ib * SUB >= row) if reverse else (lane - ib * SUB <= row)
        diag_rows.append(jnp.where(tri_keep, acc, 0.0)[:, 0:c])
    a = jnp.concatenate(off_rows, axis=0) + jnp.concatenate(diag_rows, axis=0)
    o = jnp.dot(a.astype(_bf16), v, preferred_element_type=_f32)
    return o, qb, kend, jnp.exp2(b_tot)


def _hgrn_intra_kernel(q_ref, k_ref, v_ref, lf_ref, o_ref, qb_ref, ke_ref, dt_ref, *, reverse):
    c = HG_CHUNK
    nch = HG_BLOCK // c
    nh = q_ref.shape[0]
    ri = lax.broadcasted_iota(jnp.int32, (c, c), 0)
    ci = lax.broadcasted_iota(jnp.int32, (c, c), 1)
    tri = jnp.where((ci >= ri) if reverse else (ci <= ri), 1.0, 0.0).astype(_bf16)
    ones = jnp.ones((LANES, LANES), _bf16)

    def body(cc, carry):
        rows = pl.ds(pl.multiple_of(cc * c, c), c)
        lf_all = jnp.concatenate([lf_ref[0, h, rows, :] for h in range(nh)], axis=1)
        hi, mid, lo = _split3(lf_all)
        b_all = (jnp.dot(tri, hi, preferred_element_type=_f32)
                 + jnp.dot(tri, mid, preferred_element_type=_f32)
                 + jnp.dot(tri, lo, preferred_element_type=_f32))
        for h in range(nh):
            o, qb, kend, dtot = _hgrn_intra_chunk(
                q_ref[h, rows, :].astype(_f32), k_ref[0, h, rows, :].astype(_f32),
                v_ref[h, rows, :], b_all[:, h * LANES:(h + 1) * LANES], ones, reverse)
            o_ref[h, rows, :] = o
            qb_ref[h, rows, :] = qb
            ke_ref[h, rows, :] = kend
            dt_ref[cc, h:h + 1, :] = dtot
        return carry

    lax.fori_loop(0, nch, body, 0)


def _hgrn_intra(rq, kk, rv, lf, direction):
    nh, t, _ = rq.shape
    br = HG_BLOCK
    nch = br // HG_CHUNK
    reverse = direction == 1
    spec3 = pl.BlockSpec((nh, br, LANES), lambda s: (0, s, 0))
    spec4 = pl.BlockSpec((1, nh, br, LANES), lambda s: (direction, 0, s, 0))
    sds = lambda dt: jax.ShapeDtypeStruct((nh, t, LANES), dt)
    return pl.pallas_call(
        functools.partial(_hgrn_intra_kernel, reverse=reverse),
        out_shape=(sds(_f32), sds(_bf16), sds(_bf16),
                   jax.ShapeDtypeStruct((t // HG_CHUNK, nh, LANES), _f32)),
        grid=(t // br,),
        in_specs=[spec3, spec4, spec3, spec4],
        out_specs=(spec3, spec3, spec3, pl.BlockSpec((nch, nh, LANES), lambda s: (s, 0, 0))),
        compiler_params=_cparams(("arbitrary",)),
        name="hgrn2_intra_" + ("bwd" if reverse else "fwd"),
    )(rq, kk, rv, lf)


def _hgrn_state_kernel(of_ref, qf_ref, kf_ref, df_ref, vf_ref, ob_ref, qbk_ref, kb_ref, db_ref, vb_ref,
                       outf_ref, outb_ref, st_ref):
    c = HG_CHUNK
    nch = HG_BLOCK // c
    nh = qf_ref.shape[0]
    nt = (((1,), (1,)), ((), ()))
    tn = (((0,), (0,)), ((), ()))

    @pl.when(pl.program_id(0) == 0)
    def _():
        st_ref[...] = jnp.zeros(st_ref.shape, st_ref.dtype)

    dirs = ((0, False, of_ref, qf_ref, kf_ref, df_ref, vf_ref, outf_ref),
            (1, True, ob_ref, qbk_ref, kb_ref, db_ref, vb_ref, outb_ref))

    def body(cc, carry):
        for di, reverse, oi_ref, q_ref, k_ref, d_ref, v_ref, out_ref in dirs:
            chunk = (nch - 1 - cc) if reverse else cc
            rows = pl.ds(pl.multiple_of(chunk * c, c), c)
            for h in range(nh):
                st = st_ref[di, h]
                o = oi_ref[h, rows, :] + lax.dot_general(
                    q_ref[h, rows, :], st.astype(_bf16), nt, preferred_element_type=_f32)
                out_ref[h, rows, :] = o.astype(out_ref.dtype)
                upd = lax.dot_general(v_ref[h, rows, :], k_ref[h, rows, :], tn,
                                      preferred_element_type=_f32)
                st_ref[di, h] = st * d_ref[chunk, pl.ds(h, 1), :] + upd
        return carry

    lax.fori_loop(0, nch, body, 0)


def _hgrn_state(intra_f, intra_b, rv, l):
    o_f, qb_f, ke_f, dt_f = intra_f
    o_b, qb_b, ke_b, dt_b = intra_b
    nh, t, _ = rv.shape
    br = HG_BLOCK
    nch = br // HG_CHUNK
    nb = t // br
    nc = l // br

    def blk_b(s):
        return jnp.where(s < nc, nc - 1 - s, nb - 1 - (s - nc))

    def specs(blk):
        s3 = pl.BlockSpec((nh, br, LANES), lambda s: (0, blk(s), 0))
        sd = pl.BlockSpec((nch, nh, LANES), lambda s: (blk(s), 0, 0))
        return [s3, s3, s3, sd, s3]

    fwd = specs(lambda s: s)
    bwd = specs(blk_b)
    return pl.pallas_call(
        _hgrn_state_kernel,
        out_shape=(jax.ShapeDtypeStruct((nh, t, LANES), _bf16),
                   jax.ShapeDtypeStruct((nh, t, LANES), _bf16)),
        grid=(nb,),
        in_specs=fwd + bwd,
        out_specs=(fwd[0], bwd[0]),
        scratch_shapes=[pltpu.VMEM((2, nh, B_DV, B_DK), _f32)],
        compiler_params=_cparams(("arbitrary",)),
        name="hgrn2_state",
    )(o_f, qb_f, ke_f, dt_f, rv, o_b, qb_b, ke_b, dt_b, rv)


def _merge_kernel(ya_ref, of_ref, ob_ref, rg_ref, ga_ref, gb_ref, x_ref, wa_ref, wb_ref, wo_ref,
                  wrh_ref, wrl_ref, br_ref, hg_ref, g2_ref, mod_ref,
                  x1_ref, h2_ref, route_ref, cnt_ref, run_ref):
    i = pl.program_id(0)
    tm, d = x_ref.shape

    @pl.when(i == 0)
    def _():
        run_ref[...] = jnp.zeros(run_ref.shape, run_ref.dtype)

    parts = []
    for h in range(B_HEADS):
        o = of_ref[h].astype(_f32) + ob_ref[h].astype(_f32)
        o = o * lax.rsqrt(jnp.mean(o * o, axis=-1, keepdims=True) + EPS)
        o = o * hg_ref[:, h * LANES:(h + 1) * LANES]
        parts.append((o * rg_ref[h].astype(_f32)).astype(_bf16))
    yb = jnp.concatenate(parts, axis=1)
    z0 = jnp.dot(ya_ref[...], wa_ref[...], preferred_element_type=_f32)
    z1 = jnp.dot(yb, wb_ref[...], preferred_element_type=_f32)
    merged = ga_ref[...].astype(_f32) * z0 + gb_ref[...].astype(_f32) * z1
    y = jnp.dot(merged.astype(_bf16), wo_ref[...], preferred_element_type=_f32)
    x1 = x_ref[...] + mod_ref[0:1, 2 * d:3 * d] * y
    x1_ref[...] = x1
    h2 = _rms_mod(x1, g2_ref[...], mod_ref[0:1, 3 * d:4 * d], mod_ref[0:1, 4 * d:5 * d])
    h2_ref[...] = h2

    hh = h2.astype(_bf16)
    hl = (h2 - hh.astype(_f32)).astype(_bf16)
    logits = (jnp.dot(hh, wrh_ref[...], preferred_element_type=_f32)
              + jnp.dot(hl, wrh_ref[...], preferred_element_type=_f32)
              + jnp.dot(hh, wrl_ref[...], preferred_element_type=_f32)) + br_ref[...]
    lane = lax.broadcasted_iota(jnp.int32, (tm, LANES), 1)
    work = jnp.where(lane < N_EXPERTS, logits, NEG)
    vals, idxs = [], []
    onehot = jnp.zeros((tm, LANES), _f32)
    for _ in range(TOP_K):
        m = jnp.max(work, axis=-1, keepdims=True)
        idx = jnp.min(jnp.where(work == m, lane, LANES), axis=-1, keepdims=True)
        sel = lane == idx
        vals.append(m)
        idxs.append(idx)
        onehot = jnp.where(sel, 1.0, onehot)
        work = jnp.where(sel, NEG, work)
    es = [jnp.exp(vv - vals[0]) for vv in vals]
    tot = es[0] + es[1] + es[2] + es[3]
    ri = lax.broadcasted_iota(jnp.int32, (tm, tm), 0)
    ci = lax.broadcasted_iota(jnp.int32, (tm, tm), 1)
    ltri = jnp.where(ci < ri, 1.0, 0.0).astype(_bf16)
    prefix = jnp.dot(ltri, onehot.astype(_bf16), preferred_element_type=_f32) + run_ref[...]
    route = jnp.zeros((tm, LANES), _f32)
    for j in range(TOP_K):
        rank = jnp.sum(jnp.where(lane == idxs[j], prefix, 0.0), axis=-1, keepdims=True)
        route = jnp.where(lane == j, idxs[j].astype(_f32), route)
        route = jnp.where(lane == TOP_K + j, es[j] / tot, route)
        route = jnp.where(lane == 2 * TOP_K + j, rank, route)
    route_ref[...] = route
    run_new = run_ref[...] + jnp.sum(onehot, axis=0, keepdims=True)
    run_ref[...] = run_new
    cnt_ref[...] = run_new


def _merge_route(ya, o_f, o_b, rg, gates, x2, wa, wb, wo, wr_hi, wr_lo, b_router, hg, g2, mod, l):
    n, d = x2.shape
    tm = 256
    off = l // tm
    const = lambda shape: pl.BlockSpec(shape, lambda i: (0,) * len(shape),
                                       pipeline_mode=pl.Buffered(1))
    head_spec = pl.BlockSpec((B_HEADS, tm, LANES), lambda i: (0, i + off, 0))
    return pl.pallas_call(
        _merge_kernel,
        out_shape=(jax.ShapeDtypeStruct((n, d), _f32),
                   jax.ShapeDtypeStruct((n, d), _f32),
                   jax.ShapeDtypeStruct((n, LANES), _f32),
                   jax.ShapeDtypeStruct((1, LANES), _f32)),
        grid=(n // tm,),
        in_specs=[pl.BlockSpec((tm, A_WIDTH), lambda i: (i, 0)),
                  head_spec, head_spec, head_spec,
                  pl.BlockSpec((tm, d), lambda i: (i + off, 0)),
                  pl.BlockSpec((tm, d), lambda i: (i + off, 1)),
                  pl.BlockSpec((tm, d), lambda i: (i, 0)),
                  const(wa.shape), const(wb.shape), const(wo.shape),
                  const(wr_hi.shape), const(wr_lo.shape), const((1, LANES)),
                  const((1, B_WIDTH)), const((1, d)), const(mod.shape)],
        out_specs=(pl.BlockSpec((tm, d), lambda i: (i, 0)),
                   pl.BlockSpec((tm, d), lambda i: (i, 0)),
                   pl.BlockSpec((tm, LANES), lambda i: (i, 0)),
                   pl.BlockSpec((1, LANES), lambda i: (0, 0))),
        scratch_shapes=[pltpu.VMEM((1, LANES), _f32)],
        compiler_params=_cparams(("arbitrary",)),
        name="merge_route",
    )(ya, o_f, o_b, rg, gates, gates, x2, wa, wb, wo, wr_hi, wr_lo, b_router, hg, g2, mod)


def _dispatch_kernel(dest_ref, h2_ref, xs_in_ref, xs_ref, sem):
    del xs_in_ref
    tm = h2_ref.shape[0]

    def issue(t, carry):
        for j in range(TOP_K):
            dst = dest_ref[t * TOP_K + j]
            pltpu.make_async_copy(h2_ref.at[pl.ds(t, 1)], xs_ref.at[pl.ds(dst, 1)], sem).start()
        return carry

    lax.fori_loop(0, tm, issue, 0)
    for _ in range(TOP_K):
        pltpu.make_async_copy(h2_ref, xs_ref.at[pl.ds(0, tm)], sem).wait()


def _dispatch(h2, dest_flat, n_slots):
    n, d = h2.shape
    tm = 256
    xs0 = jnp.zeros((n_slots, d), h2.dtype)
    return pl.pallas_call(
        _dispatch_kernel,
        out_shape=jax.ShapeDtypeStruct((n_slots, d), h2.dtype),
        grid=(n // tm,),
        in_specs=[pl.BlockSpec((tm * TOP_K,), lambda i: (i,), memory_space=pltpu.SMEM),
                  pl.BlockSpec((tm, d), lambda i: (i, 0)),
                  pl.BlockSpec(memory_space=pl.ANY)],
        out_specs=pl.BlockSpec(memory_space=pl.ANY),
        scratch_shapes=[pltpu.SemaphoreType.DMA],
        input_output_aliases={2: 0},
        compiler_params=_cparams(("arbitrary",)),
        name="moe_dispatch",
    )(dest_flat, h2, xs0)


def _moe_up_kernel(ie_ref, ic_ref, it_ref, x_ref, wg_ref, wu_ref, bg_ref, bu_ref, a_ref):
    del ie_ref, ic_ref, it_ref
    x = x_ref[...].astype(_bf16)
    g = jnp.dot(x, wg_ref[0], preferred_element_type=_f32) + bg_ref[0]
    u = jnp.dot(x, wu_ref[0], preferred_element_type=_f32) + bu_ref[0]
    g = jnp.minimum(g, SWIGLU_LIMIT)
    u = jnp.clip(u, -SWIGLU_LIMIT, SWIGLU_LIMIT)
    a_ref[...] = (g * _sigmoid(SWIGLU_ALPHA * g) * (u + 1.0)).astype(a_ref.dtype)


def _moe_up(xs, wg, wu, bg, bu, item_e, item_c, item_t):
    n_slots, d = xs.shape
    ff = wg.shape[2]
    fc = ff // 2
    tm = MOE_TM
    n_items = item_e.shape[0]
    w_spec = pl.BlockSpec((1, d, fc), lambda i, ie, ic, it: (ie[i], 0, ic[i]))
    b_spec = pl.BlockSpec((1, 1, fc), lambda i, ie, ic, it: (ie[i], 0, ic[i]))
    return pl.pallas_call(
        _moe_up_kernel,
        out_shape=jax.ShapeDtypeStruct((n_slots, ff), _bf16),
        grid_spec=pltpu.PrefetchScalarGridSpec(
            num_scalar_prefetch=3,
            grid=(n_items,),
            in_specs=[pl.BlockSpec((tm, d), lambda i, ie, ic, it: (it[i], 0)),
                      w_spec, w_spec, b_spec, b_spec],
            out_specs=pl.BlockSpec((tm, fc), lambda i, ie, ic, it: (it[i], ic[i]))),
        compiler_params=_cparams(("arbitrary",)),
        name="moe_gate_up",
    )(item_e, item_c, item_t, xs, wg, wu, bg, bu)


def _moe_down_kernel(te_ref, a_ref, wd_ref, bd_ref, y_ref):
    del te_ref
    y_ref[...] = jnp.dot(a_ref[...], wd_ref[0], preferred_element_type=_f32) + bd_ref[0]


def _moe_down(act, wd, bd, tile_e):
    n_slots, ff = act.shape
    d = wd.shape[2]
    tm = MOE_TM
    return pl.pallas_call(
        _moe_down_kernel,
        out_shape=jax.ShapeDtypeStruct((n_slots, d), _f32),
        grid_spec=pltpu.PrefetchScalarGridSpec(
            num_scalar_prefetch=1,
            grid=(n_slots // tm,),
            in_specs=[pl.BlockSpec((tm, ff), lambda t, te: (t, 0)),
                      pl.BlockSpec((1, ff, d), lambda t, te: (te[t], 0, 0)),
                      pl.BlockSpec((1, 1, d), lambda t, te: (te[t], 0, 0))],
            out_specs=pl.BlockSpec((tm, d), lambda t, te: (t, 0))),
        compiler_params=_cparams(("arbitrary",)),
        name="moe_down",
    )(tile_e, act, wd, bd)


def _combine_kernel(dest_ref, ys_ref, route_ref, x1_ref, mod_ref, gf_ref, o_ref, buf_ref, sem):
    tm, d = x1_ref.shape

    def issue(t, carry):
        for j in range(TOP_K):
            src = dest_ref[t * TOP_K + j]
            pltpu.make_async_copy(ys_ref.at[pl.ds(src, 1)], buf_ref.at[j, pl.ds(t, 1)], sem).start()
        return carry

    lax.fori_loop(0, tm, issue, 0)
    for j in range(TOP_K):
        pltpu.make_async_copy(ys_ref.at[pl.ds(0, tm)], buf_ref.at[j], sem).wait()
    acc = jnp.zeros((tm, d), _f32)
    for j in range(TOP_K):
        acc = acc + route_ref[:, TOP_K + j:TOP_K + j + 1] * buf_ref[j]
    x2 = x1_ref[...] + mod_ref[0:1, 5 * d:6 * d] * acc
    y = x2 * lax.rsqrt(jnp.mean(x2 * x2, axis=-1, keepdims=True) + EPS)
    o_ref[...] = y * gf_ref[...]


def _combine(ys, dest_flat, route, x1, mod, gf):
    n, d = x1.shape
    tm = 256
    return pl.pallas_call(
        _combine_kernel,
        out_shape=jax.ShapeDtypeStruct((n, d), _f32),
        grid=(n // tm,),
        in_specs=[pl.BlockSpec((tm * TOP_K,), lambda i: (i,), memory_space=pltpu.SMEM),
                  pl.BlockSpec(memory_space=pl.ANY),
                  pl.BlockSpec((tm, LANES), lambda i: (i, 0)),
                  pl.BlockSpec((tm, d), lambda i: (i, 0)),
                  pl.BlockSpec(mod.shape, lambda i: (0, 0)),
                  pl.BlockSpec((1, d), lambda i: (0, 0))],
        out_specs=pl.BlockSpec((tm, d), lambda i: (i, 0)),
        scratch_shapes=[pltpu.VMEM((TOP_K, tm, d), _f32), pltpu.SemaphoreType.DMA],
        compiler_params=_cparams(("arbitrary",)),
        name="moe_combine",
    )(dest_flat, ys, route, x1, mod, gf.reshape(1, d))


def _rope_tables(n, l):
    rows = n // GRID_W
    row = jnp.repeat(jnp.arange(rows, dtype=_f32), GRID_W)
    col = jnp.tile(jnp.arange(GRID_W, dtype=_f32), rows)
    inv_freq = ROPE_BASE ** (-jnp.arange(ROPE_PAIRS_PER_AXIS, dtype=_f32) / ROPE_PAIRS_PER_AXIS)
    ang = jnp.concatenate([row[:, None] * inv_freq, col[:, None] * inv_freq], axis=-1)
    cos, sin = jnp.cos(ang), jnp.sin(ang)
    cos_t = jnp.tile(cos, (1, LANES // 32))
    sin_t = jnp.tile(jnp.concatenate([-sin, sin], axis=-1), (1, LANES // A_HEAD_DIM))
    cos_t = jnp.concatenate([jnp.ones((l, LANES), _f32), cos_t], axis=0)
    sin_t = jnp.concatenate([jnp.zeros((l, LANES), _f32), sin_t], axis=0)
    return cos_t, sin_t


def _routing_tables(route, counts, n_tiles):
    tm = MOE_TM
    idx = route[:, 0:TOP_K].astype(jnp.int32)
    rank = route[:, 2 * TOP_K:3 * TOP_K].astype(jnp.int32)
    cnt = counts[0, :N_EXPERTS].astype(jnp.int32)
    ntile = (cnt + tm - 1) // tm
    tend = jnp.cumsum(ntile)
    tstart = tend - ntile
    dest = (tstart * tm)[idx] + rank
    tiles = jnp.arange(n_tiles, dtype=jnp.int32)
    e_raw = jnp.searchsorted(tend, tiles, side="right").astype(jnp.int32)
    tile_e = jnp.minimum(e_raw, N_EXPERTS - 1)
    tstart_x = jnp.concatenate([tstart, tend[-1:]])
    ntile_x = jnp.concatenate([ntile, (n_tiles - tend[-1])[None]])
    base = tstart_x[e_raw]
    pos0 = 2 * base + (tiles - base)
    pos1 = pos0 + ntile_x[e_raw]
    pos = jnp.concatenate([pos0, pos1])
    item_t = jnp.zeros((2 * n_tiles,), jnp.int32).at[pos].set(jnp.concatenate([tiles, tiles]))
    item_c = jnp.zeros((2 * n_tiles,), jnp.int32).at[pos1].set(1)
    item_e = tile_e[item_t]
    return dest.reshape(-1), tile_e, item_e, item_c, item_t


def kernel(x, c, ctx, c_ctx, norm_mix_g, norm_ffn_g, w_mod, b_mod, w_in, attn_sinks,
           lb_fwd_logits, lb_bwd_logits, hgrn_norm_g, w_branch, w_out, w_router, b_router,
           w_e_gate, b_e_gate, w_e_up, b_e_up, w_e_down, b_e_down, final_norm_g):
    bsz, n, d = x.shape
    l = ctx.shape[1]
    assert bsz == 1 and d == D_MODEL and norm_mix_g.shape[0] == 1
    assert n % HG_BLOCK == 0 and l % HG_BLOCK == 0 and n // A_BLOCK >= 2
    x2 = x.reshape(n, d)
    ctx2 = ctx.reshape(l, d)

    mod = _mod_vectors(c, c_ctx, w_mod[0], b_mod[0])
    h_all = _norm_all(x2, ctx2, norm_mix_g[0], mod)

    w = w_in[0].astype(_bf16)
    c0 = 0
    segs = []
    for width in (A_WIDTH + 2 * A_KV_WIDTH, B_WIDTH, B_WIDTH, B_WIDTH, B_WIDTH, B_WIDTH, 2 * d):
        segs.append(w[:, c0:c0 + width])
        c0 += width
    w_qkv, w_rq, w_zf, w_zb, w_rv, w_rg, w_gates = segs
    cos_t, sin_t = _rope_tables(n, l)
    q, k, v = _proj_qkv(h_all, w_qkv, cos_t, sin_t)
    rq = _proj_heads(h_all, w_rq, "silu")
    rv = _proj_heads(h_all, w_rv, "none")
    rg = _proj_heads(h_all, w_rg, "silu")
    lb_f = jax.nn.softmax(lb_fwd_logits.astype(_f32), axis=0)[0]
    lb_b = jax.nn.softmax(lb_bwd_logits.astype(_f32), axis=0)[0]
    lb2 = jnp.stack([lb_f, lb_b]).reshape(2, 1, B_WIDTH)
    lf, kk = _proj_gate(h_all, jnp.stack([w_zf, w_zb]), lb2)
    gates = _proj_sigmoid(h_all, w_gates)

    ya = _attention(q, k, v, attn_sinks[0], n, l)
    o_f, o_b = _hgrn_state(_hgrn_intra(rq, kk, rv, lf, 0), _hgrn_intra(rq, kk, rv, lf, 1), rv, l)

    wr = jnp.zeros((d, LANES), _f32).at[:, :N_EXPERTS].set(w_router[0].astype(_f32))
    wr_hi = wr.astype(_bf16)
    wr_lo = (wr - wr_hi.astype(_f32)).astype(_bf16)
    br = jnp.zeros((1, LANES), _f32).at[0, :N_EXPERTS].set(b_router[0].astype(_f32))
    x1, h2, route, counts = _merge_route(
        ya, o_f, o_b, rg, gates, x2,
        w_branch[0, 0].astype(_bf16), w_branch[0, 1].astype(_bf16), w_out[0].astype(_bf16),
        wr_hi, wr_lo, br, hgrn_norm_g[0].reshape(1, B_WIDTH).astype(_f32),
        norm_ffn_g[0].reshape(1, d), mod, l)

    n_tiles = (n * TOP_K) // MOE_TM + N_EXPERTS
    dest, tile_e, item_e, item_c, item_t = _routing_tables(route, counts, n_tiles)
    xs = _dispatch(h2, dest, n_tiles * MOE_TM)
    act = _moe_up(xs, w_e_gate[0].astype(_bf16), w_e_up[0].astype(_bf16),
                  b_e_gate[0].reshape(N_EXPERTS, 1, EXPERT_FF), b_e_up[0].reshape(N_EXPERTS, 1, EXPERT_FF),
                  item_e, item_c, item_t)
    ys = _moe_down(act, w_e_down[0].astype(_bf16), b_e_down[0].reshape(N_EXPERTS, 1, d), tile_e)
    out = _combine(ys, dest, route, x1, mod, final_norm_g)
    return out.reshape(bsz, n, d)
```

```python
import functools

import jax
import jax.numpy as jnp
import numpy as np
from jax import lax
from jax.experimental import pallas as pl
from jax.experimental.pallas import tpu as pltpu

D_MODEL = 2048
GRID_W = 64
EPS = 1e-6
A_HEADS = 16
A_KV_HEADS = 2
A_GROUP = A_HEADS // A_KV_HEADS
A_HEAD_DIM = 64
A_WIDTH = A_HEADS * A_HEAD_DIM
A_KV_WIDTH = A_KV_HEADS * A_HEAD_DIM
WINDOW = 128
A_BLOCK = 128
ROPE_BASE = 10000.0
ROPE_PAIRS_PER_AXIS = A_HEAD_DIM // 4
B_HEADS = 8
B_DK = 128
B_DV = 128
B_WIDTH = B_HEADS * B_DV
N_EXPERTS = 32
TOP_K = 4
EXPERT_FF = 2048
SWIGLU_LIMIT = 7.0
SWIGLU_ALPHA = 1.702

LANES = 128
SUB = 8
HG_CHUNK = 64
HG_BLOCK = 256
MOE_TM = 256
NEG = -1e30
LOG2E = 1.4426950408889634
VMEM_LIMIT = 56 * 1024 * 1024

_f32 = jnp.float32
_bf16 = jnp.bfloat16


def _cparams(sem):
    return pltpu.CompilerParams(dimension_semantics=sem, vmem_limit_bytes=VMEM_LIMIT)


def _sigmoid(x):
    return 1.0 / (1.0 + jnp.exp(-x))


def _mod_kernel(s_ref, w_ref, b_ref, o_ref):
    tn = w_ref.shape[1]
    for r in range(2):
        s = s_ref[r]
        s = s * _sigmoid(s)
        for j in range(tn // LANES):
            sl = slice(j * LANES, (j + 1) * LANES)
            acc = jnp.sum(w_ref[:, sl] * s, axis=0, keepdims=True)
            o_ref[r:r + 1, sl] = acc + b_ref[:, sl]


def _mod_vectors(c, c_ctx, w_mod, b_mod):
    d, n_out = w_mod.shape
    tn = 1024
    s = jnp.stack([c.reshape(d), c_ctx.reshape(d)]).astype(_f32)
    s = jnp.broadcast_to(s[:, :, None], (2, d, LANES))
    return pl.pallas_call(
        _mod_kernel,
        out_shape=jax.ShapeDtypeStruct((2, n_out), _f32),
        grid=(n_out // tn,),
        in_specs=[pl.BlockSpec((2, d, LANES), lambda j: (0, 0, 0)),
                  pl.BlockSpec((d, tn), lambda j: (0, j)),
                  pl.BlockSpec((1, tn), lambda j: (0, j))],
        out_specs=pl.BlockSpec((2, tn), lambda j: (0, j)),
        compiler_params=_cparams(("arbitrary",)),
        name="mod_vectors",
    )(s, w_mod, b_mod.reshape(1, n_out))


def _rms_mod(xf, g, shift, scale):
    y = xf * lax.rsqrt(jnp.mean(xf * xf, axis=-1, keepdims=True) + EPS)
    return (y * g) * (1.0 + scale) + shift


def _norm_kernel(x_ref, ctx_ref, g_ref, mod_ref, o_ref, *, n_ctx_tiles):
    i = pl.program_id(0)
    d = x_ref.shape[1]

    @pl.when(i < n_ctx_tiles)
    def _():
        o_ref[...] = _rms_mod(ctx_ref[...], g_ref[...], mod_ref[1:2, 0:d],
                              mod_ref[1:2, d:2 * d]).astype(o_ref.dtype)

    @pl.when(i >= n_ctx_tiles)
    def _():
        o_ref[...] = _rms_mod(x_ref[...], g_ref[...], mod_ref[0:1, 0:d],
                              mod_ref[0:1, d:2 * d]).astype(o_ref.dtype)


def _norm_all(x2, ctx2, g, mod):
    n, d = x2.shape
    l = ctx2.shape[0]
    tm = 256
    nct = l // tm
    return pl.pallas_call(
        functools.partial(_norm_kernel, n_ctx_tiles=nct),
        out_shape=jax.ShapeDtypeStruct((l + n, d), _bf16),
        grid=((l + n) // tm,),
        in_specs=[pl.BlockSpec((tm, d), lambda i: (jnp.maximum(i - nct, 0), 0)),
                  pl.BlockSpec((tm, d), lambda i: (jnp.minimum(i, nct - 1), 0)),
                  pl.BlockSpec((1, d), lambda i: (0, 0)),
                  pl.BlockSpec(mod.shape, lambda i: (0, 0))],
        out_specs=pl.BlockSpec((tm, d), lambda i: (i, 0)),
        compiler_params=_cparams(("arbitrary",)),
        name="adaln_norm",
    )(x2, ctx2, g.reshape(1, d), mod)


def _rope(a, cos, sin):
    lane = lax.broadcasted_iota(jnp.int32, a.shape, 1)
    first = (lane % A_HEAD_DIM) < (A_HEAD_DIM // 2)
    rot = jnp.where(first, pltpu.roll(a, LANES - 32, 1), pltpu.roll(a, 32, 1))
    return a * cos + rot * sin


def _proj_qkv_kernel(h_ref, w_ref, cos_ref, sin_ref, q_ref, k_ref, v_ref):
    acc = jnp.dot(h_ref[...], w_ref[...], preferred_element_type=_f32)
    cos = cos_ref[...]
    sin = sin_ref[...]
    scale = A_HEAD_DIM ** -0.5 * LOG2E
    for j in range(A_WIDTH // LANES):
        sl = slice(j * LANES, (j + 1) * LANES)
        q_ref[:, sl] = (_rope(acc[:, sl], cos, sin) * scale).astype(q_ref.dtype)
    k_ref[...] = _rope(acc[:, A_WIDTH:A_WIDTH + LANES], cos, sin).astype(k_ref.dtype)
    v_ref[...] = acc[:, A_WIDTH + LANES:A_WIDTH + 2 * LANES].astype(v_ref.dtype)


def _proj_heads_kernel(h_ref, w_ref, o_ref, *, act):
    acc = jnp.dot(h_ref[...], w_ref[...], preferred_element_type=_f32)
    if act == "silu":
        acc = acc * _sigmoid(acc)
    for hh in range(o_ref.shape[0]):
        o_ref[hh] = acc[:, hh * LANES:(hh + 1) * LANES].astype(o_ref.dtype)


def _proj_gate_kernel(h_ref, w_ref, lb_ref, lf_ref, kk_ref):
    z = jnp.dot(h_ref[...], w_ref[0], preferred_element_type=_f32)
    lb = lb_ref[0]
    sg = _sigmoid(z)
    logf = jnp.log2(lb + (1.0 - lb) * sg)
    kk = (1.0 - lb) * _sigmoid(-z)
    for hh in range(lf_ref.shape[1]):
        sl = slice(hh * LANES, (hh + 1) * LANES)
        lf_ref[0, hh] = logf[:, sl]
        kk_ref[0, hh] = kk[:, sl].astype(kk_ref.dtype)


def _proj_sigmoid_kernel(h_ref, w_ref, o_ref):
    acc = jnp.dot(h_ref[...], w_ref[...], preferred_element_type=_f32)
    o_ref[...] = _sigmoid(acc).astype(o_ref.dtype)


def _row_tile(t):
    for tm in (1280, 640, 256, 128):
        if t % tm == 0:
            return tm
    raise ValueError(t)


def _proj_qkv(h, w, cos_t, sin_t):
    t, d = h.shape
    tm = _row_tile(t)
    ncol = w.shape[1]
    return pl.pallas_call(
        _proj_qkv_kernel,
        out_shape=(jax.ShapeDtypeStruct((t, A_WIDTH), _bf16),
                   jax.ShapeDtypeStruct((t, A_KV_WIDTH), _bf16),
                   jax.ShapeDtypeStruct((t, A_KV_WIDTH), _bf16)),
        grid=(t // tm,),
        in_specs=[pl.BlockSpec((tm, d), lambda i: (i, 0)),
                  pl.BlockSpec((d, ncol), lambda i: (0, 0)),
                  pl.BlockSpec((tm, LANES), lambda i: (i, 0)),
                  pl.BlockSpec((tm, LANES), lambda i: (i, 0))],
        out_specs=(pl.BlockSpec((tm, A_WIDTH), lambda i: (i, 0)),
                   pl.BlockSpec((tm, A_KV_WIDTH), lambda i: (i, 0)),
                   pl.BlockSpec((tm, A_KV_WIDTH), lambda i: (i, 0))),
        compiler_params=_cparams(("arbitrary",)),
        name="proj_qkv",
    )(h, w, cos_t, sin_t)


def _proj_heads(h, w, act):
    t, d = h.shape
    tm = _row_tile(t)
    tn = 512
    nh = tn // LANES
    return pl.pallas_call(
        functools.partial(_proj_heads_kernel, act=act),
        out_shape=jax.ShapeDtypeStruct((w.shape[1] // LANES, t, LANES), _bf16),
        grid=(w.shape[1] // tn, t // tm),
        in_specs=[pl.BlockSpec((tm, d), lambda j, i: (i, 0)),
                  pl.BlockSpec((d, tn), lambda j, i: (0, j))],
        out_specs=pl.BlockSpec((nh, tm, LANES), lambda j, i: (j, i, 0)),
        compiler_params=_cparams(("arbitrary", "arbitrary")),
        name="proj_heads_" + act,
    )(h, w)


def _proj_gate(h, w2, lb2):
    t, d = h.shape
    tm = _row_tile(t)
    tn = 512
    nh = tn // LANES
    ncb = w2.shape[2] // tn
    out_sds = lambda dt: jax.ShapeDtypeStruct((2, w2.shape[2] // LANES, t, LANES), dt)
    return pl.pallas_call(
        _proj_gate_kernel,
        out_shape=(out_sds(_f32), out_sds(_bf16)),
        grid=(2, ncb, t // tm),
        in_specs=[pl.BlockSpec((tm, d), lambda r, j, i: (i, 0)),
                  pl.BlockSpec((1, d, tn), lambda r, j, i: (r, 0, j)),
                  pl.BlockSpec((1, 1, tn), lambda r, j, i: (r, 0, j))],
        out_specs=(pl.BlockSpec((1, nh, tm, LANES), lambda r, j, i: (r, j, i, 0)),
                   pl.BlockSpec((1, nh, tm, LANES), lambda r, j, i: (r, j, i, 0))),
        compiler_params=_cparams(("arbitrary", "arbitrary", "arbitrary")),
        name="proj_gate",
    )(h, w2, lb2)


def _proj_sigmoid(h, w):
    t, d = h.shape
    tm = _row_tile(t)
    tn = 512
    return pl.pallas_call(
        _proj_sigmoid_kernel,
        out_shape=jax.ShapeDtypeStruct((t, w.shape[1]), _bf16),
        grid=(w.shape[1] // tn, t // tm),
        in_specs=[pl.BlockSpec((tm, d), lambda j, i: (i, 0)),
                  pl.BlockSpec((d, tn), lambda j, i: (0, j))],
        out_specs=pl.BlockSpec((tm, tn), lambda j, i: (i, j)),
        compiler_params=_cparams(("arbitrary", "arbitrary")),
        name="proj_sigmoid",
    )(h, w)


def _attn_kernel(q_ref, kp_ref, kc_ref, kn_ref, vp_ref, vc_ref, vn_ref,
                 kx_ref, vx_ref, bias_ref, sink_ref, o_ref):
    hd = A_HEAD_DIM
    blk = A_BLOCK
    nt = (((1,), (1,)), ((), ()))
    n_ctx = kx_ref.shape[0]
    ones_x = jnp.ones((n_ctx, LANES), _bf16)
    ones_b = jnp.ones((2 * blk, LANES), _bf16)
    for g in range(A_KV_HEADS):
        gs = slice(g * hd, (g + 1) * hd)
        qg = jnp.concatenate(
            [q_ref[:, (g * A_GROUP + h) * hd:(g * A_GROUP + h + 1) * hd] for h in range(A_GROUP)],
            axis=0)
        k_pc = jnp.concatenate([kp_ref[:, gs], kc_ref[:, gs]], axis=0)
        v_pc = jnp.concatenate([vp_ref[:, gs], vc_ref[:, gs]], axis=0)
        s_x = lax.dot_general(qg, kx_ref[:, gs], nt, preferred_element_type=_f32)
        s_pc = lax.dot_general(qg, k_pc, nt, preferred_element_type=_f32)
        s_n = lax.dot_general(qg, kn_ref[:, gs], nt, preferred_element_type=_f32) + bias_ref[0, 1]
        s_p = s_pc[:, 0:blk] + bias_ref[0, 0]
        s_c = s_pc[:, blk:2 * blk]
        mx = jnp.maximum(jnp.maximum(s_p, s_c), s_n)
        for j in range(n_ctx // LANES):
            mx = jnp.maximum(mx, s_x[:, j * LANES:(j + 1) * LANES])
        sink = sink_ref[g]
        m = jnp.maximum(jnp.max(mx, axis=-1, keepdims=True), sink)
        p_x = jnp.exp2(s_x - m).astype(_bf16)
        p_pc = jnp.concatenate([jnp.exp2(s_p - m), jnp.exp2(s_c - m)], axis=1).astype(_bf16)
        p_n = jnp.exp2(s_n - m).astype(_bf16)
        o = (jnp.dot(p_x, vx_ref[:, gs], preferred_element_type=_f32)
             + jnp.dot(p_pc, v_pc, preferred_element_type=_f32)
             + jnp.dot(p_n, vn_ref[:, gs], preferred_element_type=_f32))
        den = (jnp.dot(p_x, ones_x, preferred_element_type=_f32)
               + jnp.dot(p_pc, ones_b, preferred_element_type=_f32)
               + jnp.dot(p_n, ones_b[0:blk], preferred_element_type=_f32))
        o = o / (den[:, 0:hd] + jnp.exp2(sink - m))
        for h in range(A_GROUP):
            c0 = (g * A_GROUP + h) * hd
            o_ref[:, c0:c0 + hd] = o[h * A_BLOCK:(h + 1) * A_BLOCK, :].astype(o_ref.dtype)


def _attention(q, k, v, sinks, n, l):
    blk = A_BLOCK
    nblk = n // blk
    off = l // blk
    rows = A_GROUP * blk
    r = np.arange(rows)[:, None] % blk
    j = np.arange(blk)[None, :]
    keep_prev = (j - blk - r) >= -WINDOW
    keep_next = (j + blk - r) <= WINDOW
    none = np.zeros_like(keep_prev)
    variants = [(none, keep_next), (keep_prev, keep_next), (keep_prev, none)]
    bias = jnp.asarray(np.where(np.array(variants), 0.0, NEG).astype(np.float32))
    sink_col = jnp.repeat(sinks.astype(_f32).reshape(A_KV_HEADS, A_GROUP) * LOG2E, blk, axis=1)
    sink_col = sink_col.reshape(A_KV_HEADS, rows, 1)

    def bias_idx(i):
        return (jnp.where(i == 0, 0, jnp.where(i == nblk - 1, 2, 1)), 0, 0, 0)

    kv_spec = lambda f: pl.BlockSpec((blk, A_KV_WIDTH), f)
    prev = lambda i: (jnp.maximum(i - 1, 0) + off, 0)
    cur = lambda i: (i + off, 0)
    nxt = lambda i: (jnp.minimum(i + 1, nblk - 1) + off, 0)
    return pl.pallas_call(
        _attn_kernel,
        out_shape=jax.ShapeDtypeStruct((n, A_WIDTH), _bf16),
        grid=(nblk,),
        in_specs=[pl.BlockSpec((blk, A_WIDTH), cur),
                  kv_spec(prev), kv_spec(cur), kv_spec(nxt),
                  kv_spec(prev), kv_spec(cur), kv_spec(nxt),
                  pl.BlockSpec((l, A_KV_WIDTH), lambda i: (0, 0)),
                  pl.BlockSpec((l, A_KV_WIDTH), lambda i: (0, 0)),
                  pl.BlockSpec((1, 2, rows, blk), bias_idx),
                  pl.BlockSpec((A_KV_HEADS, rows, 1), lambda i: (0, 0, 0))],
        out_specs=pl.BlockSpec((blk, A_WIDTH), lambda i: (i, 0)),
        compiler_params=_cparams(("arbitrary",)),
        name="window_attention",
    )(q, k, k, k, v, v, v, k, v, bias, sink_col)


def _split3(x):
    hi = x.astype(_bf16)
    r1 = x - hi.astype(_f32)
    mid = r1.astype(_bf16)
    lo = (r1 - mid.astype(_f32)).astype(_bf16)
    return hi, mid, lo


def _hgrn_intra_scores(q, k, b, emat, reverse):
    c = HG_CHUNK
    nsb = c // SUB
    nt = (((1,), (1,)), ((), ()))
    qb = (q * jnp.exp2(b)).astype(_bf16)
    b_tot = b[0:1, :] if reverse else b[c - 1:c, :]
    kend = (k * jnp.exp2(b_tot - b)).astype(_bf16)

    zero_row = jnp.zeros((1, LANES), _f32)
    zero_blk = jnp.zeros((SUB, LANES), _f32)
    y_rows, z_rows, w_cols = [], [], []
    for ib in range(nsb):
        r0 = ib * SUB
        bi = b[r0:r0 + SUB]
        qi = q[r0:r0 + SUB]
        pieces = []
        for s in range(SUB):
            bs = b[r0 + s:r0 + s + 1]
            ks = k[r0 + s:r0 + s + 1]
            pieces.append(qi * ks * jnp.exp2(jnp.minimum(bi - bs, 0.0)))
        y_rows.append(jnp.concatenate(pieces, axis=1))
        if reverse:
            lo_r, hi_r = r0 + SUB, c
            ref_b = b[r0 + SUB:r0 + SUB + 1] if ib < nsb - 1 else zero_row
        else:
            lo_r, hi_r = 0, r0
            ref_b = b[r0 - 1:r0] if ib > 0 else zero_row
        if hi_r > lo_r:
            qp = qi * jnp.exp2(bi - ref_b)
            kp = [k[lo_r:hi_r] * jnp.exp2(ref_b - b[lo_r:hi_r])]
            if lo_r > 0:
                kp.insert(0, jnp.zeros((lo_r, LANES), _f32))
            if hi_r < c:
                kp.append(jnp.zeros((c - hi_r, LANES), _f32))
            w_cols.append(jnp.concatenate(kp, axis=0) if len(kp) > 1 else kp[0])
        else:
            qp = zero_blk
            w_cols.append(jnp.zeros((c, LANES), _f32))
        z_rows.append(jnp.concatenate([qp if jb == ib else zero_blk for jb in range(nsb)], axis=1))
    y = jnp.concatenate(y_rows, axis=0).astype(_bf16)
    z = jnp.concatenate(z_rows, axis=0).astype(_bf16)
    w = jnp.concatenate(w_cols, axis=1).astype(_bf16)
    a_diag = jnp.dot(y, emat, preferred_element_type=_f32)
    a_off = lax.dot_general(z, w, nt, preferred_element_type=_f32)
    return a_diag, a_off, qb, kend, jnp.exp2(b_tot)


def _hgrn_intra_kernel(q_ref, k_ref, lf_ref, emat_ref, a_ref, qb_ref, ke_ref, dt_ref, *, reverse):
    c = HG_CHUNK
    nch = HG_BLOCK // c
    nh = q_ref.shape[0]
    ri = lax.broadcasted_iota(jnp.int32, (c, c), 0)
    ci = lax.broadcasted_iota(jnp.int32, (c, c), 1)
    tri = jnp.where((ci >= ri) if reverse else (ci <= ri), 1.0, 0.0).astype(_bf16)
    dmask = (ci // SUB == ri // SUB) & ((ci >= ri) if reverse else (ci <= ri))
    emat = emat_ref[...]

    def body(cc, carry):
        rows = pl.ds(pl.multiple_of(cc * c, c), c)
        lf_all = jnp.concatenate([lf_ref[0, h, rows, :] for h in range(nh)], axis=1)
        hi, mid, lo = _split3(lf_all)
        b_all = (jnp.dot(tri, hi, preferred_element_type=_f32)
                 + jnp.dot(tri, mid, preferred_element_type=_f32)
                 + jnp.dot(tri, lo, preferred_element_type=_f32))
        for h in range(nh):
            a_diag, a_off, qb, kend, dtot = _hgrn_intra_scores(
                q_ref[h, rows, :].astype(_f32), k_ref[0, h, rows, :].astype(_f32),
                b_all[:, h * LANES:(h + 1) * LANES], emat, reverse)
            qb_ref[h, rows, :] = qb
            ke_ref[h, rows, :] = kend
            dt_ref[cc, h:h + 1, :] = dtot
            a = a_off + jnp.where(dmask, a_diag[:, 0:c], 0.0)
            a_ref[h, rows, :] = jnp.concatenate([a, jnp.zeros_like(a)], axis=1).astype(a_ref.dtype)
        return carry

    lax.fori_loop(0, nch, body, 0)


def _hgrn_intra(rq, kk, lf, direction):
    nh, t, _ = rq.shape
    br = HG_BLOCK
    nch = br // HG_CHUNK
    reverse = direction == 1
    spec3 = pl.BlockSpec((nh, br, LANES), lambda s: (0, s, 0))
    spec4 = pl.BlockSpec((1, nh, br, LANES), lambda s: (direction, 0, s, 0))
    sds = jax.ShapeDtypeStruct((nh, t, LANES), _bf16)
    emat = (np.arange(SUB * LANES)[:, None] // LANES == np.arange(LANES)[None, :] % SUB)
    emat = jnp.asarray(emat.astype(np.float32), dtype=_bf16)
    return pl.pallas_call(
        functools.partial(_hgrn_intra_kernel, reverse=reverse),
        out_shape=(sds, sds, sds, jax.ShapeDtypeStruct((t // HG_CHUNK, nh, LANES), _f32)),
        grid=(t // br,),
        in_specs=[spec3, spec4, spec4, pl.BlockSpec(emat.shape, lambda s: (0, 0))],
        out_specs=(spec3, spec3, spec3, pl.BlockSpec((nch, nh, LANES), lambda s: (s, 0, 0))),
        compiler_params=_cparams(("arbitrary",)),
        name="hgrn2_intra_" + ("bwd" if reverse else "fwd"),
    )(rq, kk, lf, emat)


def _hgrn_state_kernel(af_ref, qf_ref, kf_ref, df_ref, vf_ref, ab_ref, qbk_ref, kb_ref, db_ref, vb_ref,
                       outf_ref, outb_ref, st_ref):
    c = HG_CHUNK
    nch = HG_BLOCK // c
    nh = qf_ref.shape[0]
    nt = (((1,), (1,)), ((), ()))
    tn = (((0,), (0,)), ((), ()))

    @pl.when(pl.program_id(0) == 0)
    def _():
        st_ref[...] = jnp.zeros(st_ref.shape, st_ref.dtype)

    dirs = ((0, False, af_ref, qf_ref, kf_ref, df_ref, vf_ref, outf_ref),
            (1, True, ab_ref, qbk_ref, kb_ref, db_ref, vb_ref, outb_ref))

    def body(cc, carry):
        for di, reverse, a_ref, q_ref, k_ref, d_ref, v_ref, out_ref in dirs:
            chunk = (nch - 1 - cc) if reverse else cc
            rows = pl.ds(pl.multiple_of(chunk * c, c), c)
            for h in range(nh):
                st = st_ref[di, h]
                v = v_ref[h, rows, :]
                o = (jnp.dot(a_ref[h, rows, 0:c], v, preferred_element_type=_f32)
                     + lax.dot_general(q_ref[h, rows, :], st.astype(_bf16), nt, preferred_element_type=_f32))
                out_ref[h, rows, :] = o.astype(out_ref.dtype)
                upd = lax.dot_general(v, k_ref[h, rows, :], tn, preferred_element_type=_f32)
                st_ref[di, h] = st * d_ref[chunk, pl.ds(h, 1), :] + upd
        return carry

    lax.fori_loop(0, nch, body, 0)


def _hgrn_state(intra_f, intra_b, rv, l):
    a_f, qb_f, ke_f, dt_f = intra_f
    a_b, qb_b, ke_b, dt_b = intra_b
    nh, t, _ = rv.shape
    br = HG_BLOCK
    nch = br // HG_CHUNK
    nb = t // br
    nc = l // br

    def blk_b(s):
        return jnp.where(s < nc, nc - 1 - s, nb - 1 - (s - nc))

    def specs(blk):
        s3 = pl.BlockSpec((nh, br, LANES), lambda s: (0, blk(s), 0))
        sd = pl.BlockSpec((nch, nh, LANES), lambda s: (blk(s), 0, 0))
        return [s3, s3, s3, sd, s3]

    fwd = specs(lambda s: s)
    bwd = specs(blk_b)
    return pl.pallas_call(
        _hgrn_state_kernel,
        out_shape=(jax.ShapeDtypeStruct((nh, t, LANES), _bf16),
                   jax.ShapeDtypeStruct((nh, t, LANES), _bf16)),
        grid=(nb,),
        in_specs=fwd + bwd,
        out_specs=(fwd[0], bwd[0]),
        scratch_shapes=[pltpu.VMEM((2, nh, B_DV, B_DK), _f32)],
        compiler_params=_cparams(("arbitrary",)),
        name="hgrn2_state",
    )(a_f, qb_f, ke_f, dt_f, rv, a_b, qb_b, ke_b, dt_b, rv)


def _merge_kernel(ya_ref, of_ref, ob_ref, rg_ref, ga_ref, gb_ref, x_ref, wa_ref, wb_ref, wo_ref,
                  wrh_ref, wrl_ref, br_ref, hg_ref, g2_ref, mod_ref,
                  x1_ref, h2_ref, route_ref, cnt_ref, run_ref):
    i = pl.program_id(0)
    tm, d = x_ref.shape

    @pl.when(i == 0)
    def _():
        run_ref[...] = jnp.zeros(run_ref.shape, run_ref.dtype)

    parts = []
    for h in range(B_HEADS):
        o = of_ref[h].astype(_f32) + ob_ref[h].astype(_f32)
        o = o * lax.rsqrt(jnp.mean(o * o, axis=-1, keepdims=True) + EPS)
        o = o * hg_ref[:, h * LANES:(h + 1) * LANES]
        parts.append((o * rg_ref[h].astype(_f32)).astype(_bf16))
    yb = jnp.concatenate(parts, axis=1)
    z0 = jnp.dot(ya_ref[...], wa_ref[...], preferred_element_type=_f32)
    z1 = jnp.dot(yb, wb_ref[...], preferred_element_type=_f32)
    merged = ga_ref[...].astype(_f32) * z0 + gb_ref[...].astype(_f32) * z1
    y = jnp.dot(merged.astype(_bf16), wo_ref[...], preferred_element_type=_f32)
    x1 = x_ref[...] + mod_ref[0:1, 2 * d:3 * d] * y
    x1_ref[...] = x1
    h2 = _rms_mod(x1, g2_ref[...], mod_ref[0:1, 3 * d:4 * d], mod_ref[0:1, 4 * d:5 * d])
    h2_ref[...] = h2

    hh = h2.astype(_bf16)
    hl = (h2 - hh.astype(_f32)).astype(_bf16)
    logits = (jnp.dot(hh, wrh_ref[...], preferred_element_type=_f32)
              + jnp.dot(hl, wrh_ref[...], preferred_element_type=_f32)
              + jnp.dot(hh, wrl_ref[...], preferred_element_type=_f32)) + br_ref[...]
    lane = lax.broadcasted_iota(jnp.int32, (tm, LANES), 1)
    work = jnp.where(lane < N_EXPERTS, logits, NEG)
    vals, idxs = [], []
    onehot = jnp.zeros((tm, LANES), _f32)
    for _ in range(TOP_K):
        m = jnp.max(work, axis=-1, keepdims=True)
        idx = jnp.min(jnp.where(work == m, lane, LANES), axis=-1, keepdims=True)
        sel = lane == idx
        vals.append(m)
        idxs.append(idx)
        onehot = jnp.where(sel, 1.0, onehot)
        work = jnp.where(sel, NEG, work)
    es = [jnp.exp(vv - vals[0]) for vv in vals]
    tot = es[0] + es[1] + es[2] + es[3]
    ri = lax.broadcasted_iota(jnp.int32, (tm, tm), 0)
    ci = lax.broadcasted_iota(jnp.int32, (tm, tm), 1)
    ltri = jnp.where(ci < ri, 1.0, 0.0).astype(_bf16)
    prefix = jnp.dot(ltri, onehot.astype(_bf16), preferred_element_type=_f32) + run_ref[...]
    route = jnp.zeros((tm, LANES), _f32)
    for j in range(TOP_K):
        rank = jnp.sum(jnp.where(lane == idxs[j], prefix, 0.0), axis=-1, keepdims=True)
        route = jnp.where(lane == j, idxs[j].astype(_f32), route)
        route = jnp.where(lane == TOP_K + j, es[j] / tot, route)
        route = jnp.where(lane == 2 * TOP_K + j, rank, route)
    route_ref[...] = route
    run_new = run_ref[...] + jnp.sum(onehot, axis=0, keepdims=True)
    run_ref[...] = run_new
    cnt_ref[...] = run_new


def _merge_route(ya, o_f, o_b, rg, gates, x2, wa, wb, wo, wr_hi, wr_lo, b_router, hg, g2, mod, l):
    n, d = x2.shape
    tm = 256
    off = l // tm
    const = lambda shape: pl.BlockSpec(shape, lambda i: (0,) * len(shape),
                                       pipeline_mode=pl.Buffered(1))
    head_spec = pl.BlockSpec((B_HEADS, tm, LANES), lambda i: (0, i + off, 0))
    return pl.pallas_call(
        _merge_kernel,
        out_shape=(jax.ShapeDtypeStruct((n, d), _f32),
                   jax.ShapeDtypeStruct((n, d), _f32),
                   jax.ShapeDtypeStruct((n, LANES), _f32),
                   jax.ShapeDtypeStruct((1, LANES), _f32)),
        grid=(n // tm,),
        in_specs=[pl.BlockSpec((tm, A_WIDTH), lambda i: (i, 0)),
                  head_spec, head_spec, head_spec,
                  pl.BlockSpec((tm, d), lambda i: (i + off, 0)),
                  pl.BlockSpec((tm, d), lambda i: (i + off, 1)),
                  pl.BlockSpec((tm, d), lambda i: (i, 0)),
                  const(wa.shape), const(wb.shape), const(wo.shape),
                  const(wr_hi.shape), const(wr_lo.shape), const((1, LANES)),
                  const((1, B_WIDTH)), const((1, d)), const(mod.shape)],
        out_specs=(pl.BlockSpec((tm, d), lambda i: (i, 0)),
                   pl.BlockSpec((tm, d), lambda i: (i, 0)),
                   pl.BlockSpec((tm, LANES), lambda i: (i, 0)),
                   pl.BlockSpec((1, LANES), lambda i: (0, 0))),
        scratch_shapes=[pltpu.VMEM((1, LANES), _f32)],
        compiler_params=_cparams(("arbitrary",)),
        name="merge_route",
    )(ya, o_f, o_b, rg, gates, gates, x2, wa, wb, wo, wr_hi, wr_lo, b_router, hg, g2, mod)


def _zero_fill_tiles(zero_ref, dst_ref, first, last, sem, cols=(None,)):
    tm = MOE_TM

    def copy(t, col):
        rows = pl.ds(pl.multiple_of(t * tm, tm), tm)
        dst = dst_ref.at[rows] if col is None else dst_ref.at[rows, col]
        return pltpu.make_async_copy(zero_ref, dst, sem)

    def start(t, carry):
        for col in cols:
            copy(t, col).start()
        return carry

    def wait(t, carry):
        for col in cols:
            copy(t, col).wait()
        return carry

    lax.fori_loop(first, last, start, 0)
    lax.fori_loop(first, last, wait, 0)


def _dispatch_kernel(lt_ref, nt_ref, dest_ref, h2_ref, xs_ref, zero_ref, sem, zsem):
    tm = h2_ref.shape[0]

    @pl.when(pl.program_id(0) == 0)
    def _():
        zero_ref[...] = jnp.zeros(zero_ref.shape, zero_ref.dtype)
        for e in range(N_EXPERTS):
            @pl.when(nt_ref[e] > 0)
            def _():
                pltpu.make_async_copy(zero_ref, xs_ref.at[pl.ds(lt_ref[e] * MOE_TM, MOE_TM)], zsem).start()
        for e in range(N_EXPERTS):
            @pl.when(nt_ref[e] > 0)
            def _():
                pltpu.make_async_copy(zero_ref, xs_ref.at[pl.ds(lt_ref[e] * MOE_TM, MOE_TM)], zsem).wait()
        _zero_fill_tiles(zero_ref, xs_ref, lt_ref[N_EXPERTS - 1] + 1, xs_ref.shape[0] // MOE_TM, zsem)

    def issue(t, carry):
        for j in range(TOP_K):
            dst = dest_ref[t * TOP_K + j]
            pltpu.make_async_copy(h2_ref.at[pl.ds(t, 1)], xs_ref.at[pl.ds(dst, 1)], sem).start(priority=j % 2)
        return carry

    lax.fori_loop(0, tm, issue, 0)
    for _ in range(TOP_K):
        pltpu.make_async_copy(h2_ref, xs_ref.at[pl.ds(0, tm)], sem).wait()


def _dispatch(h2, dest_flat, last_tile, ntile, n_slots):
    n, d = h2.shape
    tm = 256
    return pl.pallas_call(
        _dispatch_kernel,
        out_shape=jax.ShapeDtypeStruct((n_slots, d), h2.dtype),
        grid_spec=pltpu.PrefetchScalarGridSpec(
            num_scalar_prefetch=2,
            grid=(n // tm,),
            in_specs=[pl.BlockSpec((tm * TOP_K,), lambda i, lt, nt: (i,), memory_space=pltpu.SMEM),
                      pl.BlockSpec((tm, d), lambda i, lt, nt: (i, 0))],
            out_specs=pl.BlockSpec(memory_space=pl.ANY),
            scratch_shapes=[pltpu.VMEM((MOE_TM, d), h2.dtype), pltpu.SemaphoreType.DMA,
                            pltpu.SemaphoreType.DMA]),
        compiler_params=_cparams(("arbitrary",)),
        name="moe_dispatch",
    )(last_tile, ntile, dest_flat, h2)


def _cast_rows(src_ref, dst_ref):
    rows = dst_ref.shape[0]
    step = 256

    def body(i, carry):
        r = pl.ds(pl.multiple_of(i * step, step), step)
        dst_ref[r, :] = src_ref[0, r, :].astype(dst_ref.dtype)
        return carry

    lax.fori_loop(0, rows // step, body, 0)


def _expert_tile_loop(n_t, in_copy, out_copy, compute):
    @pl.when(n_t > 0)
    def _():
        in_copy(0, 0).start()

    def body(t, carry):
        slot = t % 2
        in_copy(t, slot).wait()

        @pl.when(t + 1 < n_t)
        def _():
            in_copy(t + 1, 1 - slot).start()

        @pl.when(t >= 2)
        def _():
            out_copy(t - 2, slot).wait()

        compute(slot)
        out_copy(t, slot).start()
        return carry

    lax.fori_loop(0, n_t, body, 0)

    @pl.when(n_t >= 2)
    def _():
        out_copy(n_t - 2, n_t % 2).wait()

    @pl.when(n_t >= 1)
    def _():
        out_copy(n_t - 1, (n_t - 1) % 2).wait()


def _moe_up_kernel(ts_ref, nt_ref, xs_ref, wg_ref, wu_ref, bg_ref, bu_ref, act_ref,
                   wgb_ref, wub_ref, xbuf, obuf, sem_in, sem_out):
    e = pl.program_id(0)
    c = pl.program_id(1)
    tm = MOE_TM
    fc = wgb_ref.shape[1]
    n_t = nt_ref[e]
    t0 = ts_ref[e]
    _cast_rows(wg_ref, wgb_ref)
    _cast_rows(wu_ref, wub_ref)
    col = pl.ds(pl.multiple_of(c * fc, fc), fc)

    def rows(t):
        return pl.ds(pl.multiple_of((t0 + t) * tm, tm), tm)

    def in_copy(t, slot):
        return pltpu.make_async_copy(xs_ref.at[rows(t)], xbuf.at[slot], sem_in.at[slot])

    def out_copy(t, slot):
        return pltpu.make_async_copy(obuf.at[slot], act_ref.at[rows(t), col], sem_out.at[slot])

    def compute(slot):
        x = xbuf[slot].astype(_bf16)
        g = jnp.dot(x, wgb_ref[...], preferred_element_type=_f32) + bg_ref[0]
        u = jnp.dot(x, wub_ref[...], preferred_element_type=_f32) + bu_ref[0]
        g = jnp.minimum(g, SWIGLU_LIMIT)
        u = jnp.clip(u, -SWIGLU_LIMIT, SWIGLU_LIMIT)
        obuf[slot] = (g * _sigmoid(SWIGLU_ALPHA * g) * (u + 1.0)).astype(obuf.dtype)

    _expert_tile_loop(n_t, in_copy, out_copy, compute)

    @pl.when((e == pl.num_programs(0) - 1) & (c == pl.num_programs(1) - 1))
    def _():
        obuf[0] = jnp.zeros(obuf.shape[1:], obuf.dtype)
        n_col = act_ref.shape[1] // fc
        _zero_fill_tiles(obuf.at[0], act_ref, t0 + n_t, act_ref.shape[0] // tm, sem_out.at[0],
                         cols=tuple(pl.ds(j * fc, fc) for j in range(n_col)))


def _moe_up(xs, wg, wu, bg, bu, tstart, ntile):
    n_slots, d = xs.shape
    ne, _, ff = wg.shape
    fc = ff // 2
    tm = MOE_TM
    w_spec = pl.BlockSpec((1, d, fc), lambda e, c, ts, nt: (e, 0, c))
    b_spec = pl.BlockSpec((1, 1, fc), lambda e, c, ts, nt: (e, 0, c))
    return pl.pallas_call(
        _moe_up_kernel,
        out_shape=jax.ShapeDtypeStruct((n_slots, ff), _bf16),
        grid_spec=pltpu.PrefetchScalarGridSpec(
            num_scalar_prefetch=2,
            grid=(ne, ff // fc),
            in_specs=[pl.BlockSpec(memory_space=pl.ANY), w_spec, w_spec, b_spec, b_spec],
            out_specs=pl.BlockSpec(memory_space=pl.ANY),
            scratch_shapes=[pltpu.VMEM((d, fc), _bf16), pltpu.VMEM((d, fc), _bf16),
                            pltpu.VMEM((2, tm, d), xs.dtype), pltpu.VMEM((2, tm, fc), _bf16),
                            pltpu.SemaphoreType.DMA((2,)), pltpu.SemaphoreType.DMA((2,))]),
        compiler_params=_cparams(("arbitrary", "arbitrary")),
        name="moe_gate_up",
    )(tstart, ntile, xs, wg, wu, bg, bu)


def _moe_down_kernel(ts_ref, nt_ref, act_ref, wd_ref, bd_ref, ys_ref, wdb_ref, abuf, ybuf, sem_in, sem_out):
    e = pl.program_id(0)
    c = pl.program_id(1)
    tm = MOE_TM
    dc = wdb_ref.shape[1]
    n_t = nt_ref[e]
    t0 = ts_ref[e]
    _cast_rows(wd_ref, wdb_ref)
    col = pl.ds(pl.multiple_of(c * dc, dc), dc)

    def rows(t):
        return pl.ds(pl.multiple_of((t0 + t) * tm, tm), tm)

    def in_copy(t, slot):
        return pltpu.make_async_copy(act_ref.at[rows(t)], abuf.at[slot], sem_in.at[slot])

    def out_copy(t, slot):
        return pltpu.make_async_copy(ybuf.at[slot], ys_ref.at[rows(t), col], sem_out.at[slot])

    def compute(slot):
        ybuf[slot] = jnp.dot(abuf[slot], wdb_ref[...], preferred_element_type=_f32) + bd_ref[0]

    _expert_tile_loop(n_t, in_copy, out_copy, compute)

    @pl.when((e == pl.num_programs(0) - 1) & (c == pl.num_programs(1) - 1))
    def _():
        ybuf[0] = jnp.zeros(ybuf.shape[1:], ybuf.dtype)
        n_col = ys_ref.shape[1] // dc
        _zero_fill_tiles(ybuf.at[0], ys_ref, t0 + n_t, ys_ref.shape[0] // tm, sem_out.at[0],
                         cols=tuple(pl.ds(j * dc, dc) for j in range(n_col)))


def _moe_down(act, wd, bd, tstart, ntile):
    n_slots, ff = act.shape
    ne, _, d = wd.shape
    dc = d // 2
    tm = MOE_TM
    return pl.pallas_call(
        _moe_down_kernel,
        out_shape=jax.ShapeDtypeStruct((n_slots, d), _f32),
        grid_spec=pltpu.PrefetchScalarGridSpec(
            num_scalar_prefetch=2,
            grid=(ne, d // dc),
            in_specs=[pl.BlockSpec(memory_space=pl.ANY),
                      pl.BlockSpec((1, ff, dc), lambda e, c, ts, nt: (e, 0, c)),
                      pl.BlockSpec((1, 1, dc), lambda e, c, ts, nt: (e, 0, c))],
            out_specs=pl.BlockSpec(memory_space=pl.ANY),
            scratch_shapes=[pltpu.VMEM((ff, dc), _bf16),
                            pltpu.VMEM((2, tm, ff), _bf16), pltpu.VMEM((2, tm, dc), _f32),
                            pltpu.SemaphoreType.DMA((2,)), pltpu.SemaphoreType.DMA((2,))]),
        compiler_params=_cparams(("arbitrary", "arbitrary")),
        name="moe_down",
    )(tstart, ntile, act, wd, bd)


def _combine_kernel(dest_ref, ys_ref, route_ref, x1_ref, mod_ref, gf_ref, o_ref, buf_ref, sem):
    tm, d = x1_ref.shape

    def issue(t, carry):
        for j in range(TOP_K):
            src = dest_ref[t * TOP_K + j]
            pltpu.make_async_copy(ys_ref.at[pl.ds(src, 1)], buf_ref.at[j, pl.ds(t, 1)], sem).start(
                priority=j % 2)
        return carry

    lax.fori_loop(0, tm, issue, 0)
    for j in range(TOP_K):
        pltpu.make_async_copy(ys_ref.at[pl.ds(0, tm)], buf_ref.at[j], sem).wait()
    acc = jnp.zeros((tm, d), _f32)
    for j in range(TOP_K):
        acc = acc + route_ref[:, TOP_K + j:TOP_K + j + 1] * buf_ref[j]
    x2 = x1_ref[...] + mod_ref[0:1, 5 * d:6 * d] * acc
    y = x2 * lax.rsqrt(jnp.mean(x2 * x2, axis=-1, keepdims=True) + EPS)
    o_ref[...] = y * gf_ref[...]


def _combine(ys, dest_flat, route, x1, mod, gf):
    n, d = x1.shape
    tm = 256
    return pl.pallas_call(
        _combine_kernel,
        out_shape=jax.ShapeDtypeStruct((n, d), _f32),
        grid=(n // tm,),
        in_specs=[pl.BlockSpec((tm * TOP_K,), lambda i: (i,), memory_space=pltpu.SMEM),
                  pl.BlockSpec(memory_space=pl.ANY),
                  pl.BlockSpec((tm, LANES), lambda i: (i, 0)),
                  pl.BlockSpec((tm, d), lambda i: (i, 0)),
                  pl.BlockSpec(mod.shape, lambda i: (0, 0)),
                  pl.BlockSpec((1, d), lambda i: (0, 0))],
        out_specs=pl.BlockSpec((tm, d), lambda i: (i, 0)),
        scratch_shapes=[pltpu.VMEM((TOP_K, tm, d), _f32), pltpu.SemaphoreType.DMA],
        compiler_params=_cparams(("arbitrary",)),
        name="moe_combine",
    )(dest_flat, ys, route, x1, mod, gf.reshape(1, d))


def _rope_tables(n, l):
    rows = n // GRID_W
    inv_freq = ROPE_BASE ** (-jnp.arange(ROPE_PAIRS_PER_AXIS, dtype=_f32) / ROPE_PAIRS_PER_AXIS)
    ang_r = jnp.arange(rows, dtype=_f32)[:, None] * inv_freq
    ang_c = jnp.arange(GRID_W, dtype=_f32)[:, None] * inv_freq
    rep = lambda t_r, t_c: jnp.concatenate(
        [jnp.repeat(t_r, GRID_W, axis=0), jnp.tile(t_c, (rows, 1))], axis=-1)
    cos, sin = rep(jnp.cos(ang_r), jnp.cos(ang_c)), rep(jnp.sin(ang_r), jnp.sin(ang_c))
    cos_t = jnp.tile(cos, (1, LANES // 32))
    sin_t = jnp.tile(jnp.concatenate([-sin, sin], axis=-1), (1, LANES // A_HEAD_DIM))
    cos_t = jnp.concatenate([jnp.ones((l, LANES), _f32), cos_t], axis=0)
    sin_t = jnp.concatenate([jnp.zeros((l, LANES), _f32), sin_t], axis=0)
    return cos_t, sin_t


def _routing_tables(route, counts):
    tm = MOE_TM
    idx = route[:, 0:TOP_K].astype(jnp.int32)
    rank = route[:, 2 * TOP_K:3 * TOP_K].astype(jnp.int32)
    cnt = counts[0, :N_EXPERTS].astype(jnp.int32)
    ntile = (cnt + tm - 1) // tm
    tstart = jnp.cumsum(ntile) - ntile
    slot0 = tstart * tm
    onehot = idx[:, :, None] == jnp.arange(N_EXPERTS, dtype=jnp.int32)
    dest = jnp.sum(jnp.where(onehot, slot0, 0), axis=-1) + rank
    return dest.reshape(-1), tstart, ntile


def kernel(x, c, ctx, c_ctx, norm_mix_g, norm_ffn_g, w_mod, b_mod, w_in, attn_sinks,
           lb_fwd_logits, lb_bwd_logits, hgrn_norm_g, w_branch, w_out, w_router, b_router,
           w_e_gate, b_e_gate, w_e_up, b_e_up, w_e_down, b_e_down, final_norm_g):
    bsz, n, d = x.shape
    l = ctx.shape[1]
    assert bsz == 1 and d == D_MODEL and norm_mix_g.shape[0] == 1
    assert n % HG_BLOCK == 0 and l % HG_BLOCK == 0 and n // A_BLOCK >= 2
    x2 = x.reshape(n, d)
    ctx2 = ctx.reshape(l, d)

    mod = _mod_vectors(c, c_ctx, w_mod[0], b_mod[0])
    h_all = _norm_all(x2, ctx2, norm_mix_g[0], mod)

    w = w_in[0].astype(_bf16)
    c0 = 0
    segs = []
    for width in (A_WIDTH + 2 * A_KV_WIDTH, B_WIDTH, B_WIDTH, B_WIDTH, B_WIDTH, B_WIDTH, 2 * d):
        segs.append(w[:, c0:c0 + width])
        c0 += width
    w_qkv, w_rq, w_zf, w_zb, w_rv, w_rg, w_gates = segs
    cos_t, sin_t = _rope_tables(n, l)
    q, k, v = _proj_qkv(h_all, w_qkv, cos_t, sin_t)
    rq = _proj_heads(h_all, w_rq, "silu")
    rv = _proj_heads(h_all, w_rv, "none")
    rg = _proj_heads(h_all, w_rg, "silu")
    lb_f = jax.nn.softmax(lb_fwd_logits.astype(_f32), axis=0)[0]
    lb_b = jax.nn.softmax(lb_bwd_logits.astype(_f32), axis=0)[0]
    lb2 = jnp.stack([lb_f, lb_b]).reshape(2, 1, B_WIDTH)
    lf, kk = _proj_gate(h_all, jnp.stack([w_zf, w_zb]), lb2)
    gates = _proj_sigmoid(h_all, w_gates)

    ya = _attention(q, k, v, attn_sinks[0], n, l)
    o_f, o_b = _hgrn_state(_hgrn_intra(rq, kk, lf, 0), _hgrn_intra(rq, kk, lf, 1), rv, l)

    wr = jnp.zeros((d, LANES), _f32).at[:, :N_EXPERTS].set(w_router[0].astype(_f32))
    wr_hi = wr.astype(_bf16)
    wr_lo = (wr - wr_hi.astype(_f32)).astype(_bf16)
    br = jnp.zeros((1, LANES), _f32).at[0, :N_EXPERTS].set(b_router[0].astype(_f32))
    x1, h2, route, counts = _merge_route(
        ya, o_f, o_b, rg, gates, x2,
        w_branch[0, 0].astype(_bf16), w_branch[0, 1].astype(_bf16), w_out[0].astype(_bf16),
        wr_hi, wr_lo, br, hgrn_norm_g[0].reshape(1, B_WIDTH).astype(_f32),
        norm_ffn_g[0].reshape(1, d), mod, l)

    n_tiles = (n * TOP_K) // MOE_TM + N_EXPERTS
    dest, tstart, ntile = _routing_tables(route, counts)
    xs = _dispatch(h2, dest, tstart + ntile - 1, ntile, n_tiles * MOE_TM)
    act = _moe_up(xs, w_e_gate[0], w_e_up[0],
                  b_e_gate[0].reshape(N_EXPERTS, 1, EXPERT_FF), b_e_up[0].reshape(N_EXPERTS, 1, EXPERT_FF),
                  tstart, ntile)
    ys = _moe_down(act, w_e_down[0], b_e_down[0].reshape(N_EXPERTS, 1, d), tstart, ntile)
    out = _combine(ys, dest, route, x1, mod, final_norm_g)
    return out.reshape(bsz, n, d)
```

```python
import functools

import jax
import jax.numpy as jnp
import numpy as np
from jax import lax
from jax.experimental import pallas as pl
from jax.experimental.pallas import tpu as pltpu

D_MODEL = 2048
GRID_W = 64
EPS = 1e-6
A_HEADS = 16
A_KV_HEADS = 2
A_GROUP = A_HEADS // A_KV_HEADS
A_HEAD_DIM = 64
A_WIDTH = A_HEADS * A_HEAD_DIM
A_KV_WIDTH = A_KV_HEADS * A_HEAD_DIM
WINDOW = 128
A_BLOCK = 128
ROPE_BASE = 10000.0
ROPE_PAIRS_PER_AXIS = A_HEAD_DIM // 4
B_HEADS = 8
B_DK = 128
B_DV = 128
B_WIDTH = B_HEADS * B_DV
N_EXPERTS = 32
TOP_K = 4
EXPERT_FF = 2048
SWIGLU_LIMIT = 7.0
SWIGLU_ALPHA = 1.702

LANES = 128
SUB = 8
HG_CHUNK = 64
HG_BLOCK = 256
MOE_TM = 256
NEG = -1e30
LOG2E = 1.4426950408889634
VMEM_LIMIT = 56 * 1024 * 1024

_f32 = jnp.float32
_bf16 = jnp.bfloat16


def _cparams(sem):
    return pltpu.CompilerParams(dimension_semantics=sem, vmem_limit_bytes=VMEM_LIMIT)


def _sigmoid(x):
    return 1.0 / (1.0 + jnp.exp(-x))


def _mod_kernel(s_ref, w_ref, b_ref, o_ref):
    tn = w_ref.shape[1]
    for r in range(2):
        s = s_ref[r]
        s = s * _sigmoid(s)
        for j in range(tn // LANES):
            sl = slice(j * LANES, (j + 1) * LANES)
            acc = jnp.sum(w_ref[:, sl] * s, axis=0, keepdims=True)
            o_ref[r:r + 1, sl] = acc + b_ref[:, sl]


def _mod_vectors(c, c_ctx, w_mod, b_mod):
    d, n_out = w_mod.shape
    tn = 1024
    s = jnp.stack([c.reshape(d), c_ctx.reshape(d)]).astype(_f32)
    s = jnp.broadcast_to(s[:, :, None], (2, d, LANES))
    return pl.pallas_call(
        _mod_kernel,
        out_shape=jax.ShapeDtypeStruct((2, n_out), _f32),
        grid=(n_out // tn,),
        in_specs=[pl.BlockSpec((2, d, LANES), lambda j: (0, 0, 0)),
                  pl.BlockSpec((d, tn), lambda j: (0, j)),
                  pl.BlockSpec((1, tn), lambda j: (0, j))],
        out_specs=pl.BlockSpec((2, tn), lambda j: (0, j)),
        compiler_params=_cparams(("arbitrary",)),
        name="mod_vectors",
    )(s, w_mod, b_mod.reshape(1, n_out))


def _rms_mod(xf, g, shift, scale):
    y = xf * lax.rsqrt(jnp.mean(xf * xf, axis=-1, keepdims=True) + EPS)
    return (y * g) * (1.0 + scale) + shift


def _norm_kernel(x_ref, ctx_ref, g_ref, mod_ref, o_ref, *, n_ctx_tiles):
    i = pl.program_id(0)
    d = x_ref.shape[1]

    @pl.when(i < n_ctx_tiles)
    def _():
        o_ref[...] = _rms_mod(ctx_ref[...], g_ref[...], mod_ref[1:2, 0:d],
                              mod_ref[1:2, d:2 * d]).astype(o_ref.dtype)

    @pl.when(i >= n_ctx_tiles)
    def _():
        o_ref[...] = _rms_mod(x_ref[...], g_ref[...], mod_ref[0:1, 0:d],
                              mod_ref[0:1, d:2 * d]).astype(o_ref.dtype)


def _norm_all(x2, ctx2, g, mod):
    n, d = x2.shape
    l = ctx2.shape[0]
    tm = 256
    nct = l // tm
    return pl.pallas_call(
        functools.partial(_norm_kernel, n_ctx_tiles=nct),
        out_shape=jax.ShapeDtypeStruct((l + n, d), _bf16),
        grid=((l + n) // tm,),
        in_specs=[pl.BlockSpec((tm, d), lambda i: (jnp.maximum(i - nct, 0), 0)),
                  pl.BlockSpec((tm, d), lambda i: (jnp.minimum(i, nct - 1), 0)),
                  pl.BlockSpec((1, d), lambda i: (0, 0)),
                  pl.BlockSpec(mod.shape, lambda i: (0, 0))],
        out_specs=pl.BlockSpec((tm, d), lambda i: (i, 0)),
        compiler_params=_cparams(("arbitrary",)),
        name="adaln_norm",
    )(x2, ctx2, g.reshape(1, d), mod)


def _rope(a, cos, sin):
    lane = lax.broadcasted_iota(jnp.int32, a.shape, 1)
    first = (lane % A_HEAD_DIM) < (A_HEAD_DIM // 2)
    rot = jnp.where(first, pltpu.roll(a, LANES - 32, 1), pltpu.roll(a, 32, 1))
    return a * cos + rot * sin


def _proj_qkv_kernel(h_ref, w_ref, cos_ref, sin_ref, q_ref, k_ref, v_ref):
    acc = jnp.dot(h_ref[...], w_ref[...], preferred_element_type=_f32)
    cos = cos_ref[...]
    sin = sin_ref[...]
    scale = A_HEAD_DIM ** -0.5 * LOG2E
    for j in range(A_WIDTH // LANES):
        sl = slice(j * LANES, (j + 1) * LANES)
        q_ref[:, sl] = (_rope(acc[:, sl], cos, sin) * scale).astype(q_ref.dtype)
    k_ref[...] = _rope(acc[:, A_WIDTH:A_WIDTH + LANES], cos, sin).astype(k_ref.dtype)
    v_ref[...] = acc[:, A_WIDTH + LANES:A_WIDTH + 2 * LANES].astype(v_ref.dtype)


def _proj_heads_kernel(h_ref, w_ref, o_ref, *, act):
    acc = jnp.dot(h_ref[...], w_ref[...], preferred_element_type=_f32)
    if act == "silu":
        acc = acc * _sigmoid(acc)
    for hh in range(o_ref.shape[0]):
        o_ref[hh] = acc[:, hh * LANES:(hh + 1) * LANES].astype(o_ref.dtype)


def _proj_gate_kernel(h_ref, w_ref, lb_ref, lf_ref, kk_ref):
    z = jnp.dot(h_ref[...], w_ref[0], preferred_element_type=_f32)
    lb = lb_ref[0]
    sg = _sigmoid(z)
    logf = jnp.log2(lb + (1.0 - lb) * sg)
    kk = (1.0 - lb) * _sigmoid(-z)
    for hh in range(lf_ref.shape[1]):
        sl = slice(hh * LANES, (hh + 1) * LANES)
        lf_ref[0, hh] = logf[:, sl]
        kk_ref[0, hh] = kk[:, sl].astype(kk_ref.dtype)


def _proj_sigmoid_kernel(h_ref, w_ref, o_ref):
    acc = jnp.dot(h_ref[...], w_ref[...], preferred_element_type=_f32)
    o_ref[...] = _sigmoid(acc).astype(o_ref.dtype)


def _row_tile(t):
    for tm in (1280, 640, 256, 128):
        if t % tm == 0:
            return tm
    raise ValueError(t)


def _proj_qkv(h, w, cos_t, sin_t):
    t, d = h.shape
    tm = _row_tile(t)
    ncol = w.shape[1]
    return pl.pallas_call(
        _proj_qkv_kernel,
        out_shape=(jax.ShapeDtypeStruct((t, A_WIDTH), _bf16),
                   jax.ShapeDtypeStruct((t, A_KV_WIDTH), _bf16),
                   jax.ShapeDtypeStruct((t, A_KV_WIDTH), _bf16)),
        grid=(t // tm,),
        in_specs=[pl.BlockSpec((tm, d), lambda i: (i, 0)),
                  pl.BlockSpec((d, ncol), lambda i: (0, 0)),
                  pl.BlockSpec((tm, LANES), lambda i: (i, 0)),
                  pl.BlockSpec((tm, LANES), lambda i: (i, 0))],
        out_specs=(pl.BlockSpec((tm, A_WIDTH), lambda i: (i, 0)),
                   pl.BlockSpec((tm, A_KV_WIDTH), lambda i: (i, 0)),
                   pl.BlockSpec((tm, A_KV_WIDTH), lambda i: (i, 0))),
        compiler_params=_cparams(("arbitrary",)),
        name="proj_qkv",
    )(h, w, cos_t, sin_t)


def _proj_heads(h, w, act):
    t, d = h.shape
    tm = _row_tile(t)
    tn = 512
    nh = tn // LANES
    return pl.pallas_call(
        functools.partial(_proj_heads_kernel, act=act),
        out_shape=jax.ShapeDtypeStruct((w.shape[1] // LANES, t, LANES), _bf16),
        grid=(w.shape[1] // tn, t // tm),
        in_specs=[pl.BlockSpec((tm, d), lambda j, i: (i, 0)),
                  pl.BlockSpec((d, tn), lambda j, i: (0, j))],
        out_specs=pl.BlockSpec((nh, tm, LANES), lambda j, i: (j, i, 0)),
        compiler_params=_cparams(("arbitrary", "arbitrary")),
        name="proj_heads_" + act,
    )(h, w)


def _proj_gate(h, w2, lb2):
    t, d = h.shape
    tm = _row_tile(t)
    tn = 512
    nh = tn // LANES
    ncb = w2.shape[2] // tn
    out_sds = lambda dt: jax.ShapeDtypeStruct((2, w2.shape[2] // LANES, t, LANES), dt)
    return pl.pallas_call(
        _proj_gate_kernel,
        out_shape=(out_sds(_f32), out_sds(_bf16)),
        grid=(2, ncb, t // tm),
        in_specs=[pl.BlockSpec((tm, d), lambda r, j, i: (i, 0)),
                  pl.BlockSpec((1, d, tn), lambda r, j, i: (r, 0, j)),
                  pl.BlockSpec((1, 1, tn), lambda r, j, i: (r, 0, j))],
        out_specs=(pl.BlockSpec((1, nh, tm, LANES), lambda r, j, i: (r, j, i, 0)),
                   pl.BlockSpec((1, nh, tm, LANES), lambda r, j, i: (r, j, i, 0))),
        compiler_params=_cparams(("arbitrary", "arbitrary", "arbitrary")),
        name="proj_gate",
    )(h, w2, lb2)


def _proj_sigmoid(h, w):
    t, d = h.shape
    tm = _row_tile(t)
    tn = 512
    return pl.pallas_call(
        _proj_sigmoid_kernel,
        out_shape=jax.ShapeDtypeStruct((t, w.shape[1]), _bf16),
        grid=(w.shape[1] // tn, t // tm),
        in_specs=[pl.BlockSpec((tm, d), lambda j, i: (i, 0)),
                  pl.BlockSpec((d, tn), lambda j, i: (0, j))],
        out_specs=pl.BlockSpec((tm, tn), lambda j, i: (i, j)),
        compiler_params=_cparams(("arbitrary", "arbitrary")),
        name="proj_sigmoid",
    )(h, w)


def _attn_kernel(q_ref, kp_ref, kc_ref, kn_ref, vp_ref, vc_ref, vn_ref,
                 kx_ref, vx_ref, bias_ref, sink_ref, o_ref):
    hd = A_HEAD_DIM
    blk = A_BLOCK
    nt = (((1,), (1,)), ((), ()))
    n_ctx = kx_ref.shape[0]
    ones = jnp.ones((n_ctx + 3 * blk, LANES), _bf16)
    for g in range(A_KV_HEADS):
        gs = slice(g * hd, (g + 1) * hd)
        qg = jnp.concatenate(
            [q_ref[:, (g * A_GROUP + h) * hd:(g * A_GROUP + h + 1) * hd] for h in range(A_GROUP)],
            axis=0)
        k_pc = jnp.concatenate([kp_ref[:, gs], kc_ref[:, gs]], axis=0)
        s_x = lax.dot_general(qg, kx_ref[:, gs], nt, preferred_element_type=_f32)
        s_pc = lax.dot_general(qg, k_pc, nt, preferred_element_type=_f32)
        s_n = lax.dot_general(qg, kn_ref[:, gs], nt, preferred_element_type=_f32) + bias_ref[0, 1]
        s_p = s_pc[:, 0:blk] + bias_ref[0, 0]
        s_c = s_pc[:, blk:2 * blk]
        mx = jnp.maximum(jnp.maximum(s_p, s_c), s_n)
        for j in range(n_ctx // LANES):
            mx = jnp.maximum(mx, s_x[:, j * LANES:(j + 1) * LANES])
        sink = sink_ref[g]
        m = jnp.maximum(jnp.max(mx, axis=-1, keepdims=True), sink)
        p = jnp.concatenate([jnp.exp2(s_x - m), jnp.exp2(s_p - m), jnp.exp2(s_c - m), jnp.exp2(s_n - m)],
                            axis=1).astype(_bf16)
        v_all = jnp.concatenate([vx_ref[:, gs], vp_ref[:, gs], vc_ref[:, gs], vn_ref[:, gs]], axis=0)
        o = jnp.dot(p, v_all, preferred_element_type=_f32)
        den = jnp.dot(p, ones, preferred_element_type=_f32)
        o = o / (den[:, 0:hd] + jnp.exp2(sink - m))
        for h in range(A_GROUP):
            c0 = (g * A_GROUP + h) * hd
            o_ref[:, c0:c0 + hd] = o[h * A_BLOCK:(h + 1) * A_BLOCK, :].astype(o_ref.dtype)


def _attention(q, k, v, sinks, n, l):
    blk = A_BLOCK
    nblk = n // blk
    off = l // blk
    rows = A_GROUP * blk
    r = np.arange(rows)[:, None] % blk
    j = np.arange(blk)[None, :]
    keep_prev = (j - blk - r) >= -WINDOW
    keep_next = (j + blk - r) <= WINDOW
    none = np.zeros_like(keep_prev)
    variants = [(none, keep_next), (keep_prev, keep_next), (keep_prev, none)]
    bias = jnp.asarray(np.where(np.array(variants), 0.0, NEG).astype(np.float32))
    sink_col = jnp.repeat(sinks.astype(_f32).reshape(A_KV_HEADS, A_GROUP) * LOG2E, blk, axis=1)
    sink_col = sink_col.reshape(A_KV_HEADS, rows, 1)

    def bias_idx(i):
        return (jnp.where(i == 0, 0, jnp.where(i == nblk - 1, 2, 1)), 0, 0, 0)

    kv_spec = lambda f: pl.BlockSpec((blk, A_KV_WIDTH), f)
    prev = lambda i: (jnp.maximum(i - 1, 0) + off, 0)
    cur = lambda i: (i + off, 0)
    nxt = lambda i: (jnp.minimum(i + 1, nblk - 1) + off, 0)
    return pl.pallas_call(
        _attn_kernel,
        out_shape=jax.ShapeDtypeStruct((n, A_WIDTH), _bf16),
        grid=(nblk,),
        in_specs=[pl.BlockSpec((blk, A_WIDTH), cur),
                  kv_spec(prev), kv_spec(cur), kv_spec(nxt),
                  kv_spec(prev), kv_spec(cur), kv_spec(nxt),
                  pl.BlockSpec((l, A_KV_WIDTH), lambda i: (0, 0)),
                  pl.BlockSpec((l, A_KV_WIDTH), lambda i: (0, 0)),
                  pl.BlockSpec((1, 2, rows, blk), bias_idx),
                  pl.BlockSpec((A_KV_HEADS, rows, 1), lambda i: (0, 0, 0))],
        out_specs=pl.BlockSpec((blk, A_WIDTH), lambda i: (i, 0)),
        compiler_params=_cparams(("arbitrary",)),
        name="window_attention",
    )(q, k, k, k, v, v, v, k, v, bias, sink_col)


def _split3(x):
    hi = x.astype(_bf16)
    r1 = x - hi.astype(_f32)
    mid = r1.astype(_bf16)
    lo = (r1 - mid.astype(_f32)).astype(_bf16)
    return hi, mid, lo


def _hgrn_intra_scores(q, k, b, emat, reverse):
    c = HG_CHUNK
    nsb = c // SUB
    nt = (((1,), (1,)), ((), ()))
    qb = (q * jnp.exp2(b)).astype(_bf16)
    b_tot = b[0:1, :] if reverse else b[c - 1:c, :]
    kend = (k * jnp.exp2(b_tot - b)).astype(_bf16)

    zero_row = jnp.zeros((1, LANES), _f32)
    zero_blk = jnp.zeros((SUB, LANES), _f32)
    y_rows, z_rows, w_cols = [], [], []
    for ib in range(nsb):
        r0 = ib * SUB
        bi = b[r0:r0 + SUB]
        qi = q[r0:r0 + SUB]
        pieces = []
        for s in range(SUB):
            bs = b[r0 + s:r0 + s + 1]
            ks = k[r0 + s:r0 + s + 1]
            pieces.append(qi * ks * jnp.exp2(jnp.minimum(bi - bs, 0.0)))
        y_rows.append(jnp.concatenate(pieces, axis=1))
        if reverse:
            lo_r, hi_r = r0 + SUB, c
            ref_b = b[r0 + SUB:r0 + SUB + 1] if ib < nsb - 1 else zero_row
        else:
            lo_r, hi_r = 0, r0
            ref_b = b[r0 - 1:r0] if ib > 0 else zero_row
        if hi_r > lo_r:
            qp = qi * jnp.exp2(bi - ref_b)
            kp = [k[lo_r:hi_r] * jnp.exp2(ref_b - b[lo_r:hi_r])]
            if lo_r > 0:
                kp.insert(0, jnp.zeros((lo_r, LANES), _f32))
            if hi_r < c:
                kp.append(jnp.zeros((c - hi_r, LANES), _f32))
            w_cols.append(jnp.concatenate(kp, axis=0) if len(kp) > 1 else kp[0])
        else:
            qp = zero_blk
            w_cols.append(jnp.zeros((c, LANES), _f32))
        z_rows.append(jnp.concatenate([qp if jb == ib else zero_blk for jb in range(nsb)], axis=1))
    y = jnp.concatenate(y_rows, axis=0).astype(_bf16)
    z = jnp.concatenate(z_rows, axis=0).astype(_bf16)
    w = jnp.concatenate(w_cols, axis=1).astype(_bf16)
    a_diag = jnp.dot(y, emat, preferred_element_type=_f32)
    a_off = lax.dot_general(z, w, nt, preferred_element_type=_f32)
    return a_diag, a_off, qb, kend, jnp.exp2(b_tot)


def _hgrn_intra_kernel(q_ref, k_ref, lf_ref, emat_ref, a_ref, qb_ref, ke_ref, dt_ref, *, reverse):
    c = HG_CHUNK
    nch = HG_BLOCK // c
    nh = q_ref.shape[0]
    ri = lax.broadcasted_iota(jnp.int32, (c, c), 0)
    ci = lax.broadcasted_iota(jnp.int32, (c, c), 1)
    tri = jnp.where((ci >= ri) if reverse else (ci <= ri), 1.0, 0.0).astype(_bf16)
    dmask = (ci // SUB == ri // SUB) & ((ci >= ri) if reverse else (ci <= ri))
    emat = emat_ref[...]

    def body(cc, carry):
        rows = pl.ds(pl.multiple_of(cc * c, c), c)
        lf_all = jnp.concatenate([lf_ref[0, h, rows, :] for h in range(nh)], axis=1)
        hi, mid, lo = _split3(lf_all)
        b_all = (jnp.dot(tri, hi, preferred_element_type=_f32)
                 + jnp.dot(tri, mid, preferred_element_type=_f32)
                 + jnp.dot(tri, lo, preferred_element_type=_f32))
        for h in range(nh):
            a_diag, a_off, qb, kend, dtot = _hgrn_intra_scores(
                q_ref[h, rows, :].astype(_f32), k_ref[0, h, rows, :].astype(_f32),
                b_all[:, h * LANES:(h + 1) * LANES], emat, reverse)
            qb_ref[h, rows, :] = qb
            ke_ref[h, rows, :] = kend
            dt_ref[cc, h:h + 1, :] = dtot
            a = a_off + jnp.where(dmask, a_diag[:, 0:c], 0.0)
            a_ref[h, rows, :] = jnp.concatenate([a, jnp.zeros_like(a)], axis=1).astype(a_ref.dtype)
        return carry

    lax.fori_loop(0, nch, body, 0)


def _hgrn_intra(rq, kk, lf, direction):
    nh, t, _ = rq.shape
    br = HG_BLOCK
    nch = br // HG_CHUNK
    reverse = direction == 1
    spec3 = pl.BlockSpec((nh, br, LANES), lambda s: (0, s, 0))
    spec4 = pl.BlockSpec((1, nh, br, LANES), lambda s: (direction, 0, s, 0))
    sds = jax.ShapeDtypeStruct((nh, t, LANES), _bf16)
    emat = (np.arange(SUB * LANES)[:, None] // LANES == np.arange(LANES)[None, :] % SUB)
    emat = jnp.asarray(emat.astype(np.float32), dtype=_bf16)
    return pl.pallas_call(
        functools.partial(_hgrn_intra_kernel, reverse=reverse),
        out_shape=(sds, sds, sds, jax.ShapeDtypeStruct((t // HG_CHUNK, nh, LANES), _f32)),
        grid=(t // br,),
        in_specs=[spec3, spec4, spec4, pl.BlockSpec(emat.shape, lambda s: (0, 0))],
        out_specs=(spec3, spec3, spec3, pl.BlockSpec((nch, nh, LANES), lambda s: (s, 0, 0))),
        compiler_params=_cparams(("arbitrary",)),
        name="hgrn2_intra_" + ("bwd" if reverse else "fwd"),
    )(rq, kk, lf, emat)


def _hgrn_state_kernel(af_ref, qf_ref, kf_ref, df_ref, vf_ref, ab_ref, qbk_ref, kb_ref, db_ref, vb_ref,
                       outf_ref, outb_ref, st_ref):
    c = HG_CHUNK
    nch = HG_BLOCK // c
    nh = qf_ref.shape[0]
    nt = (((1,), (1,)), ((), ()))
    tn = (((0,), (0,)), ((), ()))

    @pl.when(pl.program_id(0) == 0)
    def _():
        st_ref[...] = jnp.zeros(st_ref.shape, st_ref.dtype)

    dirs = ((0, False, af_ref, qf_ref, kf_ref, df_ref, vf_ref, outf_ref),
            (1, True, ab_ref, qbk_ref, kb_ref, db_ref, vb_ref, outb_ref))

    def body(cc, carry):
        for di, reverse, a_ref, q_ref, k_ref, d_ref, v_ref, out_ref in dirs:
            chunk = (nch - 1 - cc) if reverse else cc
            rows = pl.ds(pl.multiple_of(chunk * c, c), c)
            for h in range(nh):
                st = st_ref[di, h]
                v = v_ref[h, rows, :]
                o = (jnp.dot(a_ref[h, rows, 0:c], v, preferred_element_type=_f32)
                     + lax.dot_general(q_ref[h, rows, :], st.astype(_bf16), nt, preferred_element_type=_f32))
                out_ref[h, rows, :] = o.astype(out_ref.dtype)
                upd = lax.dot_general(v, k_ref[h, rows, :], tn, preferred_element_type=_f32)
                st_ref[di, h] = st * d_ref[chunk, pl.ds(h, 1), :] + upd
        return carry

    lax.fori_loop(0, nch, body, 0)


def _hgrn_state(intra_f, intra_b, rv, l):
    a_f, qb_f, ke_f, dt_f = intra_f
    a_b, qb_b, ke_b, dt_b = intra_b
    nh, t, _ = rv.shape
    br = HG_BLOCK
    nch = br // HG_CHUNK
    nb = t // br
    nc = l // br

    def blk_b(s):
        return jnp.where(s < nc, nc - 1 - s, nb - 1 - (s - nc))

    def specs(blk):
        s3 = pl.BlockSpec((nh, br, LANES), lambda s: (0, blk(s), 0))
        sd = pl.BlockSpec((nch, nh, LANES), lambda s: (blk(s), 0, 0))
        return [s3, s3, s3, sd, s3]

    fwd = specs(lambda s: s)
    bwd = specs(blk_b)
    return pl.pallas_call(
        _hgrn_state_kernel,
        out_shape=(jax.ShapeDtypeStruct((nh, t, LANES), _bf16),
                   jax.ShapeDtypeStruct((nh, t, LANES), _bf16)),
        grid=(nb,),
        in_specs=fwd + bwd,
        out_specs=(fwd[0], bwd[0]),
        scratch_shapes=[pltpu.VMEM((2, nh, B_DV, B_DK), _f32)],
        compiler_params=_cparams(("arbitrary",)),
        name="hgrn2_state",
    )(a_f, qb_f, ke_f, dt_f, rv, a_b, qb_b, ke_b, dt_b, rv)


def _merge_kernel(ya_ref, of_ref, ob_ref, rg_ref, ga_ref, gb_ref, x_ref, wa_ref, wb_ref, wo_ref,
                  wrh_ref, wrl_ref, br_ref, hg_ref, g2_ref, mod_ref,
                  x1_ref, h2_ref, route_ref, cnt_ref, run_ref):
    i = pl.program_id(0)
    tm, d = x_ref.shape

    @pl.when(i == 0)
    def _():
        run_ref[...] = jnp.zeros(run_ref.shape, run_ref.dtype)

    parts = []
    for h in range(B_HEADS):
        o = of_ref[h].astype(_f32) + ob_ref[h].astype(_f32)
        o = o * lax.rsqrt(jnp.mean(o * o, axis=-1, keepdims=True) + EPS)
        o = o * hg_ref[:, h * LANES:(h + 1) * LANES]
        parts.append((o * rg_ref[h].astype(_f32)).astype(_bf16))
    yb = jnp.concatenate(parts, axis=1)
    z0 = jnp.dot(ya_ref[...], wa_ref[...], preferred_element_type=_f32)
    z1 = jnp.dot(yb, wb_ref[...], preferred_element_type=_f32)
    merged = ga_ref[...].astype(_f32) * z0 + gb_ref[...].astype(_f32) * z1
    y = jnp.dot(merged.astype(_bf16), wo_ref[...], preferred_element_type=_f32)
    x1 = x_ref[...] + mod_ref[0:1, 2 * d:3 * d] * y
    x1_ref[...] = x1
    h2 = _rms_mod(x1, g2_ref[...], mod_ref[0:1, 3 * d:4 * d], mod_ref[0:1, 4 * d:5 * d])
    h2_ref[...] = h2

    hh = h2.astype(_bf16)
    hl = (h2 - hh.astype(_f32)).astype(_bf16)
    logits = (jnp.dot(hh, wrh_ref[...], preferred_element_type=_f32)
              + jnp.dot(hl, wrh_ref[...], preferred_element_type=_f32)
              + jnp.dot(hh, wrl_ref[...], preferred_element_type=_f32)) + br_ref[...]
    lane = lax.broadcasted_iota(jnp.int32, (tm, LANES), 1)
    work = jnp.where(lane < N_EXPERTS, logits, NEG)
    vals, idxs = [], []
    onehot = jnp.zeros((tm, LANES), _f32)
    for _ in range(TOP_K):
        m = jnp.max(work, axis=-1, keepdims=True)
        idx = jnp.min(jnp.where(work == m, lane, LANES), axis=-1, keepdims=True)
        sel = lane == idx
        vals.append(m)
        idxs.append(idx)
        onehot = jnp.where(sel, 1.0, onehot)
        work = jnp.where(sel, NEG, work)
    es = [jnp.exp(vv - vals[0]) for vv in vals]
    tot = es[0] + es[1] + es[2] + es[3]
    ri = lax.broadcasted_iota(jnp.int32, (tm, tm), 0)
    ci = lax.broadcasted_iota(jnp.int32, (tm, tm), 1)
    ltri = jnp.where(ci < ri, 1.0, 0.0).astype(_bf16)
    prefix = jnp.dot(ltri, onehot.astype(_bf16), preferred_element_type=_f32) + run_ref[...]
    route = jnp.zeros((tm, LANES), _f32)
    for j in range(TOP_K):
        rank = jnp.sum(jnp.where(lane == idxs[j], prefix, 0.0), axis=-1, keepdims=True)
        route = jnp.where(lane == j, idxs[j].astype(_f32), route)
        route = jnp.where(lane == TOP_K + j, es[j] / tot, route)
        route = jnp.where(lane == 2 * TOP_K + j, rank, route)
    route_ref[...] = route
    run_new = run_ref[...] + jnp.sum(onehot, axis=0, keepdims=True)
    run_ref[...] = run_new
    cnt_ref[...] = run_new


def _merge_route(ya, o_f, o_b, rg, gates, x2, wa, wb, wo, wr_hi, wr_lo, b_router, hg, g2, mod, l):
    n, d = x2.shape
    tm = 256
    off = l // tm
    const = lambda shape: pl.BlockSpec(shape, lambda i: (0,) * len(shape),
                                       pipeline_mode=pl.Buffered(1))
    head_spec = pl.BlockSpec((B_HEADS, tm, LANES), lambda i: (0, i + off, 0))
    return pl.pallas_call(
        _merge_kernel,
        out_shape=(jax.ShapeDtypeStruct((n, d), _f32),
                   jax.ShapeDtypeStruct((n, d), _f32),
                   jax.ShapeDtypeStruct((n, LANES), _f32),
                   jax.ShapeDtypeStruct((1, LANES), _f32)),
        grid=(n // tm,),
        in_specs=[pl.BlockSpec((tm, A_WIDTH), lambda i: (i, 0)),
                  head_spec, head_spec, head_spec,
                  pl.BlockSpec((tm, d), lambda i: (i + off, 0)),
                  pl.BlockSpec((tm, d), lambda i: (i + off, 1)),
                  pl.BlockSpec((tm, d), lambda i: (i, 0)),
                  const(wa.shape), const(wb.shape), const(wo.shape),
                  const(wr_hi.shape), const(wr_lo.shape), const((1, LANES)),
                  const((1, B_WIDTH)), const((1, d)), const(mod.shape)],
        out_specs=(pl.BlockSpec((tm, d), lambda i: (i, 0)),
                   pl.BlockSpec((tm, d), lambda i: (i, 0)),
                   pl.BlockSpec((tm, LANES), lambda i: (i, 0)),
                   pl.BlockSpec((1, LANES), lambda i: (0, 0))),
        scratch_shapes=[pltpu.VMEM((1, LANES), _f32)],
        compiler_params=_cparams(("arbitrary",)),
        name="merge_route",
    )(ya, o_f, o_b, rg, gates, gates, x2, wa, wb, wo, wr_hi, wr_lo, b_router, hg, g2, mod)


def _zero_fill_tiles(zero_ref, dst_ref, first, last, sem, cols=(None,)):
    tm = MOE_TM

    def copy(t, col):
        rows = pl.ds(pl.multiple_of(t * tm, tm), tm)
        dst = dst_ref.at[rows] if col is None else dst_ref.at[rows, col]
        return pltpu.make_async_copy(zero_ref, dst, sem)

    def start(t, carry):
        for col in cols:
            copy(t, col).start()
        return carry

    def wait(t, carry):
        for col in cols:
            copy(t, col).wait()
        return carry

    lax.fori_loop(first, last, start, 0)
    lax.fori_loop(first, last, wait, 0)


def _dispatch_kernel(lt_ref, nt_ref, dest_ref, h2_ref, xs_ref, zero_ref, sem, zsem):
    tm = h2_ref.shape[0]

    @pl.when(pl.program_id(0) == 0)
    def _():
        zero_ref[...] = jnp.zeros(zero_ref.shape, zero_ref.dtype)
        for e in range(N_EXPERTS):
            @pl.when(nt_ref[e] > 0)
            def _():
                pltpu.make_async_copy(zero_ref, xs_ref.at[pl.ds(lt_ref[e] * MOE_TM, MOE_TM)], zsem).start()
        for e in range(N_EXPERTS):
            @pl.when(nt_ref[e] > 0)
            def _():
                pltpu.make_async_copy(zero_ref, xs_ref.at[pl.ds(lt_ref[e] * MOE_TM, MOE_TM)], zsem).wait()
        _zero_fill_tiles(zero_ref, xs_ref, lt_ref[N_EXPERTS - 1] + 1, xs_ref.shape[0] // MOE_TM, zsem)

    def issue(t, carry):
        for j in range(TOP_K):
            dst = dest_ref[t * TOP_K + j]
            pltpu.make_async_copy(h2_ref.at[pl.ds(t, 1)], xs_ref.at[pl.ds(dst, 1)], sem).start()
        return carry

    lax.fori_loop(0, tm, issue, 0)
    for _ in range(TOP_K):
        pltpu.make_async_copy(h2_ref, xs_ref.at[pl.ds(0, tm)], sem).wait()


def _dispatch(h2, dest_flat, last_tile, ntile, n_slots):
    n, d = h2.shape
    tm = 256
    return pl.pallas_call(
        _dispatch_kernel,
        out_shape=jax.ShapeDtypeStruct((n_slots, d), h2.dtype),
        grid_spec=pltpu.PrefetchScalarGridSpec(
            num_scalar_prefetch=2,
            grid=(n // tm,),
            in_specs=[pl.BlockSpec((tm * TOP_K,), lambda i, lt, nt: (i,), memory_space=pltpu.SMEM),
                      pl.BlockSpec((tm, d), lambda i, lt, nt: (i, 0))],
            out_specs=pl.BlockSpec(memory_space=pl.ANY),
            scratch_shapes=[pltpu.VMEM((MOE_TM, d), h2.dtype), pltpu.SemaphoreType.DMA,
                            pltpu.SemaphoreType.DMA]),
        compiler_params=_cparams(("arbitrary",)),
        name="moe_dispatch",
    )(last_tile, ntile, dest_flat, h2)


def _cast_rows(src_ref, dst_ref):
    rows = dst_ref.shape[0]
    step = 256

    def body(i, carry):
        r = pl.ds(pl.multiple_of(i * step, step), step)
        dst_ref[r, :] = src_ref[0, r, :].astype(dst_ref.dtype)
        return carry

    lax.fori_loop(0, rows // step, body, 0)


def _expert_tile_loop(n_t, in_copy, out_copy, compute):
    @pl.when(n_t > 0)
    def _():
        in_copy(0, 0).start(priority=1)

    def body(t, carry):
        slot = t % 2
        in_copy(t, slot).wait()

        @pl.when(t + 1 < n_t)
        def _():
            in_copy(t + 1, 1 - slot).start(priority=1)

        @pl.when(t >= 2)
        def _():
            out_copy(t - 2, slot).wait()

        compute(slot)
        out_copy(t, slot).start(priority=1)
        return carry

    lax.fori_loop(0, n_t, body, 0)

    @pl.when(n_t >= 2)
    def _():
        out_copy(n_t - 2, n_t % 2).wait()

    @pl.when(n_t >= 1)
    def _():
        out_copy(n_t - 1, (n_t - 1) % 2).wait()


def _moe_up_kernel(ts_ref, nt_ref, xs_ref, wg_ref, wu_ref, bg_ref, bu_ref, act_ref,
                   wgb_ref, wub_ref, xbuf, obuf, sem_in, sem_out):
    e = pl.program_id(0)
    c = pl.program_id(1)
    tm = MOE_TM
    fc = wgb_ref.shape[1]
    n_t = nt_ref[e]
    t0 = ts_ref[e]
    _cast_rows(wg_ref, wgb_ref)
    _cast_rows(wu_ref, wub_ref)
    col = pl.ds(pl.multiple_of(c * fc, fc), fc)

    def rows(t):
        return pl.ds(pl.multiple_of((t0 + t) * tm, tm), tm)

    def in_copy(t, slot):
        return pltpu.make_async_copy(xs_ref.at[rows(t)], xbuf.at[slot], sem_in.at[slot])

    def out_copy(t, slot):
        return pltpu.make_async_copy(obuf.at[slot], act_ref.at[rows(t), col], sem_out.at[slot])

    def compute(slot):
        x = xbuf[slot].astype(_bf16)
        g = jnp.dot(x, wgb_ref[...], preferred_element_type=_f32) + bg_ref[0]
        u = jnp.dot(x, wub_ref[...], preferred_element_type=_f32) + bu_ref[0]
        g = jnp.minimum(g, SWIGLU_LIMIT)
        u = jnp.clip(u, -SWIGLU_LIMIT, SWIGLU_LIMIT)
        obuf[slot] = (g * _sigmoid(SWIGLU_ALPHA * g) * (u + 1.0)).astype(obuf.dtype)

    _expert_tile_loop(n_t, in_copy, out_copy, compute)

    @pl.when((e == pl.num_programs(0) - 1) & (c == pl.num_programs(1) - 1))
    def _():
        obuf[0] = jnp.zeros(obuf.shape[1:], obuf.dtype)
        n_col = act_ref.shape[1] // fc
        _zero_fill_tiles(obuf.at[0], act_ref, t0 + n_t, act_ref.shape[0] // tm, sem_out.at[0],
                         cols=tuple(pl.ds(j * fc, fc) for j in range(n_col)))


def _moe_up(xs, wg, wu, bg, bu, tstart, ntile):
    n_slots, d = xs.shape
    ne, _, ff = wg.shape
    fc = ff // 2
    tm = MOE_TM
    w_spec = pl.BlockSpec((1, d, fc), lambda e, c, ts, nt: (e, 0, c))
    b_spec = pl.BlockSpec((1, 1, fc), lambda e, c, ts, nt: (e, 0, c))
    return pl.pallas_call(
        _moe_up_kernel,
        out_shape=jax.ShapeDtypeStruct((n_slots, ff), _bf16),
        grid_spec=pltpu.PrefetchScalarGridSpec(
            num_scalar_prefetch=2,
            grid=(ne, ff // fc),
            in_specs=[pl.BlockSpec(memory_space=pl.ANY), w_spec, w_spec, b_spec, b_spec],
            out_specs=pl.BlockSpec(memory_space=pl.ANY),
            scratch_shapes=[pltpu.VMEM((d, fc), _bf16), pltpu.VMEM((d, fc), _bf16),
                            pltpu.VMEM((2, tm, d), xs.dtype), pltpu.VMEM((2, tm, fc), _bf16),
                            pltpu.SemaphoreType.DMA((2,)), pltpu.SemaphoreType.DMA((2,))]),
        compiler_params=_cparams(("arbitrary", "arbitrary")),
        name="moe_gate_up",
    )(tstart, ntile, xs, wg, wu, bg, bu)


def _moe_down_kernel(ts_ref, nt_ref, act_ref, wd_ref, bd_ref, ys_ref, wdb_ref, abuf, ybuf, sem_in, sem_out):
    e = pl.program_id(0)
    c = pl.program_id(1)
    tm = MOE_TM
    dc = wdb_ref.shape[1]
    n_t = nt_ref[e]
    t0 = ts_ref[e]
    _cast_rows(wd_ref, wdb_ref)
    col = pl.ds(pl.multiple_of(c * dc, dc), dc)

    def rows(t):
        return pl.ds(pl.multiple_of((t0 + t) * tm, tm), tm)

    def in_copy(t, slot):
        return pltpu.make_async_copy(act_ref.at[rows(t)], abuf.at[slot], sem_in.at[slot])

    def out_copy(t, slot):
        return pltpu.make_async_copy(ybuf.at[slot], ys_ref.at[rows(t), col], sem_out.at[slot])

    def compute(slot):
        ybuf[slot] = jnp.dot(abuf[slot], wdb_ref[...], preferred_element_type=_f32) + bd_ref[0]

    _expert_tile_loop(n_t, in_copy, out_copy, compute)

    @pl.when((e == pl.num_programs(0) - 1) & (c == pl.num_programs(1) - 1))
    def _():
        ybuf[0] = jnp.zeros(ybuf.shape[1:], ybuf.dtype)
        n_col = ys_ref.shape[1] // dc
        _zero_fill_tiles(ybuf.at[0], ys_ref, t0 + n_t, ys_ref.shape[0] // tm, sem_out.at[0],
                         cols=tuple(pl.ds(j * dc, dc) for j in range(n_col)))


def _moe_down(act, wd, bd, tstart, ntile):
    n_slots, ff = act.shape
    ne, _, d = wd.shape
    dc = d // 2
    tm = MOE_TM
    return pl.pallas_call(
        _moe_down_kernel,
        out_shape=jax.ShapeDtypeStruct((n_slots, d), _f32),
        grid_spec=pltpu.PrefetchScalarGridSpec(
            num_scalar_prefetch=2,
            grid=(ne, d // dc),
            in_specs=[pl.BlockSpec(memory_space=pl.ANY),
                      pl.BlockSpec((1, ff, dc), lambda e, c, ts, nt: (e, 0, c)),
                      pl.BlockSpec((1, 1, dc), lambda e, c, ts, nt: (e, 0, c))],
            out_specs=pl.BlockSpec(memory_space=pl.ANY),
            scratch_shapes=[pltpu.VMEM((ff, dc), _bf16),
                            pltpu.VMEM((2, tm, ff), _bf16), pltpu.VMEM((2, tm, dc), _f32),
                            pltpu.SemaphoreType.DMA((2,)), pltpu.SemaphoreType.DMA((2,))]),
        compiler_params=_cparams(("arbitrary", "arbitrary")),
        name="moe_down",
    )(tstart, ntile, act, wd, bd)


def _combine_kernel(dest_ref, ys_ref, route_ref, x1_ref, mod_ref, gf_ref, o_ref, buf_ref, sem):
    tm, d = x1_ref.shape

    def issue(t, carry):
        for j in range(TOP_K):
            src = dest_ref[t * TOP_K + j]
            pltpu.make_async_copy(ys_ref.at[pl.ds(src, 1)], buf_ref.at[j, pl.ds(t, 1)], sem).start()
        return carry

    lax.fori_loop(0, tm, issue, 0)
    for j in range(TOP_K):
        pltpu.make_async_copy(ys_ref.at[pl.ds(0, tm)], buf_ref.at[j], sem).wait()
    acc = jnp.zeros((tm, d), _f32)
    for j in range(TOP_K):
        acc = acc + route_ref[:, TOP_K + j:TOP_K + j + 1] * buf_ref[j]
    x2 = x1_ref[...] + mod_ref[0:1, 5 * d:6 * d] * acc
    y = x2 * lax.rsqrt(jnp.mean(x2 * x2, axis=-1, keepdims=True) + EPS)
    o_ref[...] = y * gf_ref[...]


def _combine(ys, dest_flat, route, x1, mod, gf):
    n, d = x1.shape
    tm = 256
    return pl.pallas_call(
        _combine_kernel,
        out_shape=jax.ShapeDtypeStruct((n, d), _f32),
        grid=(n // tm,),
        in_specs=[pl.BlockSpec((tm * TOP_K,), lambda i: (i,), memory_space=pltpu.SMEM),
                  pl.BlockSpec(memory_space=pl.ANY),
                  pl.BlockSpec((tm, LANES), lambda i: (i, 0)),
                  pl.BlockSpec((tm, d), lambda i: (i, 0)),
                  pl.BlockSpec(mod.shape, lambda i: (0, 0)),
                  pl.BlockSpec((1, d), lambda i: (0, 0))],
        out_specs=pl.BlockSpec((tm, d), lambda i: (i, 0)),
        scratch_shapes=[pltpu.VMEM((TOP_K, tm, d), _f32), pltpu.SemaphoreType.DMA],
        compiler_params=_cparams(("arbitrary",)),
        name="moe_combine",
    )(dest_flat, ys, route, x1, mod, gf.reshape(1, d))


def _rope_tables(n, l):
    rows = n // GRID_W
    inv_freq = ROPE_BASE ** (-jnp.arange(ROPE_PAIRS_PER_AXIS, dtype=_f32) / ROPE_PAIRS_PER_AXIS)
    ang_r = jnp.arange(rows, dtype=_f32)[:, None] * inv_freq
    ang_c = jnp.arange(GRID_W, dtype=_f32)[:, None] * inv_freq
    rep = lambda t_r, t_c: jnp.concatenate(
        [jnp.repeat(t_r, GRID_W, axis=0), jnp.tile(t_c, (rows, 1))], axis=-1)
    cos, sin = rep(jnp.cos(ang_r), jnp.cos(ang_c)), rep(jnp.sin(ang_r), jnp.sin(ang_c))
    cos_t = jnp.tile(cos, (1, LANES // 32))
    sin_t = jnp.tile(jnp.concatenate([-sin, sin], axis=-1), (1, LANES // A_HEAD_DIM))
    cos_t = jnp.concatenate([jnp.ones((l, LANES), _f32), cos_t], axis=0)
    sin_t = jnp.concatenate([jnp.zeros((l, LANES), _f32), sin_t], axis=0)
    return cos_t, sin_t


def _routing_tables(route, counts):
    tm = MOE_TM
    idx = route[:, 0:TOP_K].astype(jnp.int32)
    rank = route[:, 2 * TOP_K:3 * TOP_K].astype(jnp.int32)
    cnt = counts[0, :N_EXPERTS].astype(jnp.int32)
    ntile = (cnt + tm - 1) // tm
    tstart = jnp.cumsum(ntile) - ntile
    slot0 = tstart * tm
    onehot = idx[:, :, None] == jnp.arange(N_EXPERTS, dtype=jnp.int32)
    dest = jnp.sum(jnp.where(onehot, slot0, 0), axis=-1) + rank
    return dest.reshape(-1), tstart, ntile


def kernel(x, c, ctx, c_ctx, norm_mix_g, norm_ffn_g, w_mod, b_mod, w_in, attn_sinks,
           lb_fwd_logits, lb_bwd_logits, hgrn_norm_g, w_branch, w_out, w_router, b_router,
           w_e_gate, b_e_gate, w_e_up, b_e_up, w_e_down, b_e_down, final_norm_g):
    bsz, n, d = x.shape
    l = ctx.shape[1]
    assert bsz == 1 and d == D_MODEL and norm_mix_g.shape[0] == 1
    assert n % HG_BLOCK == 0 and l % HG_BLOCK == 0 and n // A_BLOCK >= 2
    x2 = x.reshape(n, d)
    ctx2 = ctx.reshape(l, d)

    mod = _mod_vectors(c, c_ctx, w_mod[0], b_mod[0])
    h_all = _norm_all(x2, ctx2, norm_mix_g[0], mod)

    w = w_in[0].astype(_bf16)
    c0 = 0
    segs = []
    for width in (A_WIDTH + 2 * A_KV_WIDTH, B_WIDTH, B_WIDTH, B_WIDTH, B_WIDTH, B_WIDTH, 2 * d):
        segs.append(w[:, c0:c0 + width])
        c0 += width
    w_qkv, w_rq, w_zf, w_zb, w_rv, w_rg, w_gates = segs
    cos_t, sin_t = _rope_tables(n, l)
    q, k, v = _proj_qkv(h_all, w_qkv, cos_t, sin_t)
    rq = _proj_heads(h_all, w_rq, "silu")
    rv = _proj_heads(h_all, w_rv, "none")
    rg = _proj_heads(h_all, w_rg, "silu")
    lb_f = jax.nn.softmax(lb_fwd_logits.astype(_f32), axis=0)[0]
    lb_b = jax.nn.softmax(lb_bwd_logits.astype(_f32), axis=0)[0]
    lb2 = jnp.stack([lb_f, lb_b]).reshape(2, 1, B_WIDTH)
    lf, kk = _proj_gate(h_all, jnp.stack([w_zf, w_zb]), lb2)
    gates = _proj_sigmoid(h_all, w_gates)

    ya = _attention(q, k, v, attn_sinks[0], n, l)
    o_f, o_b = _hgrn_state(_hgrn_intra(rq, kk, lf, 0), _hgrn_intra(rq, kk, lf, 1), rv, l)

    wr = jnp.zeros((d, LANES), _f32).at[:, :N_EXPERTS].set(w_router[0].astype(_f32))
    wr_hi = wr.astype(_bf16)
    wr_lo = (wr - wr_hi.astype(_f32)).astype(_bf16)
    br = jnp.zeros((1, LANES), _f32).at[0, :N_EXPERTS].set(b_router[0].astype(_f32))
    x1, h2, route, counts = _merge_route(
        ya, o_f, o_b, rg, gates, x2,
        w_branch[0, 0].astype(_bf16), w_branch[0, 1].astype(_bf16), w_out[0].astype(_bf16),
        wr_hi, wr_lo, br, hgrn_norm_g[0].reshape(1, B_WIDTH).astype(_f32),
        norm_ffn_g[0].reshape(1, d), mod, l)

    n_tiles = (n * TOP_K) // MOE_TM + N_EXPERTS
    dest, tstart, ntile = _routing_tables(route, counts)
    xs = _dispatch(h2, dest, tstart + ntile - 1, ntile, n_tiles * MOE_TM)
    act = _moe_up(xs, w_e_gate[0], w_e_up[0],
                  b_e_gate[0].reshape(N_EXPERTS, 1, EXPERT_FF), b_e_up[0].reshape(N_EXPERTS, 1, EXPERT_FF),
                  tstart, ntile)
    ys = _moe_down(act, w_e_down[0], b_e_down[0].reshape(N_EXPERTS, 1, d), tstart, ntile)
    out = _combine(ys, dest, route, x1, mod, final_norm_g)
    return out.reshape(bsz, n, d)
```

```python
import functools

import jax
import jax.numpy as jnp
import numpy as np
from jax import lax
from jax.experimental import pallas as pl
from jax.experimental.pallas import tpu as pltpu

D_MODEL = 2048
GRID_W = 64
EPS = 1e-6
A_HEADS = 16
A_KV_HEADS = 2
A_GROUP = A_HEADS // A_KV_HEADS
A_HEAD_DIM = 64
A_WIDTH = A_HEADS * A_HEAD_DIM
A_KV_WIDTH = A_KV_HEADS * A_HEAD_DIM
WINDOW = 128
A_BLOCK = 128
ROPE_BASE = 10000.0
ROPE_PAIRS_PER_AXIS = A_HEAD_DIM // 4
B_HEADS = 8
B_DK = 128
B_DV = 128
B_WIDTH = B_HEADS * B_DV
N_EXPERTS = 32
TOP_K = 4
EXPERT_FF = 2048
SWIGLU_LIMIT = 7.0
SWIGLU_ALPHA = 1.702

LANES = 128
SUB = 8
HG_CHUNK = 64
HG_BLOCK = 256
MOE_TM = 256
MOE_NBUF = 3
NEG = -1e30
LOG2E = 1.4426950408889634
VMEM_LIMIT = 56 * 1024 * 1024

_f32 = jnp.float32
_bf16 = jnp.bfloat16


def _cparams(sem):
    return pltpu.CompilerParams(dimension_semantics=sem, vmem_limit_bytes=VMEM_LIMIT)


def _sigmoid(x):
    return 1.0 / (1.0 + jnp.exp(-x))


def _pack_bf16_pair(lo, hi):
    lo_bits = lax.bitcast_convert_type(lo.astype(_bf16).astype(_f32), jnp.uint32)
    hi_bits = lax.bitcast_convert_type(hi.astype(_bf16).astype(_f32), jnp.uint32)
    return (lo_bits >> 16) | (hi_bits & jnp.uint32(0xFFFF0000))


def _unpack_bf16_pair(w):
    lo = lax.bitcast_convert_type(w << 16, _f32)
    hi = lax.bitcast_convert_type(w & jnp.uint32(0xFFFF0000), _f32)
    return lo, hi


def _mod_kernel(s_ref, w_ref, b_ref, o_ref):
    tn = w_ref.shape[1]
    for r in range(2):
        s = s_ref[r]
        s = s * _sigmoid(s)
        for j in range(tn // LANES):
            sl = slice(j * LANES, (j + 1) * LANES)
            acc = jnp.sum(w_ref[:, sl] * s, axis=0, keepdims=True)
            o_ref[r:r + 1, sl] = acc + b_ref[:, sl]


def _mod_vectors(c, c_ctx, w_mod, b_mod):
    d, n_out = w_mod.shape
    tn = 1024
    s = jnp.stack([c.reshape(d), c_ctx.reshape(d)]).astype(_f32)
    s = jnp.broadcast_to(s[:, :, None], (2, d, LANES))
    return pl.pallas_call(
        _mod_kernel,
        out_shape=jax.ShapeDtypeStruct((2, n_out), _f32),
        grid=(n_out // tn,),
        in_specs=[pl.BlockSpec((2, d, LANES), lambda j: (0, 0, 0)),
                  pl.BlockSpec((d, tn), lambda j: (0, j)),
                  pl.BlockSpec((1, tn), lambda j: (0, j))],
        out_specs=pl.BlockSpec((2, tn), lambda j: (0, j)),
        compiler_params=_cparams(("arbitrary",)),
        name="mod_vectors",
    )(s, w_mod, b_mod.reshape(1, n_out))


def _rms_mod(xf, g, shift, scale):
    y = xf * lax.rsqrt(jnp.mean(xf * xf, axis=-1, keepdims=True) + EPS)
    return (y * g) * (1.0 + scale) + shift


def _norm_kernel(x_ref, ctx_ref, g_ref, mod_ref, o_ref, *, n_ctx_tiles):
    i = pl.program_id(0)
    d = x_ref.shape[1]

    @pl.when(i < n_ctx_tiles)
    def _():
        o_ref[...] = _rms_mod(ctx_ref[...], g_ref[...], mod_ref[1:2, 0:d],
                              mod_ref[1:2, d:2 * d]).astype(o_ref.dtype)

    @pl.when(i >= n_ctx_tiles)
    def _():
        o_ref[...] = _rms_mod(x_ref[...], g_ref[...], mod_ref[0:1, 0:d],
                              mod_ref[0:1, d:2 * d]).astype(o_ref.dtype)


def _norm_all(x2, ctx2, g, mod):
    n, d = x2.shape
    l = ctx2.shape[0]
    tm = 256
    nct = l // tm
    return pl.pallas_call(
        functools.partial(_norm_kernel, n_ctx_tiles=nct),
        out_shape=jax.ShapeDtypeStruct((l + n, d), _bf16),
        grid=((l + n) // tm,),
        in_specs=[pl.BlockSpec((tm, d), lambda i: (jnp.maximum(i - nct, 0), 0)),
                  pl.BlockSpec((tm, d), lambda i: (jnp.minimum(i, nct - 1), 0)),
                  pl.BlockSpec((1, d), lambda i: (0, 0)),
                  pl.BlockSpec(mod.shape, lambda i: (0, 0))],
        out_specs=pl.BlockSpec((tm, d), lambda i: (i, 0)),
        compiler_params=_cparams(("arbitrary",)),
        name="adaln_norm",
    )(x2, ctx2, g.reshape(1, d), mod)


def _rope(a, cos, sin):
    lane = lax.broadcasted_iota(jnp.int32, a.shape, 1)
    first = (lane % A_HEAD_DIM) < (A_HEAD_DIM // 2)
    rot = jnp.where(first, pltpu.roll(a, LANES - 32, 1), pltpu.roll(a, 32, 1))
    return a * cos + rot * sin


def _proj_qkv_kernel(h_ref, w_ref, cos_ref, sin_ref, q_ref, k_ref, v_ref):
    acc = jnp.dot(h_ref[...], w_ref[...], preferred_element_type=_f32)
    cos = cos_ref[...]
    sin = sin_ref[...]
    scale = A_HEAD_DIM ** -0.5 * LOG2E
    for j in range(A_WIDTH // LANES):
        sl = slice(j * LANES, (j + 1) * LANES)
        q_ref[:, sl] = (_rope(acc[:, sl], cos, sin) * scale).astype(q_ref.dtype)
    k_ref[...] = _rope(acc[:, A_WIDTH:A_WIDTH + LANES], cos, sin).astype(k_ref.dtype)
    v_ref[...] = acc[:, A_WIDTH + LANES:A_WIDTH + 2 * LANES].astype(v_ref.dtype)


def _proj_heads_kernel(h_ref, w_ref, o_ref, *, act):
    acc = jnp.dot(h_ref[...], w_ref[...], preferred_element_type=_f32)
    if act == "silu":
        acc = acc * _sigmoid(acc)
    for hh in range(o_ref.shape[0]):
        o_ref[hh] = acc[:, hh * LANES:(hh + 1) * LANES].astype(o_ref.dtype)


def _proj_gate_kernel(h_ref, w_ref, lb_ref, lf_ref, kk_ref):
    z = jnp.dot(h_ref[...], w_ref[0], preferred_element_type=_f32)
    lb = lb_ref[0]
    sg = _sigmoid(z)
    logf = jnp.log2(lb + (1.0 - lb) * sg)
    kk = (1.0 - lb) * _sigmoid(-z)
    for hh in range(lf_ref.shape[1]):
        sl = slice(hh * LANES, (hh + 1) * LANES)
        lf_ref[0, hh] = logf[:, sl]
        kk_ref[0, hh] = kk[:, sl].astype(kk_ref.dtype)


def _proj_sigmoid_kernel(h_ref, w_ref, o_ref):
    acc = jnp.dot(h_ref[...], w_ref[...], preferred_element_type=_f32)
    o_ref[...] = _sigmoid(acc).astype(o_ref.dtype)


def _row_tile(t):
    for tm in (1280, 640, 256, 128):
        if t % tm == 0:
            return tm
    raise ValueError(t)


def _proj_qkv(h, w, cos_t, sin_t):
    t, d = h.shape
    tm = _row_tile(t)
    ncol = w.shape[1]
    return pl.pallas_call(
        _proj_qkv_kernel,
        out_shape=(jax.ShapeDtypeStruct((t, A_WIDTH), _bf16),
                   jax.ShapeDtypeStruct((t, A_KV_WIDTH), _bf16),
                   jax.ShapeDtypeStruct((t, A_KV_WIDTH), _bf16)),
        grid=(t // tm,),
        in_specs=[pl.BlockSpec((tm, d), lambda i: (i, 0)),
                  pl.BlockSpec((d, ncol), lambda i: (0, 0)),
                  pl.BlockSpec((tm, LANES), lambda i: (i, 0)),
                  pl.BlockSpec((tm, LANES), lambda i: (i, 0))],
        out_specs=(pl.BlockSpec((tm, A_WIDTH), lambda i: (i, 0)),
                   pl.BlockSpec((tm, A_KV_WIDTH), lambda i: (i, 0)),
                   pl.BlockSpec((tm, A_KV_WIDTH), lambda i: (i, 0))),
        compiler_params=_cparams(("arbitrary",)),
        name="proj_qkv",
    )(h, w, cos_t, sin_t)


def _proj_heads(h, w, act):
    t, d = h.shape
    tm = _row_tile(t)
    tn = 512
    nh = tn // LANES
    return pl.pallas_call(
        functools.partial(_proj_heads_kernel, act=act),
        out_shape=jax.ShapeDtypeStruct((w.shape[1] // LANES, t, LANES), _bf16),
        grid=(w.shape[1] // tn, t // tm),
        in_specs=[pl.BlockSpec((tm, d), lambda j, i: (i, 0)),
                  pl.BlockSpec((d, tn), lambda j, i: (0, j))],
        out_specs=pl.BlockSpec((nh, tm, LANES), lambda j, i: (j, i, 0)),
        compiler_params=_cparams(("arbitrary", "arbitrary")),
        name="proj_heads_" + act,
    )(h, w)


def _proj_gate(h, w2, lb2):
    t, d = h.shape
    tm = _row_tile(t)
    tn = 512
    nh = tn // LANES
    ncb = w2.shape[2] // tn
    out_sds = lambda dt: jax.ShapeDtypeStruct((2, w2.shape[2] // LANES, t, LANES), dt)
    return pl.pallas_call(
        _proj_gate_kernel,
        out_shape=(out_sds(_f32), out_sds(_bf16)),
        grid=(2, ncb, t // tm),
        in_specs=[pl.BlockSpec((tm, d), lambda r, j, i: (i, 0)),
                  pl.BlockSpec((1, d, tn), lambda r, j, i: (r, 0, j)),
                  pl.BlockSpec((1, 1, tn), lambda r, j, i: (r, 0, j))],
        out_specs=(pl.BlockSpec((1, nh, tm, LANES), lambda r, j, i: (r, j, i, 0)),
                   pl.BlockSpec((1, nh, tm, LANES), lambda r, j, i: (r, j, i, 0))),
        compiler_params=_cparams(("arbitrary", "arbitrary", "arbitrary")),
        name="proj_gate",
    )(h, w2, lb2)


def _proj_sigmoid(h, w):
    t, d = h.shape
    tm = _row_tile(t)
    tn = 512
    return pl.pallas_call(
        _proj_sigmoid_kernel,
        out_shape=jax.ShapeDtypeStruct((t, w.shape[1]), _bf16),
        grid=(w.shape[1] // tn, t // tm),
        in_specs=[pl.BlockSpec((tm, d), lambda j, i: (i, 0)),
                  pl.BlockSpec((d, tn), lambda j, i: (0, j))],
        out_specs=pl.BlockSpec((tm, tn), lambda j, i: (i, j)),
        compiler_params=_cparams(("arbitrary", "arbitrary")),
        name="proj_sigmoid",
    )(h, w)


def _attn_kernel(q_ref, kp_ref, kc_ref, kn_ref, vp_ref, vc_ref, vn_ref,
                 kx_ref, vx_ref, bias_ref, sink_ref, o_ref):
    hd = A_HEAD_DIM
    blk = A_BLOCK
    nt = (((1,), (1,)), ((), ()))
    n_ctx = kx_ref.shape[0]
    ones = jnp.ones((n_ctx + 3 * blk, LANES), _bf16)
    for g in range(A_KV_HEADS):
        gs = slice(g * hd, (g + 1) * hd)
        qg = jnp.concatenate(
            [q_ref[:, (g * A_GROUP + h) * hd:(g * A_GROUP + h + 1) * hd] for h in range(A_GROUP)],
            axis=0)
        k_pc = jnp.concatenate([kp_ref[:, gs], kc_ref[:, gs]], axis=0)
        s_x = lax.dot_general(qg, kx_ref[:, gs], nt, preferred_element_type=_f32)
        s_pc = lax.dot_general(qg, k_pc, nt, preferred_element_type=_f32)
        s_n = lax.dot_general(qg, kn_ref[:, gs], nt, preferred_element_type=_f32) + bias_ref[0, 1]
        s_p = s_pc[:, 0:blk] + bias_ref[0, 0]
        s_c = s_pc[:, blk:2 * blk]
        mx = jnp.maximum(jnp.maximum(s_p, s_c), s_n)
        for j in range(n_ctx // LANES):
            mx = jnp.maximum(mx, s_x[:, j * LANES:(j + 1) * LANES])
        sink = sink_ref[g]
        m = jnp.maximum(jnp.max(mx, axis=-1, keepdims=True), sink)
        p = jnp.concatenate([jnp.exp2(s_x - m), jnp.exp2(s_p - m), jnp.exp2(s_c - m), jnp.exp2(s_n - m)],
                            axis=1).astype(_bf16)
        v_all = jnp.concatenate([vx_ref[:, gs], vp_ref[:, gs], vc_ref[:, gs], vn_ref[:, gs]], axis=0)
        o = jnp.dot(p, v_all, preferred_element_type=_f32)
        den = jnp.dot(p, ones, preferred_element_type=_f32)
        o = o / (den[:, 0:hd] + jnp.exp2(sink - m))
        for h in range(A_GROUP):
            c0 = (g * A_GROUP + h) * hd
            o_ref[:, c0:c0 + hd] = o[h * A_BLOCK:(h + 1) * A_BLOCK, :].astype(o_ref.dtype)


def _attention(q, k, v, sinks, n, l):
    blk = A_BLOCK
    nblk = n // blk
    off = l // blk
    rows = A_GROUP * blk
    r = np.arange(rows)[:, None] % blk
    j = np.arange(blk)[None, :]
    keep_prev = (j - blk - r) >= -WINDOW
    keep_next = (j + blk - r) <= WINDOW
    none = np.zeros_like(keep_prev)
    variants = [(none, keep_next), (keep_prev, keep_next), (keep_prev, none)]
    bias = jnp.asarray(np.where(np.array(variants), 0.0, NEG).astype(np.float32))
    sink_col = jnp.repeat(sinks.astype(_f32).reshape(A_KV_HEADS, A_GROUP) * LOG2E, blk, axis=1)
    sink_col = sink_col.reshape(A_KV_HEADS, rows, 1)

    def bias_idx(i):
        return (jnp.where(i == 0, 0, jnp.where(i == nblk - 1, 2, 1)), 0, 0, 0)

    kv_spec = lambda f: pl.BlockSpec((blk, A_KV_WIDTH), f)
    prev = lambda i: (jnp.maximum(i - 1, 0) + off, 0)
    cur = lambda i: (i + off, 0)
    nxt = lambda i: (jnp.minimum(i + 1, nblk - 1) + off, 0)
    return pl.pallas_call(
        _attn_kernel,
        out_shape=jax.ShapeDtypeStruct((n, A_WIDTH), _bf16),
        grid=(nblk,),
        in_specs=[pl.BlockSpec((blk, A_WIDTH), cur),
                  kv_spec(prev), kv_spec(cur), kv_spec(nxt),
                  kv_spec(prev), kv_spec(cur), kv_spec(nxt),
                  pl.BlockSpec((l, A_KV_WIDTH), lambda i: (0, 0)),
                  pl.BlockSpec((l, A_KV_WIDTH), lambda i: (0, 0)),
                  pl.BlockSpec((1, 2, rows, blk), bias_idx),
                  pl.BlockSpec((A_KV_HEADS, rows, 1), lambda i: (0, 0, 0))],
        out_specs=pl.BlockSpec((blk, A_WIDTH), lambda i: (i, 0)),
        compiler_params=_cparams(("arbitrary",)),
        name="window_attention",
    )(q, k, k, k, v, v, v, k, v, bias, sink_col)


def _split3(x):
    hi = x.astype(_bf16)
    r1 = x - hi.astype(_f32)
    mid = r1.astype(_bf16)
    lo = (r1 - mid.astype(_f32)).astype(_bf16)
    return hi, mid, lo


def _hgrn_intra_scores(q, k, b, emat, reverse):
    c = HG_CHUNK
    nsb = c // SUB
    nt = (((1,), (1,)), ((), ()))
    qb = (q * jnp.exp2(b)).astype(_bf16)
    b_tot = b[0:1, :] if reverse else b[c - 1:c, :]
    kend = (k * jnp.exp2(b_tot - b)).astype(_bf16)

    zero_row = jnp.zeros((1, LANES), _f32)
    zero_blk = jnp.zeros((SUB, LANES), _f32)
    y_rows, z_rows, w_cols = [], [], []
    for ib in range(nsb):
        r0 = ib * SUB
        bi = b[r0:r0 + SUB]
        qi = q[r0:r0 + SUB]
        pieces = []
        for s in range(SUB):
            bs = b[r0 + s:r0 + s + 1]
            ks = k[r0 + s:r0 + s + 1]
            pieces.append(qi * ks * jnp.exp2(jnp.minimum(bi - bs, 0.0)))
        y_rows.append(jnp.concatenate(pieces, axis=1))
        if reverse:
            lo_r, hi_r = r0 + SUB, c
            ref_b = b[r0 + SUB:r0 + SUB + 1] if ib < nsb - 1 else zero_row
        else:
            lo_r, hi_r = 0, r0
            ref_b = b[r0 - 1:r0] if ib > 0 else zero_row
        if hi_r > lo_r:
            qp = qi * jnp.exp2(bi - ref_b)
            kp = [k[lo_r:hi_r] * jnp.exp2(ref_b - b[lo_r:hi_r])]
            if lo_r > 0:
                kp.insert(0, jnp.zeros((lo_r, LANES), _f32))
            if hi_r < c:
                kp.append(jnp.zeros((c - hi_r, LANES), _f32))
            w_cols.append(jnp.concatenate(kp, axis=0) if len(kp) > 1 else kp[0])
        else:
            qp = zero_blk
            w_cols.append(jnp.zeros((c, LANES), _f32))
        z_rows.append(jnp.concatenate([qp if jb == ib else zero_blk for jb in range(nsb)], axis=1))
    y = jnp.concatenate(y_rows, axis=0).astype(_bf16)
    z = jnp.concatenate(z_rows, axis=0).astype(_bf16)
    w = jnp.concatenate(w_cols, axis=1).astype(_bf16)
    a_diag = jnp.dot(y, emat, preferred_element_type=_f32)
    a_off = lax.dot_general(z, w, nt, preferred_element_type=_f32)
    return a_diag, a_off, qb, kend, jnp.exp2(b_tot)


def _hgrn_intra_kernel(q_ref, k_ref, lf_ref, emat_ref, a_ref, qb_ref, ke_ref, dt_ref, *, reverse):
    c = HG_CHUNK
    nch = HG_BLOCK // c
    nh = q_ref.shape[0]
    ri = lax.broadcasted_iota(jnp.int32, (c, c), 0)
    ci = lax.broadcasted_iota(jnp.int32, (c, c), 1)
    tri = jnp.where((ci >= ri) if reverse else (ci <= ri), 1.0, 0.0).astype(_bf16)
    dmask = (ci // SUB == ri // SUB) & ((ci >= ri) if reverse else (ci <= ri))
    emat = emat_ref[...]

    def body(cc, carry):
        rows = pl.ds(pl.multiple_of(cc * c, c), c)
        lf_all = jnp.concatenate([lf_ref[0, h, rows, :] for h in range(nh)], axis=1)
        hi, mid, lo = _split3(lf_all)
        b_all = (jnp.dot(tri, hi, preferred_element_type=_f32)
                 + jnp.dot(tri, mid, preferred_element_type=_f32)
                 + jnp.dot(tri, lo, preferred_element_type=_f32))
        for h in range(nh):
            a_diag, a_off, qb, kend, dtot = _hgrn_intra_scores(
                q_ref[h, rows, :].astype(_f32), k_ref[0, h, rows, :].astype(_f32),
                b_all[:, h * LANES:(h + 1) * LANES], emat, reverse)
            qb_ref[h, rows, :] = qb
            ke_ref[h, rows, :] = kend
            dt_ref[cc, h:h + 1, :] = dtot
            a = a_off + jnp.where(dmask, a_diag[:, 0:c], 0.0)
            a_ref[h, rows, :] = jnp.concatenate([a, jnp.zeros_like(a)], axis=1).astype(a_ref.dtype)
        return carry

    lax.fori_loop(0, nch, body, 0)


def _hgrn_intra(rq, kk, lf, direction):
    nh, t, _ = rq.shape
    br = HG_BLOCK
    nch = br // HG_CHUNK
    reverse = direction == 1
    spec3 = pl.BlockSpec((nh, br, LANES), lambda s: (0, s, 0))
    spec4 = pl.BlockSpec((1, nh, br, LANES), lambda s: (direction, 0, s, 0))
    sds = jax.ShapeDtypeStruct((nh, t, LANES), _bf16)
    emat = (np.arange(SUB * LANES)[:, None] // LANES == np.arange(LANES)[None, :] % SUB)
    emat = jnp.asarray(emat.astype(np.float32), dtype=_bf16)
    return pl.pallas_call(
        functools.partial(_hgrn_intra_kernel, reverse=reverse),
        out_shape=(sds, sds, sds, jax.ShapeDtypeStruct((t // HG_CHUNK, nh, LANES), _f32)),
        grid=(t // br,),
        in_specs=[spec3, spec4, spec4, pl.BlockSpec(emat.shape, lambda s: (0, 0))],
        out_specs=(spec3, spec3, spec3, pl.BlockSpec((nch, nh, LANES), lambda s: (s, 0, 0))),
        compiler_params=_cparams(("arbitrary",)),
        name="hgrn2_intra_" + ("bwd" if reverse else "fwd"),
    )(rq, kk, lf, emat)


def _hgrn_state_kernel(af_ref, qf_ref, kf_ref, df_ref, vf_ref, ab_ref, qbk_ref, kb_ref, db_ref, vb_ref,
                       outf_ref, outb_ref, st_ref):
    c = HG_CHUNK
    nch = HG_BLOCK // c
    nh = qf_ref.shape[0]
    nt = (((1,), (1,)), ((), ()))
    tn = (((0,), (0,)), ((), ()))

    @pl.when(pl.program_id(0) == 0)
    def _():
        st_ref[...] = jnp.zeros(st_ref.shape, st_ref.dtype)

    dirs = ((0, False, af_ref, qf_ref, kf_ref, df_ref, vf_ref, outf_ref),
            (1, True, ab_ref, qbk_ref, kb_ref, db_ref, vb_ref, outb_ref))

    def body(cc, carry):
        for di, reverse, a_ref, q_ref, k_ref, d_ref, v_ref, out_ref in dirs:
            chunk = (nch - 1 - cc) if reverse else cc
            rows = pl.ds(pl.multiple_of(chunk * c, c), c)
            for h in range(nh):
                st = st_ref[di, h]
                v = v_ref[h, rows, :]
                o = (jnp.dot(a_ref[h, rows, 0:c], v, preferred_element_type=_f32)
                     + lax.dot_general(q_ref[h, rows, :], st.astype(_bf16), nt, preferred_element_type=_f32))
                out_ref[h, rows, :] = o.astype(out_ref.dtype)
                upd = lax.dot_general(v, k_ref[h, rows, :], tn, preferred_element_type=_f32)
                st_ref[di, h] = st * d_ref[chunk, pl.ds(h, 1), :] + upd
        return carry

    lax.fori_loop(0, nch, body, 0)


def _hgrn_state(intra_f, intra_b, rv, l):
    a_f, qb_f, ke_f, dt_f = intra_f
    a_b, qb_b, ke_b, dt_b = intra_b
    nh, t, _ = rv.shape
    br = HG_BLOCK
    nch = br // HG_CHUNK
    nb = t // br
    nc = l // br

    def blk_b(s):
        return jnp.where(s < nc, nc - 1 - s, nb - 1 - (s - nc))

    def specs(blk):
        s3 = pl.BlockSpec((nh, br, LANES), lambda s: (0, blk(s), 0))
        sd = pl.BlockSpec((nch, nh, LANES), lambda s: (blk(s), 0, 0))
        return [s3, s3, s3, sd, s3]

    fwd = specs(lambda s: s)
    bwd = specs(blk_b)
    return pl.pallas_call(
        _hgrn_state_kernel,
        out_shape=(jax.ShapeDtypeStruct((nh, t, LANES), _bf16),
                   jax.ShapeDtypeStruct((nh, t, LANES), _bf16)),
        grid=(nb,),
        in_specs=fwd + bwd,
        out_specs=(fwd[0], bwd[0]),
        scratch_shapes=[pltpu.VMEM((2, nh, B_DV, B_DK), _f32)],
        compiler_params=_cparams(("arbitrary",)),
        name="hgrn2_state",
    )(a_f, qb_f, ke_f, dt_f, rv, a_b, qb_b, ke_b, dt_b, rv)


def _merge_kernel(ya_ref, of_ref, ob_ref, rg_ref, ga_ref, gb_ref, x_ref, wa_ref, wb_ref, wo_ref,
                  wrh_ref, wrl_ref, br_ref, hg_ref, g2_ref, mod_ref,
                  x1_ref, h2_ref, route_ref, cnt_ref, run_ref):
    i = pl.program_id(0)
    tm, d = x_ref.shape

    @pl.when(i == 0)
    def _():
        run_ref[...] = jnp.zeros(run_ref.shape, run_ref.dtype)

    parts = []
    for h in range(B_HEADS):
        o = of_ref[h].astype(_f32) + ob_ref[h].astype(_f32)
        o = o * lax.rsqrt(jnp.mean(o * o, axis=-1, keepdims=True) + EPS)
        o = o * hg_ref[:, h * LANES:(h + 1) * LANES]
        parts.append((o * rg_ref[h].astype(_f32)).astype(_bf16))
    yb = jnp.concatenate(parts, axis=1)
    z0 = jnp.dot(ya_ref[...], wa_ref[...], preferred_element_type=_f32)
    z1 = jnp.dot(yb, wb_ref[...], preferred_element_type=_f32)
    merged = ga_ref[...].astype(_f32) * z0 + gb_ref[...].astype(_f32) * z1
    y = jnp.dot(merged.astype(_bf16), wo_ref[...], preferred_element_type=_f32)
    x1 = x_ref[...] + mod_ref[0:1, 2 * d:3 * d] * y
    x1_ref[...] = x1
    h2 = _rms_mod(x1, g2_ref[...], mod_ref[0:1, 3 * d:4 * d], mod_ref[0:1, 4 * d:5 * d])
    h2_ref[...] = _pack_bf16_pair(h2[:, 0:d // 2], h2[:, d // 2:d])

    hh = h2.astype(_bf16)
    hl = (h2 - hh.astype(_f32)).astype(_bf16)
    logits = (jnp.dot(hh, wrh_ref[...], preferred_element_type=_f32)
              + jnp.dot(hl, wrh_ref[...], preferred_element_type=_f32)
              + jnp.dot(hh, wrl_ref[...], preferred_element_type=_f32)) + br_ref[...]
    lane = lax.broadcasted_iota(jnp.int32, (tm, LANES), 1)
    work = jnp.where(lane < N_EXPERTS, logits, NEG)
    vals, idxs = [], []
    onehot = jnp.zeros((tm, LANES), _f32)
    for _ in range(TOP_K):
        m = jnp.max(work, axis=-1, keepdims=True)
        idx = jnp.min(jnp.where(work == m, lane, LANES), axis=-1, keepdims=True)
        sel = lane == idx
        vals.append(m)
        idxs.append(idx)
        onehot = jnp.where(sel, 1.0, onehot)
        work = jnp.where(sel, NEG, work)
    es = [jnp.exp(vv - vals[0]) for vv in vals]
    tot = es[0] + es[1] + es[2] + es[3]
    ri = lax.broadcasted_iota(jnp.int32, (tm, tm), 0)
    ci = lax.broadcasted_iota(jnp.int32, (tm, tm), 1)
    ltri = jnp.where(ci < ri, 1.0, 0.0).astype(_bf16)
    prefix = jnp.dot(ltri, onehot.astype(_bf16), preferred_element_type=_f32) + run_ref[...]
    route = jnp.zeros((tm, LANES), _f32)
    for j in range(TOP_K):
        rank = jnp.sum(jnp.where(lane == idxs[j], prefix, 0.0), axis=-1, keepdims=True)
        route = jnp.where(lane == j, idxs[j].astype(_f32), route)
        route = jnp.where(lane == TOP_K + j, es[j] / tot, route)
        route = jnp.where(lane == 2 * TOP_K + j, rank, route)
    route_ref[...] = route
    run_new = run_ref[...] + jnp.sum(onehot, axis=0, keepdims=True)
    run_ref[...] = run_new
    cnt_ref[...] = run_new


def _merge_route(ya, o_f, o_b, rg, gates, x2, wa, wb, wo, wr_hi, wr_lo, b_router, hg, g2, mod, l):
    n, d = x2.shape
    tm = 256
    off = l // tm
    const = lambda shape: pl.BlockSpec(shape, lambda i: (0,) * len(shape),
                                       pipeline_mode=pl.Buffered(1))
    head_spec = pl.BlockSpec((B_HEADS, tm, LANES), lambda i: (0, i + off, 0))
    return pl.pallas_call(
        _merge_kernel,
        out_shape=(jax.ShapeDtypeStruct((n, d), _f32),
                   jax.ShapeDtypeStruct((n, d // 2), jnp.uint32),
                   jax.ShapeDtypeStruct((n, LANES), _f32),
                   jax.ShapeDtypeStruct((1, LANES), _f32)),
        grid=(n // tm,),
        in_specs=[pl.BlockSpec((tm, A_WIDTH), lambda i: (i, 0)),
                  head_spec, head_spec, head_spec,
                  pl.BlockSpec((tm, d), lambda i: (i + off, 0)),
                  pl.BlockSpec((tm, d), lambda i: (i + off, 1)),
                  pl.BlockSpec((tm, d), lambda i: (i, 0)),
                  const(wa.shape), const(wb.shape), const(wo.shape),
                  const(wr_hi.shape), const(wr_lo.shape), const((1, LANES)),
                  const((1, B_WIDTH)), const((1, d)), const(mod.shape)],
        out_specs=(pl.BlockSpec((tm, d), lambda i: (i, 0)),
                   pl.BlockSpec((tm, d // 2), lambda i: (i, 0)),
                   pl.BlockSpec((tm, LANES), lambda i: (i, 0)),
                   pl.BlockSpec((1, LANES), lambda i: (0, 0))),
        scratch_shapes=[pltpu.VMEM((1, LANES), _f32)],
        compiler_params=_cparams(("arbitrary",)),
        name="merge_route",
    )(ya, o_f, o_b, rg, gates, gates, x2, wa, wb, wo, wr_hi, wr_lo, b_router, hg, g2, mod)


def _zero_fill_tiles(zero_ref, dst_ref, first, last, sem, cols=(None,)):
    tm = MOE_TM

    def copy(t, col):
        rows = pl.ds(pl.multiple_of(t * tm, tm), tm)
        dst = dst_ref.at[rows] if col is None else dst_ref.at[rows, col]
        return pltpu.make_async_copy(zero_ref, dst, sem)

    def start(t, carry):
        for col in cols:
            copy(t, col).start()
        return carry

    def wait(t, carry):
        for col in cols:
            copy(t, col).wait()
        return carry

    lax.fori_loop(first, last, start, 0)
    lax.fori_loop(first, last, wait, 0)


def _dispatch_kernel(lt_ref, nt_ref, dest_ref, h2_ref, xs_ref, zero_ref, sem, zsem):
    tm = h2_ref.shape[0]

    @pl.when(pl.program_id(0) == 0)
    def _():
        zero_ref[...] = jnp.zeros(zero_ref.shape, zero_ref.dtype)
        for e in range(N_EXPERTS):
            @pl.when(nt_ref[e] > 0)
            def _():
                pltpu.make_async_copy(zero_ref, xs_ref.at[pl.ds(lt_ref[e] * MOE_TM, MOE_TM)], zsem).start()
        for e in range(N_EXPERTS):
            @pl.when(nt_ref[e] > 0)
            def _():
                pltpu.make_async_copy(zero_ref, xs_ref.at[pl.ds(lt_ref[e] * MOE_TM, MOE_TM)], zsem).wait()
        _zero_fill_tiles(zero_ref, xs_ref, lt_ref[N_EXPERTS - 1] + 1, xs_ref.shape[0] // MOE_TM, zsem)

    def issue(t, carry):
        for j in range(TOP_K):
            dst = dest_ref[t * TOP_K + j]
            pltpu.make_async_copy(h2_ref.at[pl.ds(t, 1)], xs_ref.at[pl.ds(dst, 1)], sem).start()
        return carry

    lax.fori_loop(0, tm, issue, 0)
    for _ in range(TOP_K):
        pltpu.make_async_copy(h2_ref, xs_ref.at[pl.ds(0, tm)], sem).wait()


def _dispatch(h2, dest_flat, last_tile, ntile, n_slots):
    n, d = h2.shape
    tm = 256
    return pl.pallas_call(
        _dispatch_kernel,
        out_shape=jax.ShapeDtypeStruct((n_slots, d), h2.dtype),
        grid_spec=pltpu.PrefetchScalarGridSpec(
            num_scalar_prefetch=2,
            grid=(n // tm,),
            in_specs=[pl.BlockSpec((tm * TOP_K,), lambda i, lt, nt: (i,), memory_space=pltpu.SMEM),
                      pl.BlockSpec((tm, d), lambda i, lt, nt: (i, 0))],
            out_specs=pl.BlockSpec(memory_space=pl.ANY),
            scratch_shapes=[pltpu.VMEM((MOE_TM, d), h2.dtype), pltpu.SemaphoreType.DMA,
                            pltpu.SemaphoreType.DMA]),
        compiler_params=_cparams(("arbitrary",)),
        name="moe_dispatch",
    )(last_tile, ntile, dest_flat, h2)


def _cast_rows(src_ref, dst_ref):
    rows = dst_ref.shape[0]
    step = 256

    def body(i, carry):
        r = pl.ds(pl.multiple_of(i * step, step), step)
        dst_ref[r, :] = src_ref[0, r, :].astype(dst_ref.dtype)
        return carry

    lax.fori_loop(0, rows // step, body, 0)


def _expert_tile_loop(n_t, in_copy, out_copy, compute):
    nb = MOE_NBUF
    for k in range(nb - 1):
        @pl.when(n_t > k)
        def _():
            in_copy(k, k).start(priority=1)

    def body(t, carry):
        slot = lax.rem(t, nb)
        in_copy(t, slot).wait()

        @pl.when(t + nb - 1 < n_t)
        def _():
            in_copy(t + nb - 1, lax.rem(t + nb - 1, nb)).start(priority=1)

        @pl.when(t >= nb)
        def _():
            out_copy(t - nb, slot).wait()

        compute(slot)
        out_copy(t, slot).start(priority=1)
        return carry

    lax.fori_loop(0, n_t, body, 0)

    for k in range(1, nb + 1):
        @pl.when(n_t >= k)
        def _():
            out_copy(n_t - k, lax.rem(n_t - k, nb)).wait()


def _moe_up_kernel(ts_ref, nt_ref, xs_ref, wg_ref, wu_ref, bg_ref, bu_ref, act_ref,
                   wgb_ref, wub_ref, xbuf, obuf, sem_in, sem_out):
    e = pl.program_id(0)
    c = pl.program_id(1)
    tm = MOE_TM
    fc = wgb_ref.shape[1]
    n_t = nt_ref[e]
    t0 = ts_ref[e]
    _cast_rows(wg_ref, wgb_ref)
    _cast_rows(wu_ref, wub_ref)
    col = pl.ds(pl.multiple_of(c * fc, fc), fc)

    def rows(t):
        return pl.ds(pl.multiple_of((t0 + t) * tm, tm), tm)

    def in_copy(t, slot):
        return pltpu.make_async_copy(xs_ref.at[rows(t)], xbuf.at[slot], sem_in.at[slot])

    def out_copy(t, slot):
        return pltpu.make_async_copy(obuf.at[slot], act_ref.at[rows(t), col], sem_out.at[slot])

    def compute(slot):
        x_lo, x_hi = _unpack_bf16_pair(xbuf[slot])
        x_lo, x_hi = x_lo.astype(_bf16), x_hi.astype(_bf16)
        half = x_lo.shape[1]

        def proj(w_ref, b_ref):
            return (jnp.dot(x_lo, w_ref[0:half, :], preferred_element_type=_f32)
                    + jnp.dot(x_hi, w_ref[half:2 * half, :], preferred_element_type=_f32) + b_ref[0])

        g = proj(wgb_ref, bg_ref)
        u = proj(wub_ref, bu_ref)
        g = jnp.minimum(g, SWIGLU_LIMIT)
        u = jnp.clip(u, -SWIGLU_LIMIT, SWIGLU_LIMIT)
        obuf[slot] = (g * _sigmoid(SWIGLU_ALPHA * g) * (u + 1.0)).astype(obuf.dtype)

    _expert_tile_loop(n_t, in_copy, out_copy, compute)

    @pl.when((e == pl.num_programs(0) - 1) & (c == pl.num_programs(1) - 1))
    def _():
        obuf[0] = jnp.zeros(obuf.shape[1:], obuf.dtype)
        n_col = act_ref.shape[1] // fc
        _zero_fill_tiles(obuf.at[0], act_ref, t0 + n_t, act_ref.shape[0] // tm, sem_out.at[0],
                         cols=tuple(pl.ds(j * fc, fc) for j in range(n_col)))


def _moe_up(xs, wg, wu, bg, bu, tstart, ntile):
    n_slots = xs.shape[0]
    ne, d, ff = wg.shape
    fc = ff // 2
    tm = MOE_TM
    w_spec = pl.BlockSpec((1, d, fc), lambda e, c, ts, nt: (e, 0, c))
    b_spec = pl.BlockSpec((1, 1, fc), lambda e, c, ts, nt: (e, 0, c))
    return pl.pallas_call(
        _moe_up_kernel,
        out_shape=jax.ShapeDtypeStruct((n_slots, ff), _bf16),
        grid_spec=pltpu.PrefetchScalarGridSpec(
            num_scalar_prefetch=2,
            grid=(ne, ff // fc),
            in_specs=[pl.BlockSpec(memory_space=pl.ANY), w_spec, w_spec, b_spec, b_spec],
            out_specs=pl.BlockSpec(memory_space=pl.ANY),
            scratch_shapes=[pltpu.VMEM((d, fc), _bf16), pltpu.VMEM((d, fc), _bf16),
                            pltpu.VMEM((MOE_NBUF, tm, xs.shape[1]), xs.dtype),
                            pltpu.VMEM((MOE_NBUF, tm, fc), _bf16),
                            pltpu.SemaphoreType.DMA((MOE_NBUF,)), pltpu.SemaphoreType.DMA((MOE_NBUF,))]),
        compiler_params=_cparams(("arbitrary", "arbitrary")),
        name="moe_gate_up",
    )(tstart, ntile, xs, wg, wu, bg, bu)


def _moe_down_kernel(ts_ref, nt_ref, act_ref, wd_ref, bd_ref, ys_ref, wdb_ref, abuf, ybuf, sem_in, sem_out):
    e = pl.program_id(0)
    tm = MOE_TM
    d = wdb_ref.shape[1]
    n_t = nt_ref[e]
    t0 = ts_ref[e]
    _cast_rows(wd_ref, wdb_ref)

    def rows(t):
        return pl.ds(pl.multiple_of((t0 + t) * tm, tm), tm)

    def in_copy(t, slot):
        return pltpu.make_async_copy(act_ref.at[rows(t)], abuf.at[slot], sem_in.at[slot])

    def out_copy(t, slot):
        return pltpu.make_async_copy(ybuf.at[slot], ys_ref.at[rows(t)], sem_out.at[slot])

    def compute(slot):
        y = jnp.dot(abuf[slot], wdb_ref[...], preferred_element_type=_f32) + bd_ref[0]
        ybuf[slot] = _pack_bf16_pair(y[:, 0:d // 2], y[:, d // 2:d])

    _expert_tile_loop(n_t, in_copy, out_copy, compute)

    @pl.when(e == pl.num_programs(0) - 1)
    def _():
        ybuf[0] = jnp.zeros(ybuf.shape[1:], ybuf.dtype)
        _zero_fill_tiles(ybuf.at[0], ys_ref, t0 + n_t, ys_ref.shape[0] // tm, sem_out.at[0])


def _moe_down(act, wd, bd, tstart, ntile):
    n_slots, ff = act.shape
    ne, _, d = wd.shape
    tm = MOE_TM
    return pl.pallas_call(
        _moe_down_kernel,
        out_shape=jax.ShapeDtypeStruct((n_slots, d // 2), jnp.uint32),
        grid_spec=pltpu.PrefetchScalarGridSpec(
            num_scalar_prefetch=2,
            grid=(ne,),
            in_specs=[pl.BlockSpec(memory_space=pl.ANY),
                      pl.BlockSpec((1, ff, d), lambda e, ts, nt: (e, 0, 0)),
                      pl.BlockSpec((1, 1, d), lambda e, ts, nt: (e, 0, 0))],
            out_specs=pl.BlockSpec(memory_space=pl.ANY),
            scratch_shapes=[pltpu.VMEM((ff, d), _bf16),
                            pltpu.VMEM((MOE_NBUF, tm, ff), _bf16), pltpu.VMEM((MOE_NBUF, tm, d // 2), jnp.uint32),
                            pltpu.SemaphoreType.DMA((MOE_NBUF,)), pltpu.SemaphoreType.DMA((MOE_NBUF,))]),
        compiler_params=_cparams(("arbitrary",)),
        name="moe_down",
    )(tstart, ntile, act, wd, bd)


def _combine_kernel(dest_ref, ys_ref, route_ref, x1_ref, mod_ref, gf_ref, o_ref, buf_ref, sem):
    tm, d = x1_ref.shape

    def issue(t, carry):
        for j in range(TOP_K):
            src = dest_ref[t * TOP_K + j]
            pltpu.make_async_copy(ys_ref.at[pl.ds(src, 1)], buf_ref.at[j, pl.ds(t, 1)], sem).start()
        return carry

    lax.fori_loop(0, tm, issue, 0)
    for j in range(TOP_K):
        pltpu.make_async_copy(ys_ref.at[pl.ds(0, tm)], buf_ref.at[j], sem).wait()
    acc_lo = jnp.zeros((tm, d // 2), _f32)
    acc_hi = jnp.zeros((tm, d // 2), _f32)
    for j in range(TOP_K):
        gate = route_ref[:, TOP_K + j:TOP_K + j + 1]
        y_lo, y_hi = _unpack_bf16_pair(buf_ref[j])
        acc_lo = acc_lo + gate * y_lo
        acc_hi = acc_hi + gate * y_hi
    x2 = x1_ref[...] + mod_ref[0:1, 5 * d:6 * d] * jnp.concatenate([acc_lo, acc_hi], axis=1)
    y = x2 * lax.rsqrt(jnp.mean(x2 * x2, axis=-1, keepdims=True) + EPS)
    o_ref[...] = y * gf_ref[...]


def _combine(ys, dest_flat, route, x1, mod, gf):
    n, d = x1.shape
    tm = 256
    return pl.pallas_call(
        _combine_kernel,
        out_shape=jax.ShapeDtypeStruct((n, d), _f32),
        grid=(n // tm,),
        in_specs=[pl.BlockSpec((tm * TOP_K,), lambda i: (i,), memory_space=pltpu.SMEM),
                  pl.BlockSpec(memory_space=pl.ANY),
                  pl.BlockSpec((tm, LANES), lambda i: (i, 0)),
                  pl.BlockSpec((tm, d), lambda i: (i, 0)),
                  pl.BlockSpec(mod.shape, lambda i: (0, 0)),
                  pl.BlockSpec((1, d), lambda i: (0, 0))],
        out_specs=pl.BlockSpec((tm, d), lambda i: (i, 0)),
        scratch_shapes=[pltpu.VMEM((TOP_K, tm, d // 2), jnp.uint32), pltpu.SemaphoreType.DMA],
        compiler_params=_cparams(("arbitrary",)),
        name="moe_combine",
    )(dest_flat, ys, route, x1, mod, gf.reshape(1, d))


def _rope_tables(n, l):
    rows = n // GRID_W
    inv_freq = ROPE_BASE ** (-jnp.arange(ROPE_PAIRS_PER_AXIS, dtype=_f32) / ROPE_PAIRS_PER_AXIS)
    ang_r = jnp.arange(rows, dtype=_f32)[:, None] * inv_freq
    ang_c = jnp.arange(GRID_W, dtype=_f32)[:, None] * inv_freq
    rep = lambda t_r, t_c: jnp.concatenate(
        [jnp.repeat(t_r, GRID_W, axis=0), jnp.tile(t_c, (rows, 1))], axis=-1)
    cos, sin = rep(jnp.cos(ang_r), jnp.cos(ang_c)), rep(jnp.sin(ang_r), jnp.sin(ang_c))
    cos_t = jnp.tile(cos, (1, LANES // 32))
    sin_t = jnp.tile(jnp.concatenate([-sin, sin], axis=-1), (1, LANES // A_HEAD_DIM))
    cos_t = jnp.concatenate([jnp.ones((l, LANES), _f32), cos_t], axis=0)
    sin_t = jnp.concatenate([jnp.zeros((l, LANES), _f32), sin_t], axis=0)
    return cos_t, sin_t


def _routing_tables(route, counts):
    tm = MOE_TM
    idx = route[:, 0:TOP_K].astype(jnp.int32)
    rank = route[:, 2 * TOP_K:3 * TOP_K].astype(jnp.int32)
    cnt = counts[0, :N_EXPERTS].astype(jnp.int32)
    ntile = (cnt + tm - 1) // tm
    tstart = jnp.cumsum(ntile) - ntile
    slot0 = tstart * tm
    onehot = idx[:, :, None] == jnp.arange(N_EXPERTS, dtype=jnp.int32)
    dest = jnp.sum(jnp.where(onehot, slot0, 0), axis=-1) + rank
    return dest.reshape(-1), tstart, ntile


def kernel(x, c, ctx, c_ctx, norm_mix_g, norm_ffn_g, w_mod, b_mod, w_in, attn_sinks,
           lb_fwd_logits, lb_bwd_logits, hgrn_norm_g, w_branch, w_out, w_router, b_router,
           w_e_gate, b_e_gate, w_e_up, b_e_up, w_e_down, b_e_down, final_norm_g):
    bsz, n, d = x.shape
    l = ctx.shape[1]
    assert bsz == 1 and d == D_MODEL and norm_mix_g.shape[0] == 1
    assert n % HG_BLOCK == 0 and l % HG_BLOCK == 0 and n // A_BLOCK >= 2
    x2 = x.reshape(n, d)
    ctx2 = ctx.reshape(l, d)

    mod = _mod_vectors(c, c_ctx, w_mod[0], b_mod[0])
    h_all = _norm_all(x2, ctx2, norm_mix_g[0], mod)

    w = w_in[0].astype(_bf16)
    c0 = 0
    segs = []
    for width in (A_WIDTH + 2 * A_KV_WIDTH, B_WIDTH, B_WIDTH, B_WIDTH, B_WIDTH, B_WIDTH, 2 * d):
        segs.append(w[:, c0:c0 + width])
        c0 += width
    w_qkv, w_rq, w_zf, w_zb, w_rv, w_rg, w_gates = segs
    cos_t, sin_t = _rope_tables(n, l)
    q, k, v = _proj_qkv(h_all, w_qkv, cos_t, sin_t)
    rq = _proj_heads(h_all, w_rq, "silu")
    rv = _proj_heads(h_all, w_rv, "none")
    rg = _proj_heads(h_all, w_rg, "silu")
    lb_f = jax.nn.softmax(lb_fwd_logits.astype(_f32), axis=0)[0]
    lb_b = jax.nn.softmax(lb_bwd_logits.astype(_f32), axis=0)[0]
    lb2 = jnp.stack([lb_f, lb_b]).reshape(2, 1, B_WIDTH)
    lf, kk = _proj_gate(h_all, jnp.stack([w_zf, w_zb]), lb2)
    gates = _proj_sigmoid(h_all, w_gates)

    ya = _attention(q, k, v, attn_sinks[0], n, l)
    o_f, o_b = _hgrn_state(_hgrn_intra(rq, kk, lf, 0), _hgrn_intra(rq, kk, lf, 1), rv, l)

    wr = jnp.zeros((d, LANES), _f32).at[:, :N_EXPERTS].set(w_router[0].astype(_f32))
    wr_hi = wr.astype(_bf16)
    wr_lo = (wr - wr_hi.astype(_f32)).astype(_bf16)
    br = jnp.zeros((1, LANES), _f32).at[0, :N_EXPERTS].set(b_router[0].astype(_f32))
    x1, h2, route, counts = _merge_route(
        ya, o_f, o_b, rg, gates, x2,
        w_branch[0, 0].astype(_bf16), w_branch[0, 1].astype(_bf16), w_out[0].astype(_bf16),
        wr_hi, wr_lo, br, hgrn_norm_g[0].reshape(1, B_WIDTH).astype(_f32),
        norm_ffn_g[0].reshape(1, d), mod, l)

    n_tiles = (n * TOP_K) // MOE_TM + N_EXPERTS
    dest, tstart, ntile = _routing_tables(route, counts)
    xs = _dispatch(h2, dest, tstart + ntile - 1, ntile, n_tiles * MOE_TM)
    act = _moe_up(xs, w_e_gate[0], w_e_up[0],
                  b_e_gate[0].reshape(N_EXPERTS, 1, EXPERT_FF), b_e_up[0].reshape(N_EXPERTS, 1, EXPERT_FF),
                  tstart, ntile)
    ys = _moe_down(act, w_e_down[0], b_e_down[0].reshape(N_EXPERTS, 1, d), tstart, ntile)
    out = _combine(ys, dest, route, x1, mod, final_norm_g)
    return out.reshape(bsz, n, d)
```

```python
import functools

import jax
import jax.numpy as jnp
import numpy as np
from jax import lax
from jax.experimental import pallas as pl
from jax.experimental.pallas import tpu as pltpu

D_MODEL = 2048
GRID_W = 64
EPS = 1e-6
A_HEADS = 16
A_KV_HEADS = 2
A_GROUP = A_HEADS // A_KV_HEADS
A_HEAD_DIM = 64
A_WIDTH = A_HEADS * A_HEAD_DIM
A_KV_WIDTH = A_KV_HEADS * A_HEAD_DIM
WINDOW = 128
A_BLOCK = 128
ROPE_BASE = 10000.0
ROPE_PAIRS_PER_AXIS = A_HEAD_DIM // 4
B_HEADS = 8
B_DK = 128
B_DV = 128
B_WIDTH = B_HEADS * B_DV
N_EXPERTS = 32
TOP_K = 4
EXPERT_FF = 2048
SWIGLU_LIMIT = 7.0
SWIGLU_ALPHA = 1.702

LANES = 128
SUB = 8
HG_CHUNK = 64
HG_BLOCK = 256
MOE_TM = 256
MOE_NBUF = 3
NEG = -1e30
LOG2E = 1.4426950408889634
VMEM_LIMIT = 56 * 1024 * 1024

_f32 = jnp.float32
_bf16 = jnp.bfloat16


def _cparams(sem):
    return pltpu.CompilerParams(dimension_semantics=sem, vmem_limit_bytes=VMEM_LIMIT)


def _sigmoid(x):
    return 1.0 / (1.0 + jnp.exp(-x))


def _pack_bf16_pair(lo, hi):
    lo_bits = lax.bitcast_convert_type(lo.astype(_bf16).astype(_f32), jnp.uint32)
    hi_bits = lax.bitcast_convert_type(hi.astype(_bf16).astype(_f32), jnp.uint32)
    return (lo_bits >> 16) | (hi_bits & jnp.uint32(0xFFFF0000))


def _unpack_bf16_pair(w):
    lo = lax.bitcast_convert_type(w << 16, _f32)
    hi = lax.bitcast_convert_type(w & jnp.uint32(0xFFFF0000), _f32)
    return lo, hi


def _mod_kernel(s_ref, w_ref, b_ref, o_ref):
    tn = w_ref.shape[1]
    for r in range(2):
        s = s_ref[r]
        s = s * _sigmoid(s)
        for j in range(tn // LANES):
            sl = slice(j * LANES, (j + 1) * LANES)
            acc = jnp.sum(w_ref[:, sl] * s, axis=0, keepdims=True)
            o_ref[r:r + 1, sl] = acc + b_ref[:, sl]


def _mod_vectors(c, c_ctx, w_mod, b_mod):
    d, n_out = w_mod.shape
    tn = 1024
    s = jnp.stack([c.reshape(d), c_ctx.reshape(d)]).astype(_f32)
    s = jnp.broadcast_to(s[:, :, None], (2, d, LANES))
    return pl.pallas_call(
        _mod_kernel,
        out_shape=jax.ShapeDtypeStruct((2, n_out), _f32),
        grid=(n_out // tn,),
        in_specs=[pl.BlockSpec((2, d, LANES), lambda j: (0, 0, 0)),
                  pl.BlockSpec((d, tn), lambda j: (0, j)),
                  pl.BlockSpec((1, tn), lambda j: (0, j))],
        out_specs=pl.BlockSpec((2, tn), lambda j: (0, j)),
        compiler_params=_cparams(("arbitrary",)),
        name="mod_vectors",
    )(s, w_mod, b_mod.reshape(1, n_out))


def _rms_mod(xf, g, shift, scale):
    y = xf * lax.rsqrt(jnp.mean(xf * xf, axis=-1, keepdims=True) + EPS)
    return (y * g) * (1.0 + scale) + shift


def _norm_kernel(x_ref, ctx_ref, g_ref, mod_ref, o_ref, *, n_ctx_tiles):
    i = pl.program_id(0)
    d = x_ref.shape[1]

    @pl.when(i < n_ctx_tiles)
    def _():
        o_ref[...] = _rms_mod(ctx_ref[...], g_ref[...], mod_ref[1:2, 0:d],
                              mod_ref[1:2, d:2 * d]).astype(o_ref.dtype)

    @pl.when(i >= n_ctx_tiles)
    def _():
        o_ref[...] = _rms_mod(x_ref[...], g_ref[...], mod_ref[0:1, 0:d],
                              mod_ref[0:1, d:2 * d]).astype(o_ref.dtype)


def _norm_all(x2, ctx2, g, mod):
    n, d = x2.shape
    l = ctx2.shape[0]
    tm = 256
    nct = l // tm
    return pl.pallas_call(
        functools.partial(_norm_kernel, n_ctx_tiles=nct),
        out_shape=jax.ShapeDtypeStruct((l + n, d), _bf16),
        grid=((l + n) // tm,),
        in_specs=[pl.BlockSpec((tm, d), lambda i: (jnp.maximum(i - nct, 0), 0)),
                  pl.BlockSpec((tm, d), lambda i: (jnp.minimum(i, nct - 1), 0)),
                  pl.BlockSpec((1, d), lambda i: (0, 0)),
                  pl.BlockSpec(mod.shape, lambda i: (0, 0))],
        out_specs=pl.BlockSpec((tm, d), lambda i: (i, 0)),
        compiler_params=_cparams(("arbitrary",)),
        name="adaln_norm",
    )(x2, ctx2, g.reshape(1, d), mod)


def _rope(a, cos, sin):
    lane = lax.broadcasted_iota(jnp.int32, a.shape, 1)
    first = (lane % A_HEAD_DIM) < (A_HEAD_DIM // 2)
    rot = jnp.where(first, pltpu.roll(a, LANES - 32, 1), pltpu.roll(a, 32, 1))
    return a * cos + rot * sin


def _proj_qkv_kernel(h_ref, w_ref, cos_ref, sin_ref, q_ref, k_ref, v_ref):
    acc = jnp.dot(h_ref[...], w_ref[...], preferred_element_type=_f32)
    cos = cos_ref[...]
    sin = sin_ref[...]
    scale = A_HEAD_DIM ** -0.5 * LOG2E
    for j in range(A_WIDTH // LANES):
        sl = slice(j * LANES, (j + 1) * LANES)
        q_ref[:, sl] = (_rope(acc[:, sl], cos, sin) * scale).astype(q_ref.dtype)
    k_ref[...] = _rope(acc[:, A_WIDTH:A_WIDTH + LANES], cos, sin).astype(k_ref.dtype)
    v_ref[...] = acc[:, A_WIDTH + LANES:A_WIDTH + 2 * LANES].astype(v_ref.dtype)


def _proj_heads_kernel(h_ref, w_ref, o_ref, *, act):
    acc = jnp.dot(h_ref[...], w_ref[...], preferred_element_type=_f32)
    if act == "silu":
        acc = acc * _sigmoid(acc)
    for hh in range(o_ref.shape[0]):
        o_ref[hh] = acc[:, hh * LANES:(hh + 1) * LANES].astype(o_ref.dtype)


def _proj_gate_kernel(h_ref, w_ref, lb_ref, lf_ref, kk_ref):
    z = jnp.dot(h_ref[...], w_ref[0], preferred_element_type=_f32)
    lb = lb_ref[0]
    sg = _sigmoid(z)
    logf = jnp.log2(lb + (1.0 - lb) * sg)
    kk = (1.0 - lb) * _sigmoid(-z)
    for hh in range(lf_ref.shape[1]):
        sl = slice(hh * LANES, (hh + 1) * LANES)
        lf_ref[0, hh] = logf[:, sl]
        kk_ref[0, hh] = kk[:, sl].astype(kk_ref.dtype)


def _proj_sigmoid_kernel(h_ref, w_ref, o_ref):
    acc = jnp.dot(h_ref[...], w_ref[...], preferred_element_type=_f32)
    o_ref[...] = _sigmoid(acc).astype(o_ref.dtype)


def _row_tile(t):
    for tm in (1280, 640, 256, 128):
        if t % tm == 0:
            return tm
    raise ValueError(t)


def _proj_qkv(h, w, cos_t, sin_t):
    t, d = h.shape
    tm = _row_tile(t)
    ncol = w.shape[1]
    return pl.pallas_call(
        _proj_qkv_kernel,
        out_shape=(jax.ShapeDtypeStruct((t, A_WIDTH), _bf16),
                   jax.ShapeDtypeStruct((t, A_KV_WIDTH), _bf16),
                   jax.ShapeDtypeStruct((t, A_KV_WIDTH), _bf16)),
        grid=(t // tm,),
        in_specs=[pl.BlockSpec((tm, d), lambda i: (i, 0)),
                  pl.BlockSpec((d, ncol), lambda i: (0, 0)),
                  pl.BlockSpec((tm, LANES), lambda i: (i, 0)),
                  pl.BlockSpec((tm, LANES), lambda i: (i, 0))],
        out_specs=(pl.BlockSpec((tm, A_WIDTH), lambda i: (i, 0)),
                   pl.BlockSpec((tm, A_KV_WIDTH), lambda i: (i, 0)),
                   pl.BlockSpec((tm, A_KV_WIDTH), lambda i: (i, 0))),
        compiler_params=_cparams(("arbitrary",)),
        name="proj_qkv",
    )(h, w, cos_t, sin_t)


def _proj_heads(h, w, act):
    t, d = h.shape
    tm = _row_tile(t)
    tn = 512
    nh = tn // LANES
    return pl.pallas_call(
        functools.partial(_proj_heads_kernel, act=act),
        out_shape=jax.ShapeDtypeStruct((w.shape[1] // LANES, t, LANES), _bf16),
        grid=(w.shape[1] // tn, t // tm),
        in_specs=[pl.BlockSpec((tm, d), lambda j, i: (i, 0)),
                  pl.BlockSpec((d, tn), lambda j, i: (0, j))],
        out_specs=pl.BlockSpec((nh, tm, LANES), lambda j, i: (j, i, 0)),
        compiler_params=_cparams(("arbitrary", "arbitrary")),
        name="proj_heads_" + act,
    )(h, w)


def _proj_gate(h, w2, lb2):
    t, d = h.shape
    tm = _row_tile(t)
    tn = 512
    nh = tn // LANES
    ncb = w2.shape[2] // tn
    out_sds = lambda dt: jax.ShapeDtypeStruct((2, w2.shape[2] // LANES, t, LANES), dt)
    return pl.pallas_call(
        _proj_gate_kernel,
        out_shape=(out_sds(_f32), out_sds(_bf16)),
        grid=(2, ncb, t // tm),
        in_specs=[pl.BlockSpec((tm, d), lambda r, j, i: (i, 0)),
                  pl.BlockSpec((1, d, tn), lambda r, j, i: (r, 0, j)),
                  pl.BlockSpec((1, 1, tn), lambda r, j, i: (r, 0, j))],
        out_specs=(pl.BlockSpec((1, nh, tm, LANES), lambda r, j, i: (r, j, i, 0)),
                   pl.BlockSpec((1, nh, tm, LANES), lambda r, j, i: (r, j, i, 0))),
        compiler_params=_cparams(("arbitrary", "arbitrary", "arbitrary")),
        name="proj_gate",
    )(h, w2, lb2)


def _proj_sigmoid(h, w):
    t, d = h.shape
    tm = _row_tile(t)
    tn = 512
    return pl.pallas_call(
        _proj_sigmoid_kernel,
        out_shape=jax.ShapeDtypeStruct((t, w.shape[1]), _bf16),
        grid=(w.shape[1] // tn, t // tm),
        in_specs=[pl.BlockSpec((tm, d), lambda j, i: (i, 0)),
                  pl.BlockSpec((d, tn), lambda j, i: (0, j))],
        out_specs=pl.BlockSpec((tm, tn), lambda j, i: (i, j)),
        compiler_params=_cparams(("arbitrary", "arbitrary")),
        name="proj_sigmoid",
    )(h, w)


def _attn_kernel(q_ref, kp_ref, kc_ref, kn_ref, vp_ref, vc_ref, vn_ref,
                 kx_ref, vx_ref, bias_ref, sink_ref, o_ref):
    hd = A_HEAD_DIM
    blk = A_BLOCK
    nt = (((1,), (1,)), ((), ()))
    n_ctx = kx_ref.shape[0]
    ones = jnp.ones((n_ctx + 3 * blk, LANES), _bf16)
    for g in range(A_KV_HEADS):
        gs = slice(g * hd, (g + 1) * hd)
        qg = jnp.concatenate(
            [q_ref[:, (g * A_GROUP + h) * hd:(g * A_GROUP + h + 1) * hd] for h in range(A_GROUP)],
            axis=0)
        k_pc = jnp.concatenate([kp_ref[:, gs], kc_ref[:, gs]], axis=0)
        s_x = lax.dot_general(qg, kx_ref[:, gs], nt, preferred_element_type=_f32)
        s_pc = lax.dot_general(qg, k_pc, nt, preferred_element_type=_f32)
        s_n = lax.dot_general(qg, kn_ref[:, gs], nt, preferred_element_type=_f32) + bias_ref[0, 1]
        s_p = s_pc[:, 0:blk] + bias_ref[0, 0]
        s_c = s_pc[:, blk:2 * blk]
        mx = jnp.maximum(jnp.maximum(s_p, s_c), s_n)
        for j in range(n_ctx // LANES):
            mx = jnp.maximum(mx, s_x[:, j * LANES:(j + 1) * LANES])
        sink = sink_ref[g]
        m = jnp.maximum(jnp.max(mx, axis=-1, keepdims=True), sink)
        p = jnp.concatenate([jnp.exp2(s_x - m), jnp.exp2(s_p - m), jnp.exp2(s_c - m), jnp.exp2(s_n - m)],
                            axis=1).astype(_bf16)
        v_all = jnp.concatenate([vx_ref[:, gs], vp_ref[:, gs], vc_ref[:, gs], vn_ref[:, gs]], axis=0)
        o = jnp.dot(p, v_all, preferred_element_type=_f32)
        den = jnp.dot(p, ones, preferred_element_type=_f32)
        o = o / (den[:, 0:hd] + jnp.exp2(sink - m))
        for h in range(A_GROUP):
            c0 = (g * A_GROUP + h) * hd
            o_ref[:, c0:c0 + hd] = o[h * A_BLOCK:(h + 1) * A_BLOCK, :].astype(o_ref.dtype)


def _attention(q, k, v, sinks, n, l):
    blk = A_BLOCK
    nblk = n // blk
    off = l // blk
    rows = A_GROUP * blk
    r = np.arange(rows)[:, None] % blk
    j = np.arange(blk)[None, :]
    keep_prev = (j - blk - r) >= -WINDOW
    keep_next = (j + blk - r) <= WINDOW
    none = np.zeros_like(keep_prev)
    variants = [(none, keep_next), (keep_prev, keep_next), (keep_prev, none)]
    bias = jnp.asarray(np.where(np.array(variants), 0.0, NEG).astype(np.float32))
    sink_col = jnp.repeat(sinks.astype(_f32).reshape(A_KV_HEADS, A_GROUP) * LOG2E, blk, axis=1)
    sink_col = sink_col.reshape(A_KV_HEADS, rows, 1)

    def bias_idx(i):
        return (jnp.where(i == 0, 0, jnp.where(i == nblk - 1, 2, 1)), 0, 0, 0)

    kv_spec = lambda f: pl.BlockSpec((blk, A_KV_WIDTH), f)
    prev = lambda i: (jnp.maximum(i - 1, 0) + off, 0)
    cur = lambda i: (i + off, 0)
    nxt = lambda i: (jnp.minimum(i + 1, nblk - 1) + off, 0)
    return pl.pallas_call(
        _attn_kernel,
        out_shape=jax.ShapeDtypeStruct((n, A_WIDTH), _bf16),
        grid=(nblk,),
        in_specs=[pl.BlockSpec((blk, A_WIDTH), cur),
                  kv_spec(prev), kv_spec(cur), kv_spec(nxt),
                  kv_spec(prev), kv_spec(cur), kv_spec(nxt),
                  pl.BlockSpec((l, A_KV_WIDTH), lambda i: (0, 0)),
                  pl.BlockSpec((l, A_KV_WIDTH), lambda i: (0, 0)),
                  pl.BlockSpec((1, 2, rows, blk), bias_idx),
                  pl.BlockSpec((A_KV_HEADS, rows, 1), lambda i: (0, 0, 0))],
        out_specs=pl.BlockSpec((blk, A_WIDTH), lambda i: (i, 0)),
        compiler_params=_cparams(("arbitrary",)),
        name="window_attention",
    )(q, k, k, k, v, v, v, k, v, bias, sink_col)


def _split3(x):
    hi = x.astype(_bf16)
    r1 = x - hi.astype(_f32)
    mid = r1.astype(_bf16)
    lo = (r1 - mid.astype(_f32)).astype(_bf16)
    return hi, mid, lo


def _hgrn_intra_scores(q, k, b, emat, reverse):
    c = HG_CHUNK
    nsb = c // SUB
    nt = (((1,), (1,)), ((), ()))
    qb = (q * jnp.exp2(b)).astype(_bf16)
    b_tot = b[0:1, :] if reverse else b[c - 1:c, :]
    kend = (k * jnp.exp2(b_tot - b)).astype(_bf16)

    zero_row = jnp.zeros((1, LANES), _f32)
    zero_blk = jnp.zeros((SUB, LANES), _f32)
    y_rows, z_rows, w_cols = [], [], []
    for ib in range(nsb):
        r0 = ib * SUB
        bi = b[r0:r0 + SUB]
        qi = q[r0:r0 + SUB]
        pieces = []
        for s in range(SUB):
            bs = b[r0 + s:r0 + s + 1]
            ks = k[r0 + s:r0 + s + 1]
            pieces.append(qi * ks * jnp.exp2(jnp.minimum(bi - bs, 0.0)))
        y_rows.append(jnp.concatenate(pieces, axis=1))
        if reverse:
            lo_r, hi_r = r0 + SUB, c
            ref_b = b[r0 + SUB:r0 + SUB + 1] if ib < nsb - 1 else zero_row
        else:
            lo_r, hi_r = 0, r0
            ref_b = b[r0 - 1:r0] if ib > 0 else zero_row
        if hi_r > lo_r:
            qp = qi * jnp.exp2(bi - ref_b)
            kp = [k[lo_r:hi_r] * jnp.exp2(ref_b - b[lo_r:hi_r])]
            if lo_r > 0:
                kp.insert(0, jnp.zeros((lo_r, LANES), _f32))
            if hi_r < c:
                kp.append(jnp.zeros((c - hi_r, LANES), _f32))
            w_cols.append(jnp.concatenate(kp, axis=0) if len(kp) > 1 else kp[0])
        else:
            qp = zero_blk
            w_cols.append(jnp.zeros((c, LANES), _f32))
        z_rows.append(jnp.concatenate([qp if jb == ib else zero_blk for jb in range(nsb)], axis=1))
    y = jnp.concatenate(y_rows, axis=0).astype(_bf16)
    z = jnp.concatenate(z_rows, axis=0).astype(_bf16)
    w = jnp.concatenate(w_cols, axis=1).astype(_bf16)
    a_diag = jnp.dot(y, emat, preferred_element_type=_f32)
    a_off = lax.dot_general(z, w, nt, preferred_element_type=_f32)
    return a_diag, a_off, qb, kend, jnp.exp2(b_tot)


def _hgrn_intra_kernel(q_ref, k_ref, lf_ref, emat_ref, a_ref, qb_ref, ke_ref, dt_ref, *, reverse):
    c = HG_CHUNK
    nch = HG_BLOCK // c
    nh = q_ref.shape[0]
    ri = lax.broadcasted_iota(jnp.int32, (c, c), 0)
    ci = lax.broadcasted_iota(jnp.int32, (c, c), 1)
    tri = jnp.where((ci >= ri) if reverse else (ci <= ri), 1.0, 0.0).astype(_bf16)
    dmask = (ci // SUB == ri // SUB) & ((ci >= ri) if reverse else (ci <= ri))
    emat = emat_ref[...]

    def body(cc, carry):
        rows = pl.ds(pl.multiple_of(cc * c, c), c)
        lf_all = jnp.concatenate([lf_ref[0, h, rows, :] for h in range(nh)], axis=1)
        hi, mid, lo = _split3(lf_all)
        b_all = (jnp.dot(tri, hi, preferred_element_type=_f32)
                 + jnp.dot(tri, mid, preferred_element_type=_f32)
                 + jnp.dot(tri, lo, preferred_element_type=_f32))
        for h in range(nh):
            a_diag, a_off, qb, kend, dtot = _hgrn_intra_scores(
                q_ref[h, rows, :].astype(_f32), k_ref[0, h, rows, :].astype(_f32),
                b_all[:, h * LANES:(h + 1) * LANES], emat, reverse)
            qb_ref[h, rows, :] = qb
            ke_ref[h, rows, :] = kend
            dt_ref[cc, h:h + 1, :] = dtot
            a = a_off + jnp.where(dmask, a_diag[:, 0:c], 0.0)
            a_ref[h, rows, :] = jnp.concatenate([a, jnp.zeros_like(a)], axis=1).astype(a_ref.dtype)
        return carry

    lax.fori_loop(0, nch, body, 0)


def _hgrn_intra(rq, kk, lf, direction):
    nh, t, _ = rq.shape
    br = HG_BLOCK
    nch = br // HG_CHUNK
    reverse = direction == 1
    spec3 = pl.BlockSpec((nh, br, LANES), lambda s: (0, s, 0))
    spec4 = pl.BlockSpec((1, nh, br, LANES), lambda s: (direction, 0, s, 0))
    sds = jax.ShapeDtypeStruct((nh, t, LANES), _bf16)
    emat = (np.arange(SUB * LANES)[:, None] // LANES == np.arange(LANES)[None, :] % SUB)
    emat = jnp.asarray(emat.astype(np.float32), dtype=_bf16)
    return pl.pallas_call(
        functools.partial(_hgrn_intra_kernel, reverse=reverse),
        out_shape=(sds, sds, sds, jax.ShapeDtypeStruct((t // HG_CHUNK, nh, LANES), _f32)),
        grid=(t // br,),
        in_specs=[spec3, spec4, spec4, pl.BlockSpec(emat.shape, lambda s: (0, 0))],
        out_specs=(spec3, spec3, spec3, pl.BlockSpec((nch, nh, LANES), lambda s: (s, 0, 0))),
        compiler_params=_cparams(("arbitrary",)),
        name="hgrn2_intra_" + ("bwd" if reverse else "fwd"),
    )(rq, kk, lf, emat)


def _hgrn_state_kernel(af_ref, qf_ref, kf_ref, df_ref, vf_ref, ab_ref, qbk_ref, kb_ref, db_ref, vb_ref,
                       outf_ref, outb_ref, st_ref):
    c = HG_CHUNK
    nch = HG_BLOCK // c
    nh = qf_ref.shape[0]
    nt = (((1,), (1,)), ((), ()))
    tn = (((0,), (0,)), ((), ()))

    @pl.when(pl.program_id(0) == 0)
    def _():
        st_ref[...] = jnp.zeros(st_ref.shape, st_ref.dtype)

    dirs = ((0, False, af_ref, qf_ref, kf_ref, df_ref, vf_ref, outf_ref),
            (1, True, ab_ref, qbk_ref, kb_ref, db_ref, vb_ref, outb_ref))

    def body(cc, carry):
        for di, reverse, a_ref, q_ref, k_ref, d_ref, v_ref, out_ref in dirs:
            chunk = (nch - 1 - cc) if reverse else cc
            rows = pl.ds(pl.multiple_of(chunk * c, c), c)
            for h in range(nh):
                st = st_ref[di, h]
                v = v_ref[h, rows, :]
                o = (jnp.dot(a_ref[h, rows, 0:c], v, preferred_element_type=_f32)
                     + lax.dot_general(q_ref[h, rows, :], st.astype(_bf16), nt, preferred_element_type=_f32))
                out_ref[h, rows, :] = o.astype(out_ref.dtype)
                upd = lax.dot_general(v, k_ref[h, rows, :], tn, preferred_element_type=_f32)
                st_ref[di, h] = st * d_ref[chunk, pl.ds(h, 1), :] + upd
        return carry

    lax.fori_loop(0, nch, body, 0)


def _hgrn_state(intra_f, intra_b, rv, l):
    a_f, qb_f, ke_f, dt_f = intra_f
    a_b, qb_b, ke_b, dt_b = intra_b
    nh, t, _ = rv.shape
    br = HG_BLOCK
    nch = br // HG_CHUNK
    nb = t // br
    nc = l // br

    def blk_b(s):
        return jnp.where(s < nc, nc - 1 - s, nb - 1 - (s - nc))

    def specs(blk):
        s3 = pl.BlockSpec((nh, br, LANES), lambda s: (0, blk(s), 0))
        sd = pl.BlockSpec((nch, nh, LANES), lambda s: (blk(s), 0, 0))
        return [s3, s3, s3, sd, s3]

    fwd = specs(lambda s: s)
    bwd = specs(blk_b)
    return pl.pallas_call(
        _hgrn_state_kernel,
        out_shape=(jax.ShapeDtypeStruct((nh, t, LANES), _bf16),
                   jax.ShapeDtypeStruct((nh, t, LANES), _bf16)),
        grid=(nb,),
        in_specs=fwd + bwd,
        out_specs=(fwd[0], bwd[0]),
        scratch_shapes=[pltpu.VMEM((2, nh, B_DV, B_DK), _f32)],
        compiler_params=_cparams(("arbitrary",)),
        name="hgrn2_state",
    )(a_f, qb_f, ke_f, dt_f, rv, a_b, qb_b, ke_b, dt_b, rv)


def _merge_kernel(ya_ref, of_ref, ob_ref, rg_ref, ga_ref, gb_ref, x_ref, wa_ref, wb_ref, wo_ref,
                  wrh_ref, wrl_ref, br_ref, hg_ref, g2_ref, mod_ref,
                  x1_ref, h2_ref, route_ref, cnt_ref, run_ref):
    i = pl.program_id(0)
    tm, d = x_ref.shape

    @pl.when(i == 0)
    def _():
        run_ref[...] = jnp.zeros(run_ref.shape, run_ref.dtype)

    parts = []
    for h in range(B_HEADS):
        o = of_ref[h].astype(_f32) + ob_ref[h].astype(_f32)
        o = o * lax.rsqrt(jnp.mean(o * o, axis=-1, keepdims=True) + EPS)
        o = o * hg_ref[:, h * LANES:(h + 1) * LANES]
        parts.append((o * rg_ref[h].astype(_f32)).astype(_bf16))
    yb = jnp.concatenate(parts, axis=1)
    z0 = jnp.dot(ya_ref[...], wa_ref[...], preferred_element_type=_f32)
    z1 = jnp.dot(yb, wb_ref[...], preferred_element_type=_f32)
    merged = ga_ref[...].astype(_f32) * z0 + gb_ref[...].astype(_f32) * z1
    y = jnp.dot(merged.astype(_bf16), wo_ref[...], preferred_element_type=_f32)
    x1 = x_ref[...] + mod_ref[0:1, 2 * d:3 * d] * y
    x1_ref[...] = x1
    h2 = _rms_mod(x1, g2_ref[...], mod_ref[0:1, 3 * d:4 * d], mod_ref[0:1, 4 * d:5 * d])
    h2_ref[...] = _pack_bf16_pair(h2[:, 0:d // 2], h2[:, d // 2:d])

    hh = h2.astype(_bf16)
    hl = (h2 - hh.astype(_f32)).astype(_bf16)
    logits = (jnp.dot(hh, wrh_ref[...], preferred_element_type=_f32)
              + jnp.dot(hl, wrh_ref[...], preferred_element_type=_f32)
              + jnp.dot(hh, wrl_ref[...], preferred_element_type=_f32)) + br_ref[...]
    lane = lax.broadcasted_iota(jnp.int32, (tm, LANES), 1)
    work = jnp.where(lane < N_EXPERTS, logits, NEG)
    vals, idxs = [], []
    onehot = jnp.zeros((tm, LANES), _f32)
    for _ in range(TOP_K):
        m = jnp.max(work, axis=-1, keepdims=True)
        idx = jnp.min(jnp.where(work == m, lane, LANES), axis=-1, keepdims=True)
        sel = lane == idx
        vals.append(m)
        idxs.append(idx)
        onehot = jnp.where(sel, 1.0, onehot)
        work = jnp.where(sel, NEG, work)
    es = [jnp.exp(vv - vals[0]) for vv in vals]
    tot = es[0] + es[1] + es[2] + es[3]
    ri = lax.broadcasted_iota(jnp.int32, (tm, tm), 0)
    ci = lax.broadcasted_iota(jnp.int32, (tm, tm), 1)
    ltri = jnp.where(ci < ri, 1.0, 0.0).astype(_bf16)
    prefix = jnp.dot(ltri, onehot.astype(_bf16), preferred_element_type=_f32) + run_ref[...]
    route = jnp.zeros((tm, LANES), _f32)
    for j in range(TOP_K):
        rank = jnp.sum(jnp.where(lane == idxs[j], prefix, 0.0), axis=-1, keepdims=True)
        route = jnp.where(lane == j, idxs[j].astype(_f32), route)
        route = jnp.where(lane == TOP_K + j, es[j] / tot, route)
        route = jnp.where(lane == 2 * TOP_K + j, rank, route)
    route_ref[...] = route
    run_new = run_ref[...] + jnp.sum(onehot, axis=0, keepdims=True)
    run_ref[...] = run_new
    cnt_ref[...] = run_new


def _merge_route(ya, o_f, o_b, rg, gates, x2, wa, wb, wo, wr_hi, wr_lo, b_router, hg, g2, mod, l):
    n, d = x2.shape
    tm = 256
    off = l // tm
    const = lambda shape: pl.BlockSpec(shape, lambda i: (0,) * len(shape),
                                       pipeline_mode=pl.Buffered(1))
    head_spec = pl.BlockSpec((B_HEADS, tm, LANES), lambda i: (0, i + off, 0))
    return pl.pallas_call(
        _merge_kernel,
        out_shape=(jax.ShapeDtypeStruct((n, d), _f32),
                   jax.ShapeDtypeStruct((n, d // 2), jnp.uint32),
                   jax.ShapeDtypeStruct((n, LANES), _f32),
                   jax.ShapeDtypeStruct((1, LANES), _f32)),
        grid=(n // tm,),
        in_specs=[pl.BlockSpec((tm, A_WIDTH), lambda i: (i, 0)),
                  head_spec, head_spec, head_spec,
                  pl.BlockSpec((tm, d), lambda i: (i + off, 0)),
                  pl.BlockSpec((tm, d), lambda i: (i + off, 1)),
                  pl.BlockSpec((tm, d), lambda i: (i, 0)),
                  const(wa.shape), const(wb.shape), const(wo.shape),
                  const(wr_hi.shape), const(wr_lo.shape), const((1, LANES)),
                  const((1, B_WIDTH)), const((1, d)), const(mod.shape)],
        out_specs=(pl.BlockSpec((tm, d), lambda i: (i, 0)),
                   pl.BlockSpec((tm, d // 2), lambda i: (i, 0)),
                   pl.BlockSpec((tm, LANES), lambda i: (i, 0)),
                   pl.BlockSpec((1, LANES), lambda i: (0, 0))),
        scratch_shapes=[pltpu.VMEM((1, LANES), _f32)],
        compiler_params=_cparams(("arbitrary",)),
        name="merge_route",
    )(ya, o_f, o_b, rg, gates, gates, x2, wa, wb, wo, wr_hi, wr_lo, b_router, hg, g2, mod)


def _zero_fill_tiles(zero_ref, dst_ref, first, last, sem, cols=(None,)):
    tm = MOE_TM

    def copy(t, col):
        rows = pl.ds(pl.multiple_of(t * tm, tm), tm)
        dst = dst_ref.at[rows] if col is None else dst_ref.at[rows, col]
        return pltpu.make_async_copy(zero_ref, dst, sem)

    def start(t, carry):
        for col in cols:
            copy(t, col).start()
        return carry

    def wait(t, carry):
        for col in cols:
            copy(t, col).wait()
        return carry

    lax.fori_loop(first, last, start, 0)
    lax.fori_loop(first, last, wait, 0)


def _dispatch_kernel(lt_ref, nt_ref, dest_ref, h2_ref, xs_ref, zero_ref, sem, zsem):
    tm = h2_ref.shape[0]

    @pl.when(pl.program_id(0) == 0)
    def _():
        zero_ref[...] = jnp.zeros(zero_ref.shape, zero_ref.dtype)
        for e in range(N_EXPERTS):
            @pl.when(nt_ref[e] > 0)
            def _():
                pltpu.make_async_copy(zero_ref, xs_ref.at[pl.ds(lt_ref[e] * MOE_TM, MOE_TM)], zsem).start()
        for e in range(N_EXPERTS):
            @pl.when(nt_ref[e] > 0)
            def _():
                pltpu.make_async_copy(zero_ref, xs_ref.at[pl.ds(lt_ref[e] * MOE_TM, MOE_TM)], zsem).wait()
        _zero_fill_tiles(zero_ref, xs_ref, lt_ref[N_EXPERTS - 1] + 1, xs_ref.shape[0] // MOE_TM, zsem)

    def issue(t, carry):
        for j in range(TOP_K):
            dst = dest_ref[t * TOP_K + j]
            pltpu.make_async_copy(h2_ref.at[pl.ds(t, 1)], xs_ref.at[pl.ds(dst, 1)], sem).start()
        return carry

    lax.fori_loop(0, tm, issue, 0)
    for _ in range(TOP_K):
        pltpu.make_async_copy(h2_ref, xs_ref.at[pl.ds(0, tm)], sem).wait()


def _dispatch(h2, dest_flat, last_tile, ntile, n_slots):
    n, d = h2.shape
    tm = 256
    return pl.pallas_call(
        _dispatch_kernel,
        out_shape=jax.ShapeDtypeStruct((n_slots, d), h2.dtype),
        grid_spec=pltpu.PrefetchScalarGridSpec(
            num_scalar_prefetch=2,
            grid=(n // tm,),
            in_specs=[pl.BlockSpec((tm * TOP_K,), lambda i, lt, nt: (i,), memory_space=pltpu.SMEM),
                      pl.BlockSpec((tm, d), lambda i, lt, nt: (i, 0))],
            out_specs=pl.BlockSpec(memory_space=pl.ANY),
            scratch_shapes=[pltpu.VMEM((MOE_TM, d), h2.dtype), pltpu.SemaphoreType.DMA,
                            pltpu.SemaphoreType.DMA]),
        compiler_params=_cparams(("arbitrary",)),
        name="moe_dispatch",
    )(last_tile, ntile, dest_flat, h2)


def _cast_rows(src_ref, dst_ref):
    rows = dst_ref.shape[0]
    step = 256

    def body(i, carry):
        r = pl.ds(pl.multiple_of(i * step, step), step)
        dst_ref[r, :] = src_ref[r, :].astype(dst_ref.dtype)
        return carry

    lax.fori_loop(0, rows // step, body, 0)


def _expert_stream(ts_ref, nt_ref, weights, in_copy, out_copy, compute):
    nb = MOE_NBUF
    n_exp = nt_ref.shape[0]
    total = ts_ref[n_exp - 1] + nt_ref[n_exp - 1]
    for k in range(nb - 1):
        @pl.when(total > k)
        def _():
            in_copy(k, k).start(priority=1)
    for fetch, _, _ in weights:
        fetch(0, 0).start()

    def expert_body(e, carry):
        wslot = lax.rem(e, 2)
        for fetch, _, _ in weights:
            fetch(e, wslot).wait()

        @pl.when(e + 1 < n_exp)
        def _():
            for fetch, _, _ in weights:
                fetch(e + 1, 1 - wslot).start()

        for _, stage_ref, bf_ref in weights:
            _cast_rows(stage_ref.at[wslot], bf_ref)
        t0 = ts_ref[e]

        def tile_body(t, c2):
            g = t0 + t
            slot = lax.rem(g, nb)
            in_copy(g, slot).wait()

            @pl.when(g + nb - 1 < total)
            def _():
                in_copy(g + nb - 1, lax.rem(g + nb - 1, nb)).start(priority=1)

            @pl.when(g >= nb)
            def _():
                out_copy(g - nb, slot).wait()

            compute(e, slot)
            out_copy(g, slot).start(priority=1)
            return c2

        lax.fori_loop(0, nt_ref[e], tile_body, 0)
        return carry

    lax.fori_loop(0, n_exp, expert_body, 0)
    for k in range(1, nb + 1):
        @pl.when(total >= k)
        def _():
            out_copy(total - k, lax.rem(total - k, nb)).wait()
    return total


def _tile_rows(g):
    return pl.ds(pl.multiple_of(g * MOE_TM, MOE_TM), MOE_TM)


def _moe_up_kernel(ts_ref, nt_ref, xs_ref, wg_ref, wu_ref, bg_ref, bu_ref, act_ref,
                   wgs_ref, wus_ref, wgb_ref, wub_ref, xbuf, obuf, sem_wg, sem_wu, sem_in, sem_out):
    c = pl.program_id(0)
    tm = MOE_TM
    fc = wgb_ref.shape[1]
    col = pl.ds(pl.multiple_of(c * fc, fc), fc)

    def in_copy(g, slot):
        return pltpu.make_async_copy(xs_ref.at[_tile_rows(g)], xbuf.at[slot], sem_in.at[slot])

    def out_copy(g, slot):
        return pltpu.make_async_copy(obuf.at[slot], act_ref.at[_tile_rows(g), col], sem_out.at[slot])

    def fetch_g(e, slot):
        return pltpu.make_async_copy(wg_ref.at[e, :, col], wgs_ref.at[slot], sem_wg.at[slot])

    def fetch_u(e, slot):
        return pltpu.make_async_copy(wu_ref.at[e, :, col], wus_ref.at[slot], sem_wu.at[slot])

    def compute(e, slot):
        x_lo, x_hi = _unpack_bf16_pair(xbuf[slot])
        x_lo, x_hi = x_lo.astype(_bf16), x_hi.astype(_bf16)
        half = x_lo.shape[1]

        def proj(w_ref, b_ref):
            return (jnp.dot(x_lo, w_ref[0:half, :], preferred_element_type=_f32)
                    + jnp.dot(x_hi, w_ref[half:2 * half, :], preferred_element_type=_f32)
                    + b_ref[pl.ds(e, 1), col])

        g = jnp.minimum(proj(wgb_ref, bg_ref), SWIGLU_LIMIT)
        u = jnp.clip(proj(wub_ref, bu_ref), -SWIGLU_LIMIT, SWIGLU_LIMIT)
        obuf[slot] = (g * _sigmoid(SWIGLU_ALPHA * g) * (u + 1.0)).astype(obuf.dtype)

    total = _expert_stream(ts_ref, nt_ref, [(fetch_g, wgs_ref, wgb_ref), (fetch_u, wus_ref, wub_ref)],
                           in_copy, out_copy, compute)
    obuf[0] = jnp.zeros(obuf.shape[1:], obuf.dtype)
    _zero_fill_tiles(obuf.at[0], act_ref, total, act_ref.shape[0] // tm, sem_out.at[0], cols=(col,))


def _moe_up(xs, wg, wu, bg, bu, tstart, ntile):
    n_slots = xs.shape[0]
    ne, d, ff = wg.shape
    fc = ff // 2
    tm = MOE_TM
    any_spec = pl.BlockSpec(memory_space=pl.ANY)
    b_spec = pl.BlockSpec((ne, ff), lambda c, ts, nt: (0, 0))
    return pl.pallas_call(
        _moe_up_kernel,
        out_shape=jax.ShapeDtypeStruct((n_slots, ff), _bf16),
        grid_spec=pltpu.PrefetchScalarGridSpec(
            num_scalar_prefetch=2,
            grid=(ff // fc,),
            in_specs=[any_spec, any_spec, any_spec, b_spec, b_spec],
            out_specs=any_spec,
            scratch_shapes=[pltpu.VMEM((2, d, fc), _f32), pltpu.VMEM((2, d, fc), _f32),
                            pltpu.VMEM((d, fc), _bf16), pltpu.VMEM((d, fc), _bf16),
                            pltpu.VMEM((MOE_NBUF, tm, xs.shape[1]), xs.dtype),
                            pltpu.VMEM((MOE_NBUF, tm, fc), _bf16),
                            pltpu.SemaphoreType.DMA((2,)), pltpu.SemaphoreType.DMA((2,)),
                            pltpu.SemaphoreType.DMA((MOE_NBUF,)), pltpu.SemaphoreType.DMA((MOE_NBUF,))]),
        compiler_params=_cparams(("arbitrary",)),
        name="moe_gate_up",
    )(tstart, ntile, xs, wg, wu, bg, bu)


def _moe_down_kernel(ts_ref, nt_ref, act_ref, wd_ref, bd_ref, ys_ref,
                     wds_ref, wdb_ref, abuf, ybuf, sem_w, sem_in, sem_out):
    tm = MOE_TM
    d = wdb_ref.shape[1]

    def in_copy(g, slot):
        return pltpu.make_async_copy(act_ref.at[_tile_rows(g)], abuf.at[slot], sem_in.at[slot])

    def out_copy(g, slot):
        return pltpu.make_async_copy(ybuf.at[slot], ys_ref.at[_tile_rows(g)], sem_out.at[slot])

    def fetch(e, slot):
        return pltpu.make_async_copy(wd_ref.at[e], wds_ref.at[slot], sem_w.at[slot])

    def compute(e, slot):
        y = jnp.dot(abuf[slot], wdb_ref[...], preferred_element_type=_f32) + bd_ref[pl.ds(e, 1), :]
        ybuf[slot] = _pack_bf16_pair(y[:, 0:d // 2], y[:, d // 2:d])

    total = _expert_stream(ts_ref, nt_ref, [(fetch, wds_ref, wdb_ref)], in_copy, out_copy, compute)
    ybuf[0] = jnp.zeros(ybuf.shape[1:], ybuf.dtype)
    _zero_fill_tiles(ybuf.at[0], ys_ref, total, ys_ref.shape[0] // tm, sem_out.at[0])


def _moe_down(act, wd, bd, tstart, ntile):
    n_slots, ff = act.shape
    ne, _, d = wd.shape
    tm = MOE_TM
    any_spec = pl.BlockSpec(memory_space=pl.ANY)
    return pl.pallas_call(
        _moe_down_kernel,
        out_shape=jax.ShapeDtypeStruct((n_slots, d // 2), jnp.uint32),
        grid_spec=pltpu.PrefetchScalarGridSpec(
            num_scalar_prefetch=2,
            grid=(1,),
            in_specs=[any_spec, any_spec, pl.BlockSpec((ne, d), lambda i, ts, nt: (0, 0))],
            out_specs=any_spec,
            scratch_shapes=[pltpu.VMEM((2, ff, d), _f32), pltpu.VMEM((ff, d), _bf16),
                            pltpu.VMEM((MOE_NBUF, tm, ff), _bf16), pltpu.VMEM((MOE_NBUF, tm, d // 2), jnp.uint32),
                            pltpu.SemaphoreType.DMA((2,)),
                            pltpu.SemaphoreType.DMA((MOE_NBUF,)), pltpu.SemaphoreType.DMA((MOE_NBUF,))]),
        compiler_params=_cparams(("arbitrary",)),
        name="moe_down",
    )(tstart, ntile, act, wd, bd)


def _combine_kernel(dest_ref, ys_ref, route_ref, x1_ref, mod_ref, gf_ref, o_ref, buf_ref, sem):
    tm, d = x1_ref.shape

    def issue(t, carry):
        for j in range(TOP_K):
            src = dest_ref[t * TOP_K + j]
            pltpu.make_async_copy(ys_ref.at[pl.ds(src, 1)], buf_ref.at[j, pl.ds(t, 1)], sem).start()
        return carry

    lax.fori_loop(0, tm, issue, 0)
    for j in range(TOP_K):
        pltpu.make_async_copy(ys_ref.at[pl.ds(0, tm)], buf_ref.at[j], sem).wait()
    acc_lo = jnp.zeros((tm, d // 2), _f32)
    acc_hi = jnp.zeros((tm, d // 2), _f32)
    for j in range(TOP_K):
        gate = route_ref[:, TOP_K + j:TOP_K + j + 1]
        y_lo, y_hi = _unpack_bf16_pair(buf_ref[j])
        acc_lo = acc_lo + gate * y_lo
        acc_hi = acc_hi + gate * y_hi
    x2 = x1_ref[...] + mod_ref[0:1, 5 * d:6 * d] * jnp.concatenate([acc_lo, acc_hi], axis=1)
    y = x2 * lax.rsqrt(jnp.mean(x2 * x2, axis=-1, keepdims=True) + EPS)
    o_ref[...] = y * gf_ref[...]


def _combine(ys, dest_flat, route, x1, mod, gf):
    n, d = x1.shape
    tm = 256
    return pl.pallas_call(
        _combine_kernel,
        out_shape=jax.ShapeDtypeStruct((n, d), _f32),
        grid=(n // tm,),
        in_specs=[pl.BlockSpec((tm * TOP_K,), lambda i: (i,), memory_space=pltpu.SMEM),
                  pl.BlockSpec(memory_space=pl.ANY),
                  pl.BlockSpec((tm, LANES), lambda i: (i, 0)),
                  pl.BlockSpec((tm, d), lambda i: (i, 0)),
                  pl.BlockSpec(mod.shape, lambda i: (0, 0)),
                  pl.BlockSpec((1, d), lambda i: (0, 0))],
        out_specs=pl.BlockSpec((tm, d), lambda i: (i, 0)),
        scratch_shapes=[pltpu.VMEM((TOP_K, tm, d // 2), jnp.uint32), pltpu.SemaphoreType.DMA],
        compiler_params=_cparams(("arbitrary",)),
        name="moe_combine",
    )(dest_flat, ys, route, x1, mod, gf.reshape(1, d))


def _rope_tables(n, l):
    rows = n // GRID_W
    inv_freq = ROPE_BASE ** (-jnp.arange(ROPE_PAIRS_PER_AXIS, dtype=_f32) / ROPE_PAIRS_PER_AXIS)
    ang_r = jnp.arange(rows, dtype=_f32)[:, None] * inv_freq
    ang_c = jnp.arange(GRID_W, dtype=_f32)[:, None] * inv_freq
    rep = lambda t_r, t_c: jnp.concatenate(
        [jnp.repeat(t_r, GRID_W, axis=0), jnp.tile(t_c, (rows, 1))], axis=-1)
    cos, sin = rep(jnp.cos(ang_r), jnp.cos(ang_c)), rep(jnp.sin(ang_r), jnp.sin(ang_c))
    cos_t = jnp.tile(cos, (1, LANES // 32))
    sin_t = jnp.tile(jnp.concatenate([-sin, sin], axis=-1), (1, LANES // A_HEAD_DIM))
    cos_t = jnp.concatenate([jnp.ones((l, LANES), _f32), cos_t], axis=0)
    sin_t = jnp.concatenate([jnp.zeros((l, LANES), _f32), sin_t], axis=0)
    return cos_t, sin_t


def _routing_tables(route, counts):
    tm = MOE_TM
    idx = route[:, 0:TOP_K].astype(jnp.int32)
    rank = route[:, 2 * TOP_K:3 * TOP_K].astype(jnp.int32)
    cnt = counts[0, :N_EXPERTS].astype(jnp.int32)
    ntile = (cnt + tm - 1) // tm
    tstart = jnp.cumsum(ntile) - ntile
    slot0 = tstart * tm
    onehot = idx[:, :, None] == jnp.arange(N_EXPERTS, dtype=jnp.int32)
    dest = jnp.sum(jnp.where(onehot, slot0, 0), axis=-1) + rank
    return dest.reshape(-1), tstart, ntile


def kernel(x, c, ctx, c_ctx, norm_mix_g, norm_ffn_g, w_mod, b_mod, w_in, attn_sinks,
           lb_fwd_logits, lb_bwd_logits, hgrn_norm_g, w_branch, w_out, w_router, b_router,
           w_e_gate, b_e_gate, w_e_up, b_e_up, w_e_down, b_e_down, final_norm_g):
    bsz, n, d = x.shape
    l = ctx.shape[1]
    assert bsz == 1 and d == D_MODEL and norm_mix_g.shape[0] == 1
    assert n % HG_BLOCK == 0 and l % HG_BLOCK == 0 and n // A_BLOCK >= 2
    x2 = x.reshape(n, d)
    ctx2 = ctx.reshape(l, d)

    mod = _mod_vectors(c, c_ctx, w_mod[0], b_mod[0])
    h_all = _norm_all(x2, ctx2, norm_mix_g[0], mod)

    w = w_in[0].astype(_bf16)
    c0 = 0
    segs = []
    for width in (A_WIDTH + 2 * A_KV_WIDTH, B_WIDTH, B_WIDTH, B_WIDTH, B_WIDTH, B_WIDTH, 2 * d):
        segs.append(w[:, c0:c0 + width])
        c0 += width
    w_qkv, w_rq, w_zf, w_zb, w_rv, w_rg, w_gates = segs
    cos_t, sin_t = _rope_tables(n, l)
    q, k, v = _proj_qkv(h_all, w_qkv, cos_t, sin_t)
    rq = _proj_heads(h_all, w_rq, "silu")
    rv = _proj_heads(h_all, w_rv, "none")
    rg = _proj_heads(h_all, w_rg, "silu")
    lb_f = jax.nn.softmax(lb_fwd_logits.astype(_f32), axis=0)[0]
    lb_b = jax.nn.softmax(lb_bwd_logits.astype(_f32), axis=0)[0]
    lb2 = jnp.stack([lb_f, lb_b]).reshape(2, 1, B_WIDTH)
    lf, kk = _proj_gate(h_all, jnp.stack([w_zf, w_zb]), lb2)
    gates = _proj_sigmoid(h_all, w_gates)

    ya = _attention(q, k, v, attn_sinks[0], n, l)
    o_f, o_b = _hgrn_state(_hgrn_intra(rq, kk, lf, 0), _hgrn_intra(rq, kk, lf, 1), rv, l)

    wr = jnp.zeros((d, LANES), _f32).at[:, :N_EXPERTS].set(w_router[0].astype(_f32))
    wr_hi = wr.astype(_bf16)
    wr_lo = (wr - wr_hi.astype(_f32)).astype(_bf16)
    br = jnp.zeros((1, LANES), _f32).at[0, :N_EXPERTS].set(b_router[0].astype(_f32))
    x1, h2, route, counts = _merge_route(
        ya, o_f, o_b, rg, gates, x2,
        w_branch[0, 0].astype(_bf16), w_branch[0, 1].astype(_bf16), w_out[0].astype(_bf16),
        wr_hi, wr_lo, br, hgrn_norm_g[0].reshape(1, B_WIDTH).astype(_f32),
        norm_ffn_g[0].reshape(1, d), mod, l)

    n_tiles = (n * TOP_K) // MOE_TM + N_EXPERTS
    dest, tstart, ntile = _routing_tables(route, counts)
    xs = _dispatch(h2, dest, tstart + ntile - 1, ntile, n_tiles * MOE_TM)
    act = _moe_up(xs, w_e_gate[0], w_e_up[0], b_e_gate[0], b_e_up[0], tstart, ntile)
    ys = _moe_down(act, w_e_down[0], b_e_down[0], tstart, ntile)
    out = _combine(ys, dest, route, x1, mod, final_norm_g)
    return out.reshape(bsz, n, d)
```

```python
import functools

import jax
import jax.numpy as jnp
import numpy as np
from jax import lax
from jax.experimental import pallas as pl
from jax.experimental.pallas import tpu as pltpu

D_MODEL = 2048
GRID_W = 64
EPS = 1e-6
A_HEADS = 16
A_KV_HEADS = 2
A_GROUP = A_HEADS // A_KV_HEADS
A_HEAD_DIM = 64
A_WIDTH = A_HEADS * A_HEAD_DIM
A_KV_WIDTH = A_KV_HEADS * A_HEAD_DIM
WINDOW = 128
A_BLOCK = 128
ROPE_BASE = 10000.0
ROPE_PAIRS_PER_AXIS = A_HEAD_DIM // 4
B_HEADS = 8
B_DK = 128
B_DV = 128
B_WIDTH = B_HEADS * B_DV
N_EXPERTS = 32
TOP_K = 4
EXPERT_FF = 2048
SWIGLU_LIMIT = 7.0
SWIGLU_ALPHA = 1.702

LANES = 128
SUB = 8
HG_CHUNK = 64
HG_BLOCK = 256
MOE_TM = 256
MOE_NBUF = 3
NEG = -1e30
LOG2E = 1.4426950408889634
VMEM_LIMIT = 56 * 1024 * 1024

_f32 = jnp.float32
_bf16 = jnp.bfloat16


def _cparams(sem):
    return pltpu.CompilerParams(dimension_semantics=sem, vmem_limit_bytes=VMEM_LIMIT)


def _sigmoid(x):
    return 1.0 / (1.0 + jnp.exp(-x))


def _pack_bf16_pair(lo, hi):
    lo_bits = lax.bitcast_convert_type(lo.astype(_bf16).astype(_f32), jnp.uint32)
    hi_bits = lax.bitcast_convert_type(hi.astype(_bf16).astype(_f32), jnp.uint32)
    return (lo_bits >> 16) | (hi_bits & jnp.uint32(0xFFFF0000))


def _unpack_bf16_pair(w):
    lo = lax.bitcast_convert_type(w << 16, _f32)
    hi = lax.bitcast_convert_type(w & jnp.uint32(0xFFFF0000), _f32)
    return lo, hi


def _mod_kernel(s_ref, w_ref, b_ref, o_ref):
    tn = w_ref.shape[1]
    for r in range(2):
        s = s_ref[r]
        s = s * _sigmoid(s)
        for j in range(tn // LANES):
            sl = slice(j * LANES, (j + 1) * LANES)
            acc = jnp.sum(w_ref[:, sl] * s, axis=0, keepdims=True)
            o_ref[r:r + 1, sl] = acc + b_ref[:, sl]


def _mod_vectors(c, c_ctx, w_mod, b_mod):
    d, n_out = w_mod.shape
    tn = 1024
    s = jnp.stack([c.reshape(d), c_ctx.reshape(d)]).astype(_f32)
    s = jnp.broadcast_to(s[:, :, None], (2, d, LANES))
    return pl.pallas_call(
        _mod_kernel,
        out_shape=jax.ShapeDtypeStruct((2, n_out), _f32),
        grid=(n_out // tn,),
        in_specs=[pl.BlockSpec((2, d, LANES), lambda j: (0, 0, 0)),
                  pl.BlockSpec((d, tn), lambda j: (0, j)),
                  pl.BlockSpec((1, tn), lambda j: (0, j))],
        out_specs=pl.BlockSpec((2, tn), lambda j: (0, j)),
        compiler_params=_cparams(("arbitrary",)),
        name="mod_vectors",
    )(s, w_mod, b_mod.reshape(1, n_out))


def _rms_mod(xf, g, shift, scale):
    y = xf * lax.rsqrt(jnp.mean(xf * xf, axis=-1, keepdims=True) + EPS)
    return (y * g) * (1.0 + scale) + shift


def _norm_kernel(x_ref, ctx_ref, g_ref, mod_ref, o_ref, *, n_ctx_tiles):
    i = pl.program_id(0)
    d = x_ref.shape[1]

    @pl.when(i < n_ctx_tiles)
    def _():
        o_ref[...] = _rms_mod(ctx_ref[...], g_ref[...], mod_ref[1:2, 0:d],
                              mod_ref[1:2, d:2 * d]).astype(o_ref.dtype)

    @pl.when(i >= n_ctx_tiles)
    def _():
        o_ref[...] = _rms_mod(x_ref[...], g_ref[...], mod_ref[0:1, 0:d],
                              mod_ref[0:1, d:2 * d]).astype(o_ref.dtype)


def _norm_all(x2, ctx2, g, mod):
    n, d = x2.shape
    l = ctx2.shape[0]
    tm = 256
    nct = l // tm
    return pl.pallas_call(
        functools.partial(_norm_kernel, n_ctx_tiles=nct),
        out_shape=jax.ShapeDtypeStruct((l + n, d), _bf16),
        grid=((l + n) // tm,),
        in_specs=[pl.BlockSpec((tm, d), lambda i: (jnp.maximum(i - nct, 0), 0)),
                  pl.BlockSpec((tm, d), lambda i: (jnp.minimum(i, nct - 1), 0)),
                  pl.BlockSpec((1, d), lambda i: (0, 0)),
                  pl.BlockSpec(mod.shape, lambda i: (0, 0))],
        out_specs=pl.BlockSpec((tm, d), lambda i: (i, 0)),
        compiler_params=_cparams(("arbitrary",)),
        name="adaln_norm",
    )(x2, ctx2, g.reshape(1, d), mod)


def _rope(a, cos, sin):
    lane = lax.broadcasted_iota(jnp.int32, a.shape, 1)
    first = (lane % A_HEAD_DIM) < (A_HEAD_DIM // 2)
    rot = jnp.where(first, pltpu.roll(a, LANES - 32, 1), pltpu.roll(a, 32, 1))
    return a * cos + rot * sin


def _proj_qkv_kernel(h_ref, w_ref, cos_ref, sin_ref, q_ref, k_ref, v_ref):
    acc = jnp.dot(h_ref[...], w_ref[...], preferred_element_type=_f32)
    cos = cos_ref[...]
    sin = sin_ref[...]
    scale = A_HEAD_DIM ** -0.5 * LOG2E
    for j in range(A_WIDTH // LANES):
        sl = slice(j * LANES, (j + 1) * LANES)
        q_ref[:, sl] = (_rope(acc[:, sl], cos, sin) * scale).astype(q_ref.dtype)
    k_ref[...] = _rope(acc[:, A_WIDTH:A_WIDTH + LANES], cos, sin).astype(k_ref.dtype)
    v_ref[...] = acc[:, A_WIDTH + LANES:A_WIDTH + 2 * LANES].astype(v_ref.dtype)


def _proj_heads_kernel(h_ref, w_ref, o_ref, *, act):
    acc = jnp.dot(h_ref[...], w_ref[...], preferred_element_type=_f32)
    if act == "silu":
        acc = acc * _sigmoid(acc)
    for hh in range(o_ref.shape[0]):
        o_ref[hh] = acc[:, hh * LANES:(hh + 1) * LANES].astype(o_ref.dtype)


def _proj_gate_kernel(h_ref, w_ref, lb_ref, lf_ref, kk_ref):
    z = jnp.dot(h_ref[...], w_ref[0], preferred_element_type=_f32)
    lb = lb_ref[0]
    sg = _sigmoid(z)
    logf = jnp.log2(lb + (1.0 - lb) * sg)
    kk = (1.0 - lb) * _sigmoid(-z)
    for hh in range(lf_ref.shape[1]):
        sl = slice(hh * LANES, (hh + 1) * LANES)
        lf_ref[0, hh] = logf[:, sl]
        kk_ref[0, hh] = kk[:, sl].astype(kk_ref.dtype)


def _proj_sigmoid_kernel(h_ref, w_ref, o_ref):
    acc = jnp.dot(h_ref[...], w_ref[...], preferred_element_type=_f32)
    o_ref[...] = _sigmoid(acc).astype(o_ref.dtype)


def _row_tile(t):
    for tm in (1280, 640, 256, 128):
        if t % tm == 0:
            return tm
    raise ValueError(t)


def _proj_qkv(h, w, cos_t, sin_t):
    t, d = h.shape
    tm = _row_tile(t)
    ncol = w.shape[1]
    return pl.pallas_call(
        _proj_qkv_kernel,
        out_shape=(jax.ShapeDtypeStruct((t, A_WIDTH), _bf16),
                   jax.ShapeDtypeStruct((t, A_KV_WIDTH), _bf16),
                   jax.ShapeDtypeStruct((t, A_KV_WIDTH), _bf16)),
        grid=(t // tm,),
        in_specs=[pl.BlockSpec((tm, d), lambda i: (i, 0)),
                  pl.BlockSpec((d, ncol), lambda i: (0, 0)),
                  pl.BlockSpec((tm, LANES), lambda i: (i, 0)),
                  pl.BlockSpec((tm, LANES), lambda i: (i, 0))],
        out_specs=(pl.BlockSpec((tm, A_WIDTH), lambda i: (i, 0)),
                   pl.BlockSpec((tm, A_KV_WIDTH), lambda i: (i, 0)),
                   pl.BlockSpec((tm, A_KV_WIDTH), lambda i: (i, 0))),
        compiler_params=_cparams(("arbitrary",)),
        name="proj_qkv",
    )(h, w, cos_t, sin_t)


def _proj_heads(h, w, act):
    t, d = h.shape
    tm = _row_tile(t)
    tn = 1024
    nh = tn // LANES
    return pl.pallas_call(
        functools.partial(_proj_heads_kernel, act=act),
        out_shape=jax.ShapeDtypeStruct((w.shape[1] // LANES, t, LANES), _bf16),
        grid=(w.shape[1] // tn, t // tm),
        in_specs=[pl.BlockSpec((tm, d), lambda j, i: (i, 0)),
                  pl.BlockSpec((d, tn), lambda j, i: (0, j))],
        out_specs=pl.BlockSpec((nh, tm, LANES), lambda j, i: (j, i, 0)),
        compiler_params=_cparams(("arbitrary", "arbitrary")),
        name="proj_heads_" + act,
    )(h, w)


def _proj_gate(h, w2, lb2):
    t, d = h.shape
    tm = _row_tile(t)
    tn = 1024
    nh = tn // LANES
    ncb = w2.shape[2] // tn
    out_sds = lambda dt: jax.ShapeDtypeStruct((2, w2.shape[2] // LANES, t, LANES), dt)
    return pl.pallas_call(
        _proj_gate_kernel,
        out_shape=(out_sds(_f32), out_sds(_bf16)),
        grid=(2, ncb, t // tm),
        in_specs=[pl.BlockSpec((tm, d), lambda r, j, i: (i, 0)),
                  pl.BlockSpec((1, d, tn), lambda r, j, i: (r, 0, j)),
                  pl.BlockSpec((1, 1, tn), lambda r, j, i: (r, 0, j))],
        out_specs=(pl.BlockSpec((1, nh, tm, LANES), lambda r, j, i: (r, j, i, 0)),
                   pl.BlockSpec((1, nh, tm, LANES), lambda r, j, i: (r, j, i, 0))),
        compiler_params=_cparams(("arbitrary", "arbitrary", "arbitrary")),
        name="proj_gate",
    )(h, w2, lb2)


def _proj_sigmoid(h, w):
    t, d = h.shape
    tm = _row_tile(t)
    tn = 1024
    return pl.pallas_call(
        _proj_sigmoid_kernel,
        out_shape=jax.ShapeDtypeStruct((t, w.shape[1]), _bf16),
        grid=(w.shape[1] // tn, t // tm),
        in_specs=[pl.BlockSpec((tm, d), lambda j, i: (i, 0)),
                  pl.BlockSpec((d, tn), lambda j, i: (0, j))],
        out_specs=pl.BlockSpec((tm, tn), lambda j, i: (i, j)),
        compiler_params=_cparams(("arbitrary", "arbitrary")),
        name="proj_sigmoid",
    )(h, w)


def _attn_kernel(q_ref, kp_ref, kc_ref, kn_ref, vp_ref, vc_ref, vn_ref,
                 kx_ref, vx_ref, bias_ref, sink_ref, o_ref):
    hd = A_HEAD_DIM
    blk = A_BLOCK
    nt = (((1,), (1,)), ((), ()))
    n_ctx = kx_ref.shape[0]
    ones = jnp.ones((n_ctx + 3 * blk, LANES), _bf16)
    for g in range(A_KV_HEADS):
        gs = slice(g * hd, (g + 1) * hd)
        qg = jnp.concatenate(
            [q_ref[:, (g * A_GROUP + h) * hd:(g * A_GROUP + h + 1) * hd] for h in range(A_GROUP)],
            axis=0)
        k_pc = jnp.concatenate([kp_ref[:, gs], kc_ref[:, gs]], axis=0)
        s_x = lax.dot_general(qg, kx_ref[:, gs], nt, preferred_element_type=_f32)
        s_pc = lax.dot_general(qg, k_pc, nt, preferred_element_type=_f32)
        s_n = lax.dot_general(qg, kn_ref[:, gs], nt, preferred_element_type=_f32) + bias_ref[0, 1]
        s_p = s_pc[:, 0:blk] + bias_ref[0, 0]
        s_c = s_pc[:, blk:2 * blk]
        mx = jnp.maximum(jnp.maximum(s_p, s_c), s_n)
        for j in range(n_ctx // LANES):
            mx = jnp.maximum(mx, s_x[:, j * LANES:(j + 1) * LANES])
        sink = sink_ref[g]
        m = jnp.maximum(jnp.max(mx, axis=-1, keepdims=True), sink)
        p = jnp.concatenate([jnp.exp2(s_x - m), jnp.exp2(s_p - m), jnp.exp2(s_c - m), jnp.exp2(s_n - m)],
                            axis=1).astype(_bf16)
        v_all = jnp.concatenate([vx_ref[:, gs], vp_ref[:, gs], vc_ref[:, gs], vn_ref[:, gs]], axis=0)
        o = jnp.dot(p, v_all, preferred_element_type=_f32)
        den = jnp.dot(p, ones, preferred_element_type=_f32)
        o = o / (den[:, 0:hd] + jnp.exp2(sink - m))
        for h in range(A_GROUP):
            c0 = (g * A_GROUP + h) * hd
            o_ref[:, c0:c0 + hd] = o[h * A_BLOCK:(h + 1) * A_BLOCK, :].astype(o_ref.dtype)


def _attention(q, k, v, sinks, n, l):
    blk = A_BLOCK
    nblk = n // blk
    off = l // blk
    rows = A_GROUP * blk
    r = np.arange(rows)[:, None] % blk
    j = np.arange(blk)[None, :]
    keep_prev = (j - blk - r) >= -WINDOW
    keep_next = (j + blk - r) <= WINDOW
    none = np.zeros_like(keep_prev)
    variants = [(none, keep_next), (keep_prev, keep_next), (keep_prev, none)]
    bias = jnp.asarray(np.where(np.array(variants), 0.0, NEG).astype(np.float32))
    sink_col = jnp.repeat(sinks.astype(_f32).reshape(A_KV_HEADS, A_GROUP) * LOG2E, blk, axis=1)
    sink_col = sink_col.reshape(A_KV_HEADS, rows, 1)

    def bias_idx(i):
        return (jnp.where(i == 0, 0, jnp.where(i == nblk - 1, 2, 1)), 0, 0, 0)

    kv_spec = lambda f: pl.BlockSpec((blk, A_KV_WIDTH), f)
    prev = lambda i: (jnp.maximum(i - 1, 0) + off, 0)
    cur = lambda i: (i + off, 0)
    nxt = lambda i: (jnp.minimum(i + 1, nblk - 1) + off, 0)
    return pl.pallas_call(
        _attn_kernel,
        out_shape=jax.ShapeDtypeStruct((n, A_WIDTH), _bf16),
        grid=(nblk,),
        in_specs=[pl.BlockSpec((blk, A_WIDTH), cur),
                  kv_spec(prev), kv_spec(cur), kv_spec(nxt),
                  kv_spec(prev), kv_spec(cur), kv_spec(nxt),
                  pl.BlockSpec((l, A_KV_WIDTH), lambda i: (0, 0)),
                  pl.BlockSpec((l, A_KV_WIDTH), lambda i: (0, 0)),
                  pl.BlockSpec((1, 2, rows, blk), bias_idx),
                  pl.BlockSpec((A_KV_HEADS, rows, 1), lambda i: (0, 0, 0))],
        out_specs=pl.BlockSpec((blk, A_WIDTH), lambda i: (i, 0)),
        compiler_params=_cparams(("arbitrary",)),
        name="window_attention",
    )(q, k, k, k, v, v, v, k, v, bias, sink_col)


def _split3(x):
    hi = x.astype(_bf16)
    r1 = x - hi.astype(_f32)
    mid = r1.astype(_bf16)
    lo = (r1 - mid.astype(_f32)).astype(_bf16)
    return hi, mid, lo


def _hgrn_intra_scores(q, k, b, emat, reverse):
    c = HG_CHUNK
    nsb = c // SUB
    nt = (((1,), (1,)), ((), ()))
    qb = (q * jnp.exp2(b)).astype(_bf16)
    b_tot = b[0:1, :] if reverse else b[c - 1:c, :]
    kend = (k * jnp.exp2(b_tot - b)).astype(_bf16)

    zero_row = jnp.zeros((1, LANES), _f32)
    zero_blk = jnp.zeros((SUB, LANES), _f32)
    y_rows, z_rows, w_cols = [], [], []
    for ib in range(nsb):
        r0 = ib * SUB
        bi = b[r0:r0 + SUB]
        qi = q[r0:r0 + SUB]
        pieces = []
        for s in range(SUB):
            bs = b[r0 + s:r0 + s + 1]
            ks = k[r0 + s:r0 + s + 1]
            pieces.append(qi * ks * jnp.exp2(jnp.minimum(bi - bs, 0.0)))
        y_rows.append(jnp.concatenate(pieces, axis=1))
        if reverse:
            lo_r, hi_r = r0 + SUB, c
            ref_b = b[r0 + SUB:r0 + SUB + 1] if ib < nsb - 1 else zero_row
        else:
            lo_r, hi_r = 0, r0
            ref_b = b[r0 - 1:r0] if ib > 0 else zero_row
        if hi_r > lo_r:
            qp = qi * jnp.exp2(bi - ref_b)
            kp = [k[lo_r:hi_r] * jnp.exp2(ref_b - b[lo_r:hi_r])]
            if lo_r > 0:
                kp.insert(0, jnp.zeros((lo_r, LANES), _f32))
            if hi_r < c:
                kp.append(jnp.zeros((c - hi_r, LANES), _f32))
            w_cols.append(jnp.concatenate(kp, axis=0) if len(kp) > 1 else kp[0])
        else:
            qp = zero_blk
            w_cols.append(jnp.zeros((c, LANES), _f32))
        z_rows.append(jnp.concatenate([qp if jb == ib else zero_blk for jb in range(nsb)], axis=1))
    y = jnp.concatenate(y_rows, axis=0).astype(_bf16)
    z = jnp.concatenate(z_rows, axis=0).astype(_bf16)
    w = jnp.concatenate(w_cols, axis=1).astype(_bf16)
    a_diag = jnp.dot(y, emat, preferred_element_type=_f32)
    a_off = lax.dot_general(z, w, nt, preferred_element_type=_f32)
    return a_diag, a_off, qb, kend, jnp.exp2(b_tot)


def _hgrn_intra_kernel(q_ref, k_ref, lf_ref, emat_ref, a_ref, qb_ref, ke_ref, dt_ref, *, reverse):
    c = HG_CHUNK
    nch = HG_BLOCK // c
    nh = q_ref.shape[0]
    ri = lax.broadcasted_iota(jnp.int32, (c, c), 0)
    ci = lax.broadcasted_iota(jnp.int32, (c, c), 1)
    tri = jnp.where((ci >= ri) if reverse else (ci <= ri), 1.0, 0.0).astype(_bf16)
    dmask = (ci // SUB == ri // SUB) & ((ci >= ri) if reverse else (ci <= ri))
    emat = emat_ref[...]

    def body(cc, carry):
        rows = pl.ds(pl.multiple_of(cc * c, c), c)
        lf_all = jnp.concatenate([lf_ref[0, h, rows, :] for h in range(nh)], axis=1)
        hi, mid, lo = _split3(lf_all)
        b_all = (jnp.dot(tri, hi, preferred_element_type=_f32)
                 + jnp.dot(tri, mid, preferred_element_type=_f32)
                 + jnp.dot(tri, lo, preferred_element_type=_f32))
        for h in range(nh):
            a_diag, a_off, qb, kend, dtot = _hgrn_intra_scores(
                q_ref[h, rows, :].astype(_f32), k_ref[0, h, rows, :].astype(_f32),
                b_all[:, h * LANES:(h + 1) * LANES], emat, reverse)
            qb_ref[h, rows, :] = qb
            ke_ref[h, rows, :] = kend
            dt_ref[cc, h:h + 1, :] = dtot
            a = a_off + jnp.where(dmask, a_diag[:, 0:c], 0.0)
            a_ref[h, rows, :] = jnp.concatenate([a, jnp.zeros_like(a)], axis=1).astype(a_ref.dtype)
        return carry

    lax.fori_loop(0, nch, body, 0)


def _hgrn_intra(rq, kk, lf, direction):
    nh, t, _ = rq.shape
    br = HG_BLOCK
    nch = br // HG_CHUNK
    reverse = direction == 1
    spec3 = pl.BlockSpec((nh, br, LANES), lambda s: (0, s, 0))
    spec4 = pl.BlockSpec((1, nh, br, LANES), lambda s: (direction, 0, s, 0))
    sds = jax.ShapeDtypeStruct((nh, t, LANES), _bf16)
    emat = (np.arange(SUB * LANES)[:, None] // LANES == np.arange(LANES)[None, :] % SUB)
    emat = jnp.asarray(emat.astype(np.float32), dtype=_bf16)
    return pl.pallas_call(
        functools.partial(_hgrn_intra_kernel, reverse=reverse),
        out_shape=(sds, sds, sds, jax.ShapeDtypeStruct((t // HG_CHUNK, nh, LANES), _f32)),
        grid=(t // br,),
        in_specs=[spec3, spec4, spec4, pl.BlockSpec(emat.shape, lambda s: (0, 0))],
        out_specs=(spec3, spec3, spec3, pl.BlockSpec((nch, nh, LANES), lambda s: (s, 0, 0))),
        compiler_params=_cparams(("arbitrary",)),
        name="hgrn2_intra_" + ("bwd" if reverse else "fwd"),
    )(rq, kk, lf, emat)


def _hgrn_state_kernel(af_ref, qf_ref, kf_ref, df_ref, vf_ref, ab_ref, qbk_ref, kb_ref, db_ref, vb_ref,
                       outf_ref, outb_ref, st_ref):
    c = HG_CHUNK
    nch = HG_BLOCK // c
    nh = qf_ref.shape[0]
    nt = (((1,), (1,)), ((), ()))
    tn = (((0,), (0,)), ((), ()))

    @pl.when(pl.program_id(0) == 0)
    def _():
        st_ref[...] = jnp.zeros(st_ref.shape, st_ref.dtype)

    dirs = ((0, False, af_ref, qf_ref, kf_ref, df_ref, vf_ref, outf_ref),
            (1, True, ab_ref, qbk_ref, kb_ref, db_ref, vb_ref, outb_ref))

    def body(cc, carry):
        for di, reverse, a_ref, q_ref, k_ref, d_ref, v_ref, out_ref in dirs:
            chunk = (nch - 1 - cc) if reverse else cc
            rows = pl.ds(pl.multiple_of(chunk * c, c), c)
            for h in range(nh):
                st = st_ref[di, h]
                v = v_ref[h, rows, :]
                o = (jnp.dot(a_ref[h, rows, 0:c], v, preferred_element_type=_f32)
                     + lax.dot_general(q_ref[h, rows, :], st.astype(_bf16), nt, preferred_element_type=_f32))
                out_ref[h, rows, :] = o.astype(out_ref.dtype)
                upd = lax.dot_general(v, k_ref[h, rows, :], tn, preferred_element_type=_f32)
                st_ref[di, h] = st * d_ref[chunk, pl.ds(h, 1), :] + upd
        return carry

    lax.fori_loop(0, nch, body, 0)


def _hgrn_state(intra_f, intra_b, rv, l):
    a_f, qb_f, ke_f, dt_f = intra_f
    a_b, qb_b, ke_b, dt_b = intra_b
    nh, t, _ = rv.shape
    br = HG_BLOCK
    nch = br // HG_CHUNK
    nb = t // br
    nc = l // br

    def blk_b(s):
        return jnp.where(s < nc, nc - 1 - s, nb - 1 - (s - nc))

    def specs(blk):
        s3 = pl.BlockSpec((nh, br, LANES), lambda s: (0, blk(s), 0))
        sd = pl.BlockSpec((nch, nh, LANES), lambda s: (blk(s), 0, 0))
        return [s3, s3, s3, sd, s3]

    fwd = specs(lambda s: s)
    bwd = specs(blk_b)
    return pl.pallas_call(
        _hgrn_state_kernel,
        out_shape=(jax.ShapeDtypeStruct((nh, t, LANES), _bf16),
                   jax.ShapeDtypeStruct((nh, t, LANES), _bf16)),
        grid=(nb,),
        in_specs=fwd + bwd,
        out_specs=(fwd[0], bwd[0]),
        scratch_shapes=[pltpu.VMEM((2, nh, B_DV, B_DK), _f32)],
        compiler_params=_cparams(("arbitrary",)),
        name="hgrn2_state",
    )(a_f, qb_f, ke_f, dt_f, rv, a_b, qb_b, ke_b, dt_b, rv)


def _merge_kernel(ya_ref, of_ref, ob_ref, rg_ref, ga_ref, gb_ref, x_ref, wa_ref, wb_ref, wo_ref,
                  wr_ref, br_ref, hg_ref, g2_ref, mod_ref,
                  x1_ref, h2_ref, route_ref, cnt_ref, run_ref):
    i = pl.program_id(0)
    tm, d = x_ref.shape

    @pl.when(i == 0)
    def _():
        run_ref[...] = jnp.zeros(run_ref.shape, run_ref.dtype)

    parts = []
    for h in range(B_HEADS):
        o = of_ref[h].astype(_f32) + ob_ref[h].astype(_f32)
        o = o * lax.rsqrt(jnp.mean(o * o, axis=-1, keepdims=True) + EPS)
        o = o * hg_ref[:, h * LANES:(h + 1) * LANES]
        parts.append((o * rg_ref[h].astype(_f32)).astype(_bf16))
    yb = jnp.concatenate(parts, axis=1)
    z0 = jnp.dot(ya_ref[...], wa_ref[...], preferred_element_type=_f32)
    z1 = jnp.dot(yb, wb_ref[...], preferred_element_type=_f32)
    merged = ga_ref[...].astype(_f32) * z0 + gb_ref[...].astype(_f32) * z1
    y = jnp.dot(merged.astype(_bf16), wo_ref[...], preferred_element_type=_f32)
    x1 = x_ref[...] + mod_ref[0:1, 2 * d:3 * d] * y
    x1_ref[...] = x1
    h2 = _rms_mod(x1, g2_ref[...], mod_ref[0:1, 3 * d:4 * d], mod_ref[0:1, 4 * d:5 * d])
    h2_ref[...] = _pack_bf16_pair(h2[:, 0:d // 2], h2[:, d // 2:d])

    hh = h2.astype(_bf16)
    hl = (h2 - hh.astype(_f32)).astype(_bf16)
    both = jnp.dot(hh, wr_ref[...], preferred_element_type=_f32)
    logits = (both[:, 0:LANES] + both[:, LANES:2 * LANES]
              + jnp.dot(hl, wr_ref[:, 0:LANES], preferred_element_type=_f32)) + br_ref[...]
    lane = lax.broadcasted_iota(jnp.int32, (tm, LANES), 1)
    work = jnp.where(lane < N_EXPERTS, logits, NEG)
    vals, idxs = [], []
    onehot = jnp.zeros((tm, LANES), _f32)
    for _ in range(TOP_K):
        m = jnp.max(work, axis=-1, keepdims=True)
        idx = jnp.min(jnp.where(work == m, lane, LANES), axis=-1, keepdims=True)
        sel = lane == idx
        vals.append(m)
        idxs.append(idx)
        onehot = jnp.where(sel, 1.0, onehot)
        work = jnp.where(sel, NEG, work)
    es = [jnp.exp(vv - vals[0]) for vv in vals]
    tot = es[0] + es[1] + es[2] + es[3]
    ri = lax.broadcasted_iota(jnp.int32, (tm, tm), 0)
    ci = lax.broadcasted_iota(jnp.int32, (tm, tm), 1)
    ltri = jnp.where(ci < ri, 1.0, 0.0).astype(_bf16)
    prefix = jnp.dot(ltri, onehot.astype(_bf16), preferred_element_type=_f32) + run_ref[...]
    route = jnp.zeros((tm, LANES), _f32)
    for j in range(TOP_K):
        rank = jnp.sum(jnp.where(lane == idxs[j], prefix, 0.0), axis=-1, keepdims=True)
        route = jnp.where(lane == j, idxs[j].astype(_f32), route)
        route = jnp.where(lane == TOP_K + j, es[j] / tot, route)
        route = jnp.where(lane == 2 * TOP_K + j, rank, route)
    route_ref[...] = route
    run_new = run_ref[...] + jnp.sum(onehot, axis=0, keepdims=True)
    run_ref[...] = run_new
    cnt_ref[...] = run_new


def _merge_route(ya, o_f, o_b, rg, gates, x2, wa, wb, wo, wr_cat, b_router, hg, g2, mod, l):
    n, d = x2.shape
    tm = 256
    off = l // tm
    const = lambda shape: pl.BlockSpec(shape, lambda i: (0,) * len(shape),
                                       pipeline_mode=pl.Buffered(1))
    head_spec = pl.BlockSpec((B_HEADS, tm, LANES), lambda i: (0, i + off, 0))
    return pl.pallas_call(
        _merge_kernel,
        out_shape=(jax.ShapeDtypeStruct((n, d), _f32),
                   jax.ShapeDtypeStruct((n, d // 2), jnp.uint32),
                   jax.ShapeDtypeStruct((n, LANES), _f32),
                   jax.ShapeDtypeStruct((1, LANES), _f32)),
        grid=(n // tm,),
        in_specs=[pl.BlockSpec((tm, A_WIDTH), lambda i: (i, 0)),
                  head_spec, head_spec, head_spec,
                  pl.BlockSpec((tm, d), lambda i: (i + off, 0)),
                  pl.BlockSpec((tm, d), lambda i: (i + off, 1)),
                  pl.BlockSpec((tm, d), lambda i: (i, 0)),
                  const(wa.shape), const(wb.shape), const(wo.shape),
                  const(wr_cat.shape), const((1, LANES)),
                  const((1, B_WIDTH)), const((1, d)), const(mod.shape)],
        out_specs=(pl.BlockSpec((tm, d), lambda i: (i, 0)),
                   pl.BlockSpec((tm, d // 2), lambda i: (i, 0)),
                   pl.BlockSpec((tm, LANES), lambda i: (i, 0)),
                   pl.BlockSpec((1, LANES), lambda i: (0, 0))),
        scratch_shapes=[pltpu.VMEM((1, LANES), _f32)],
        compiler_params=_cparams(("arbitrary",)),
        name="merge_route",
    )(ya, o_f, o_b, rg, gates, gates, x2, wa, wb, wo, wr_cat, b_router, hg, g2, mod)


def _zero_fill_tiles(zero_ref, dst_ref, first, last, sem, cols=(None,)):
    tm = MOE_TM

    def copy(t, col):
        rows = pl.ds(pl.multiple_of(t * tm, tm), tm)
        dst = dst_ref.at[rows] if col is None else dst_ref.at[rows, col]
        return pltpu.make_async_copy(zero_ref, dst, sem)

    def start(t, carry):
        for col in cols:
            copy(t, col).start()
        return carry

    def wait(t, carry):
        for col in cols:
            copy(t, col).wait()
        return carry

    lax.fori_loop(first, last, start, 0)
    lax.fori_loop(first, last, wait, 0)


def _dispatch_kernel(lt_ref, nt_ref, dest_ref, h2_ref, xs_ref, zero_ref, sem, zsem):
    tm = h2_ref.shape[0]

    @pl.when(pl.program_id(0) == 0)
    def _():
        zero_ref[...] = jnp.zeros(zero_ref.shape, zero_ref.dtype)
        for e in range(N_EXPERTS):
            @pl.when(nt_ref[e] > 0)
            def _():
                pltpu.make_async_copy(zero_ref, xs_ref.at[pl.ds(lt_ref[e] * MOE_TM, MOE_TM)], zsem).start()
        for e in range(N_EXPERTS):
            @pl.when(nt_ref[e] > 0)
            def _():
                pltpu.make_async_copy(zero_ref, xs_ref.at[pl.ds(lt_ref[e] * MOE_TM, MOE_TM)], zsem).wait()
        _zero_fill_tiles(zero_ref, xs_ref, lt_ref[N_EXPERTS - 1] + 1, xs_ref.shape[0] // MOE_TM, zsem)

    def issue(t, carry):
        for j in range(TOP_K):
            dst = dest_ref[t * TOP_K + j]
            pltpu.make_async_copy(h2_ref.at[pl.ds(t, 1)], xs_ref.at[pl.ds(dst, 1)], sem).start()
        return carry

    lax.fori_loop(0, tm, issue, 0)
    for _ in range(TOP_K):
        pltpu.make_async_copy(h2_ref, xs_ref.at[pl.ds(0, tm)], sem).wait()


def _dispatch(h2, dest_flat, last_tile, ntile, n_slots):
    n, d = h2.shape
    tm = 256
    return pl.pallas_call(
        _dispatch_kernel,
        out_shape=jax.ShapeDtypeStruct((n_slots, d), h2.dtype),
        grid_spec=pltpu.PrefetchScalarGridSpec(
            num_scalar_prefetch=2,
            grid=(n // tm,),
            in_specs=[pl.BlockSpec((tm * TOP_K,), lambda i, lt, nt: (i,), memory_space=pltpu.SMEM),
                      pl.BlockSpec((tm, d), lambda i, lt, nt: (i, 0))],
            out_specs=pl.BlockSpec(memory_space=pl.ANY),
            scratch_shapes=[pltpu.VMEM((MOE_TM, d), h2.dtype), pltpu.SemaphoreType.DMA,
                            pltpu.SemaphoreType.DMA]),
        compiler_params=_cparams(("arbitrary",)),
        name="moe_dispatch",
    )(last_tile, ntile, dest_flat, h2)


def _cast_rows(src_ref, dst_ref):
    rows = dst_ref.shape[0]
    step = 256

    def body(i, carry):
        r = pl.ds(pl.multiple_of(i * step, step), step)
        dst_ref[r, :] = src_ref[r, :].astype(dst_ref.dtype)
        return carry

    lax.fori_loop(0, rows // step, body, 0)


def _expert_stream(ts_ref, nt_ref, weights, in_copy, out_copy, compute):
    nb = MOE_NBUF
    n_exp = nt_ref.shape[0]
    total = ts_ref[n_exp - 1] + nt_ref[n_exp - 1]
    for k in range(nb - 1):
        @pl.when(total > k)
        def _():
            in_copy(k, k).start(priority=1)
    for fetch, _, _ in weights:
        fetch(0, 0).start()

    def expert_body(e, carry):
        wslot = lax.rem(e, 2)
        for fetch, _, _ in weights:
            fetch(e, wslot).wait()

        @pl.when(e + 1 < n_exp)
        def _():
            for fetch, _, _ in weights:
                fetch(e + 1, 1 - wslot).start()

        for _, stage_ref, bf_ref in weights:
            _cast_rows(stage_ref.at[wslot], bf_ref)
        t0 = ts_ref[e]

        def tile_body(t, c2):
            g = t0 + t
            slot = lax.rem(g, nb)
            in_copy(g, slot).wait()

            @pl.when(g + nb - 1 < total)
            def _():
                in_copy(g + nb - 1, lax.rem(g + nb - 1, nb)).start(priority=1)

            @pl.when(g >= nb)
            def _():
                out_copy(g - nb, slot).wait()

            compute(e, slot)
            out_copy(g, slot).start(priority=1)
            return c2

        lax.fori_loop(0, nt_ref[e], tile_body, 0)
        return carry

    lax.fori_loop(0, n_exp, expert_body, 0)
    for k in range(1, nb + 1):
        @pl.when(total >= k)
        def _():
            out_copy(total - k, lax.rem(total - k, nb)).wait()
    return total


def _tile_rows(g):
    return pl.ds(pl.multiple_of(g * MOE_TM, MOE_TM), MOE_TM)


def _moe_up_kernel(ts_ref, nt_ref, xs_ref, wg_ref, wu_ref, bg_ref, bu_ref, act_ref,
                   wgs_ref, wus_ref, wgb_ref, wub_ref, xbuf, obuf, sem_wg, sem_wu, sem_in, sem_out):
    c = pl.program_id(0)
    tm = MOE_TM
    fc = wgb_ref.shape[1]
    col = pl.ds(pl.multiple_of(c * fc, fc), fc)

    def in_copy(g, slot):
        return pltpu.make_async_copy(xs_ref.at[_tile_rows(g)], xbuf.at[slot], sem_in.at[slot])

    def out_copy(g, slot):
        return pltpu.make_async_copy(obuf.at[slot], act_ref.at[_tile_rows(g), col], sem_out.at[slot])

    def fetch_g(e, slot):
        return pltpu.make_async_copy(wg_ref.at[e, :, col], wgs_ref.at[slot], sem_wg.at[slot])

    def fetch_u(e, slot):
        return pltpu.make_async_copy(wu_ref.at[e, :, col], wus_ref.at[slot], sem_wu.at[slot])

    def compute(e, slot):
        x_lo, x_hi = _unpack_bf16_pair(xbuf[slot])
        x_lo, x_hi = x_lo.astype(_bf16), x_hi.astype(_bf16)
        half = x_lo.shape[1]

        def proj(w_ref, b_ref):
            return (jnp.dot(x_lo, w_ref[0:half, :], preferred_element_type=_f32)
                    + jnp.dot(x_hi, w_ref[half:2 * half, :], preferred_element_type=_f32)
                    + b_ref[pl.ds(e, 1), col])

        g = jnp.minimum(proj(wgb_ref, bg_ref), SWIGLU_LIMIT)
        u = jnp.clip(proj(wub_ref, bu_ref), -SWIGLU_LIMIT, SWIGLU_LIMIT)
        obuf[slot] = (g * _sigmoid(SWIGLU_ALPHA * g) * (u + 1.0)).astype(obuf.dtype)

    total = _expert_stream(ts_ref, nt_ref, [(fetch_g, wgs_ref, wgb_ref), (fetch_u, wus_ref, wub_ref)],
                           in_copy, out_copy, compute)
    obuf[0] = jnp.zeros(obuf.shape[1:], obuf.dtype)
    _zero_fill_tiles(obuf.at[0], act_ref, total, act_ref.shape[0] // tm, sem_out.at[0], cols=(col,))


def _moe_up(xs, wg, wu, bg, bu, tstart, ntile):
    n_slots = xs.shape[0]
    ne, d, ff = wg.shape
    fc = ff // 2
    tm = MOE_TM
    any_spec = pl.BlockSpec(memory_space=pl.ANY)
    b_spec = pl.BlockSpec((ne, ff), lambda c, ts, nt: (0, 0))
    return pl.pallas_call(
        _moe_up_kernel,
        out_shape=jax.ShapeDtypeStruct((n_slots, ff), _bf16),
        grid_spec=pltpu.PrefetchScalarGridSpec(
            num_scalar_prefetch=2,
            grid=(ff // fc,),
            in_specs=[any_spec, any_spec, any_spec, b_spec, b_spec],
            out_specs=any_spec,
            scratch_shapes=[pltpu.VMEM((2, d, fc), _f32), pltpu.VMEM((2, d, fc), _f32),
                            pltpu.VMEM((d, fc), _bf16), pltpu.VMEM((d, fc), _bf16),
                            pltpu.VMEM((MOE_NBUF, tm, xs.shape[1]), xs.dtype),
                            pltpu.VMEM((MOE_NBUF, tm, fc), _bf16),
                            pltpu.SemaphoreType.DMA((2,)), pltpu.SemaphoreType.DMA((2,)),
                            pltpu.SemaphoreType.DMA((MOE_NBUF,)), pltpu.SemaphoreType.DMA((MOE_NBUF,))]),
        compiler_params=_cparams(("arbitrary",)),
        name="moe_gate_up",
    )(tstart, ntile, xs, wg, wu, bg, bu)


def _moe_down_kernel(ts_ref, nt_ref, act_ref, wd_ref, bd_ref, ys_ref,
                     wds_ref, wdb_ref, abuf, ybuf, sem_w, sem_in, sem_out):
    tm = MOE_TM
    d = wdb_ref.shape[1]

    def in_copy(g, slot):
        return pltpu.make_async_copy(act_ref.at[_tile_rows(g)], abuf.at[slot], sem_in.at[slot])

    def out_copy(g, slot):
        return pltpu.make_async_copy(ybuf.at[slot], ys_ref.at[_tile_rows(g)], sem_out.at[slot])

    def fetch(e, slot):
        return pltpu.make_async_copy(wd_ref.at[e], wds_ref.at[slot], sem_w.at[slot])

    def compute(e, slot):
        y = jnp.dot(abuf[slot], wdb_ref[...], preferred_element_type=_f32) + bd_ref[pl.ds(e, 1), :]
        ybuf[slot] = _pack_bf16_pair(y[:, 0:d // 2], y[:, d // 2:d])

    total = _expert_stream(ts_ref, nt_ref, [(fetch, wds_ref, wdb_ref)], in_copy, out_copy, compute)
    ybuf[0] = jnp.zeros(ybuf.shape[1:], ybuf.dtype)
    _zero_fill_tiles(ybuf.at[0], ys_ref, total, ys_ref.shape[0] // tm, sem_out.at[0])


def _moe_down(act, wd, bd, tstart, ntile):
    n_slots, ff = act.shape
    ne, _, d = wd.shape
    tm = MOE_TM
    any_spec = pl.BlockSpec(memory_space=pl.ANY)
    return pl.pallas_call(
        _moe_down_kernel,
        out_shape=jax.ShapeDtypeStruct((n_slots, d // 2), jnp.uint32),
        grid_spec=pltpu.PrefetchScalarGridSpec(
            num_scalar_prefetch=2,
            grid=(1,),
            in_specs=[any_spec, any_spec, pl.BlockSpec((ne, d), lambda i, ts, nt: (0, 0))],
            out_specs=any_spec,
            scratch_shapes=[pltpu.VMEM((2, ff, d), _f32), pltpu.VMEM((ff, d), _bf16),
                            pltpu.VMEM((MOE_NBUF, tm, ff), _bf16), pltpu.VMEM((MOE_NBUF, tm, d // 2), jnp.uint32),
                            pltpu.SemaphoreType.DMA((2,)),
                            pltpu.SemaphoreType.DMA((MOE_NBUF,)), pltpu.SemaphoreType.DMA((MOE_NBUF,))]),
        compiler_params=_cparams(("arbitrary",)),
        name="moe_down",
    )(tstart, ntile, act, wd, bd)


def _combine_kernel(dcur_ref, dnxt_ref, ys_ref, route_ref, x1_ref, mod_ref, gf_ref, o_ref, buf_ref, sem):
    tm, d = x1_ref.shape
    i = pl.program_id(0)
    slot = lax.rem(i, 2)

    def gather(dest_ref, s):
        def issue(t, carry):
            for j in range(TOP_K):
                src = dest_ref[t * TOP_K + j]
                pltpu.make_async_copy(ys_ref.at[pl.ds(src, 1)], buf_ref.at[s, j, pl.ds(t, 1)], sem.at[s]).start()
            return carry

        lax.fori_loop(0, tm, issue, 0)

    @pl.when(i == 0)
    def _():
        gather(dcur_ref, slot)

    @pl.when(i + 1 < pl.num_programs(0))
    def _():
        gather(dnxt_ref, 1 - slot)

    for j in range(TOP_K):
        pltpu.make_async_copy(ys_ref.at[pl.ds(0, tm)], buf_ref.at[slot, j], sem.at[slot]).wait()
    acc_lo = jnp.zeros((tm, d // 2), _f32)
    acc_hi = jnp.zeros((tm, d // 2), _f32)
    for j in range(TOP_K):
        gate = route_ref[:, TOP_K + j:TOP_K + j + 1]
        y_lo, y_hi = _unpack_bf16_pair(buf_ref[slot, j])
        acc_lo = acc_lo + gate * y_lo
        acc_hi = acc_hi + gate * y_hi
    x2 = x1_ref[...] + mod_ref[0:1, 5 * d:6 * d] * jnp.concatenate([acc_lo, acc_hi], axis=1)
    y = x2 * lax.rsqrt(jnp.mean(x2 * x2, axis=-1, keepdims=True) + EPS)
    o_ref[...] = y * gf_ref[...]


def _combine(ys, dest_flat, route, x1, mod, gf):
    n, d = x1.shape
    tm = 256
    nsteps = n // tm
    dest_spec = lambda f: pl.BlockSpec((tm * TOP_K,), f, memory_space=pltpu.SMEM)
    return pl.pallas_call(
        _combine_kernel,
        out_shape=jax.ShapeDtypeStruct((n, d), _f32),
        grid=(nsteps,),
        in_specs=[dest_spec(lambda i: (i,)), dest_spec(lambda i: (jnp.minimum(i + 1, nsteps - 1),)),
                  pl.BlockSpec(memory_space=pl.ANY),
                  pl.BlockSpec((tm, LANES), lambda i: (i, 0)),
                  pl.BlockSpec((tm, d), lambda i: (i, 0)),
                  pl.BlockSpec(mod.shape, lambda i: (0, 0)),
                  pl.BlockSpec((1, d), lambda i: (0, 0))],
        out_specs=pl.BlockSpec((tm, d), lambda i: (i, 0)),
        scratch_shapes=[pltpu.VMEM((2, TOP_K, tm, d // 2), jnp.uint32), pltpu.SemaphoreType.DMA((2,))],
        compiler_params=_cparams(("arbitrary",)),
        name="moe_combine",
    )(dest_flat, dest_flat, ys, route, x1, mod, gf.reshape(1, d))


def _rope_tables(n, l):
    rows = n // GRID_W
    inv_freq = ROPE_BASE ** (-jnp.arange(ROPE_PAIRS_PER_AXIS, dtype=_f32) / ROPE_PAIRS_PER_AXIS)
    ang_r = jnp.arange(rows, dtype=_f32)[:, None] * inv_freq
    ang_c = jnp.arange(GRID_W, dtype=_f32)[:, None] * inv_freq
    rep = lambda t_r, t_c: jnp.concatenate(
        [jnp.repeat(t_r, GRID_W, axis=0), jnp.tile(t_c, (rows, 1))], axis=-1)
    cos, sin = rep(jnp.cos(ang_r), jnp.cos(ang_c)), rep(jnp.sin(ang_r), jnp.sin(ang_c))
    cos_t = jnp.tile(cos, (1, LANES // 32))
    sin_t = jnp.tile(jnp.concatenate([-sin, sin], axis=-1), (1, LANES // A_HEAD_DIM))
    cos_t = jnp.concatenate([jnp.ones((l, LANES), _f32), cos_t], axis=0)
    sin_t = jnp.concatenate([jnp.zeros((l, LANES), _f32), sin_t], axis=0)
    return cos_t, sin_t


def _routing_tables(route, counts):
    tm = MOE_TM
    idx = route[:, 0:TOP_K].astype(jnp.int32)
    rank = route[:, 2 * TOP_K:3 * TOP_K].astype(jnp.int32)
    cnt = counts[0, :N_EXPERTS].astype(jnp.int32)
    ntile = (cnt + tm - 1) // tm
    tstart = jnp.cumsum(ntile) - ntile
    slot0 = tstart * tm
    onehot = idx[:, :, None] == jnp.arange(N_EXPERTS, dtype=jnp.int32)
    dest = jnp.sum(jnp.where(onehot, slot0, 0), axis=-1) + rank
    return dest.reshape(-1), tstart, ntile


def kernel(x, c, ctx, c_ctx, norm_mix_g, norm_ffn_g, w_mod, b_mod, w_in, attn_sinks,
           lb_fwd_logits, lb_bwd_logits, hgrn_norm_g, w_branch, w_out, w_router, b_router,
           w_e_gate, b_e_gate, w_e_up, b_e_up, w_e_down, b_e_down, final_norm_g):
    bsz, n, d = x.shape
    l = ctx.shape[1]
    assert bsz == 1 and d == D_MODEL and norm_mix_g.shape[0] == 1
    assert n % HG_BLOCK == 0 and l % HG_BLOCK == 0 and n // A_BLOCK >= 2
    x2 = x.reshape(n, d)
    ctx2 = ctx.reshape(l, d)

    mod = _mod_vectors(c, c_ctx, w_mod[0], b_mod[0])
    h_all = _norm_all(x2, ctx2, norm_mix_g[0], mod)

    w = w_in[0].astype(_bf16)
    c0 = 0
    segs = []
    for width in (A_WIDTH + 2 * A_KV_WIDTH, B_WIDTH, B_WIDTH, B_WIDTH, B_WIDTH, B_WIDTH, 2 * d):
        segs.append(w[:, c0:c0 + width])
        c0 += width
    w_qkv, w_rq, w_zf, w_zb, w_rv, w_rg, w_gates = segs
    cos_t, sin_t = _rope_tables(n, l)
    q, k, v = _proj_qkv(h_all, w_qkv, cos_t, sin_t)
    rq = _proj_heads(h_all, w_rq, "silu")
    rv = _proj_heads(h_all, w_rv, "none")
    rg = _proj_heads(h_all, w_rg, "silu")
    lb_f = jax.nn.softmax(lb_fwd_logits.astype(_f32), axis=0)[0]
    lb_b = jax.nn.softmax(lb_bwd_logits.astype(_f32), axis=0)[0]
    lb2 = jnp.stack([lb_f, lb_b]).reshape(2, 1, B_WIDTH)
    lf, kk = _proj_gate(h_all, jnp.stack([w_zf, w_zb]), lb2)
    gates = _proj_sigmoid(h_all, w_gates)

    ya = _attention(q, k, v, attn_sinks[0], n, l)
    o_f, o_b = _hgrn_state(_hgrn_intra(rq, kk, lf, 0), _hgrn_intra(rq, kk, lf, 1), rv, l)

    wr = jnp.zeros((d, LANES), _f32).at[:, :N_EXPERTS].set(w_router[0].astype(_f32))
    wr_hi = wr.astype(_bf16)
    wr_lo = (wr - wr_hi.astype(_f32)).astype(_bf16)
    br = jnp.zeros((1, LANES), _f32).at[0, :N_EXPERTS].set(b_router[0].astype(_f32))
    x1, h2, route, counts = _merge_route(
        ya, o_f, o_b, rg, gates, x2,
        w_branch[0, 0].astype(_bf16), w_branch[0, 1].astype(_bf16), w_out[0].astype(_bf16),
        jnp.concatenate([wr_hi, wr_lo], axis=1), br, hgrn_norm_g[0].reshape(1, B_WIDTH).astype(_f32),
        norm_ffn_g[0].reshape(1, d), mod, l)

    n_tiles = (n * TOP_K) // MOE_TM + N_EXPERTS
    dest, tstart, ntile = _routing_tables(route, counts)
    xs = _dispatch(h2, dest, tstart + ntile - 1, ntile, n_tiles * MOE_TM)
    act = _moe_up(xs, w_e_gate[0], w_e_up[0], b_e_gate[0], b_e_up[0], tstart, ntile)
    ys = _moe_down(act, w_e_down[0], b_e_down[0], tstart, ntile)
    out = _combine(ys, dest, route, x1, mod, final_norm_g)
    return out.reshape(bsz, n, d)
```

```python
import functools

import jax
import jax.numpy as jnp
import numpy as np
from jax import lax
from jax.experimental import pallas as pl
from jax.experimental.pallas import tpu as pltpu

D_MODEL = 2048
GRID_W = 64
EPS = 1e-6
A_HEADS = 16
A_KV_HEADS = 2
A_GROUP = A_HEADS // A_KV_HEADS
A_HEAD_DIM = 64
A_WIDTH = A_HEADS * A_HEAD_DIM
A_KV_WIDTH = A_KV_HEADS * A_HEAD_DIM
WINDOW = 128
A_BLOCK = 128
ROPE_BASE = 10000.0
ROPE_PAIRS_PER_AXIS = A_HEAD_DIM // 4
B_HEADS = 8
B_DK = 128
B_DV = 128
B_WIDTH = B_HEADS * B_DV
N_EXPERTS = 32
TOP_K = 4
EXPERT_FF = 2048
SWIGLU_LIMIT = 7.0
SWIGLU_ALPHA = 1.702

LANES = 128
SUB = 8
HG_CHUNK = 64
HG_BLOCK = 256
MOE_TM = 256
MOE_NBUF = 5
NEG = -1e30
LOG2E = 1.4426950408889634
VMEM_LIMIT = 56 * 1024 * 1024

_f32 = jnp.float32
_bf16 = jnp.bfloat16


def _cparams(sem):
    return pltpu.CompilerParams(dimension_semantics=sem, vmem_limit_bytes=VMEM_LIMIT)


def _sigmoid(x):
    return 1.0 / (1.0 + jnp.exp(-x))


def _pack_bf16_pair(lo, hi):
    lo_bits = lax.bitcast_convert_type(lo.astype(_bf16).astype(_f32), jnp.uint32)
    hi_bits = lax.bitcast_convert_type(hi.astype(_bf16).astype(_f32), jnp.uint32)
    return (lo_bits >> 16) | (hi_bits & jnp.uint32(0xFFFF0000))


def _unpack_bf16_pair(w):
    lo = lax.bitcast_convert_type(w << 16, _f32)
    hi = lax.bitcast_convert_type(w & jnp.uint32(0xFFFF0000), _f32)
    return lo, hi


def _mod_kernel(s_ref, w_ref, b_ref, o_ref):
    tn = w_ref.shape[1]
    for r in range(2):
        s = s_ref[r]
        s = s * _sigmoid(s)
        for j in range(tn // LANES):
            sl = slice(j * LANES, (j + 1) * LANES)
            acc = jnp.sum(w_ref[:, sl] * s, axis=0, keepdims=True)
            o_ref[r:r + 1, sl] = acc + b_ref[:, sl]


def _mod_vectors(c, c_ctx, w_mod, b_mod):
    d, n_out = w_mod.shape
    tn = 1024
    s = jnp.stack([c.reshape(d), c_ctx.reshape(d)]).astype(_f32)
    s = jnp.broadcast_to(s[:, :, None], (2, d, LANES))
    return pl.pallas_call(
        _mod_kernel,
        out_shape=jax.ShapeDtypeStruct((2, n_out), _f32),
        grid=(n_out // tn,),
        in_specs=[pl.BlockSpec((2, d, LANES), lambda j: (0, 0, 0)),
                  pl.BlockSpec((d, tn), lambda j: (0, j)),
                  pl.BlockSpec((1, tn), lambda j: (0, j))],
        out_specs=pl.BlockSpec((2, tn), lambda j: (0, j)),
        compiler_params=_cparams(("arbitrary",)),
        name="mod_vectors",
    )(s, w_mod, b_mod.reshape(1, n_out))


def _rms_mod(xf, g, shift, scale):
    y = xf * lax.rsqrt(jnp.mean(xf * xf, axis=-1, keepdims=True) + EPS)
    return (y * g) * (1.0 + scale) + shift


def _norm_kernel(x_ref, ctx_ref, g_ref, mod_ref, o_ref, *, n_ctx_tiles):
    i = pl.program_id(0)
    d = x_ref.shape[1]

    @pl.when(i < n_ctx_tiles)
    def _():
        o_ref[...] = _rms_mod(ctx_ref[...], g_ref[...], mod_ref[1:2, 0:d],
                              mod_ref[1:2, d:2 * d]).astype(o_ref.dtype)

    @pl.when(i >= n_ctx_tiles)
    def _():
        o_ref[...] = _rms_mod(x_ref[...], g_ref[...], mod_ref[0:1, 0:d],
                              mod_ref[0:1, d:2 * d]).astype(o_ref.dtype)


def _norm_all(x2, ctx2, g, mod):
    n, d = x2.shape
    l = ctx2.shape[0]
    tm = 256
    nct = l // tm
    return pl.pallas_call(
        functools.partial(_norm_kernel, n_ctx_tiles=nct),
        out_shape=jax.ShapeDtypeStruct((l + n, d), _bf16),
        grid=((l + n) // tm,),
        in_specs=[pl.BlockSpec((tm, d), lambda i: (jnp.maximum(i - nct, 0), 0)),
                  pl.BlockSpec((tm, d), lambda i: (jnp.minimum(i, nct - 1), 0)),
                  pl.BlockSpec((1, d), lambda i: (0, 0)),
                  pl.BlockSpec(mod.shape, lambda i: (0, 0))],
        out_specs=pl.BlockSpec((tm, d), lambda i: (i, 0)),
        compiler_params=_cparams(("arbitrary",)),
        name="adaln_norm",
    )(x2, ctx2, g.reshape(1, d), mod)


def _rope(a, cos, sin):
    lane = lax.broadcasted_iota(jnp.int32, a.shape, 1)
    first = (lane % A_HEAD_DIM) < (A_HEAD_DIM // 2)
    rot = jnp.where(first, pltpu.roll(a, LANES - 32, 1), pltpu.roll(a, 32, 1))
    return a * cos + rot * sin


def _proj_qkv_kernel(h_ref, w_ref, cos_ref, sin_ref, q_ref, k_ref, v_ref):
    acc = jnp.dot(h_ref[...], w_ref[...], preferred_element_type=_f32)
    cos = cos_ref[...]
    sin = sin_ref[...]
    scale = A_HEAD_DIM ** -0.5 * LOG2E
    for j in range(A_WIDTH // LANES):
        sl = slice(j * LANES, (j + 1) * LANES)
        q_ref[:, sl] = (_rope(acc[:, sl], cos, sin) * scale).astype(q_ref.dtype)
    k_ref[...] = _rope(acc[:, A_WIDTH:A_WIDTH + LANES], cos, sin).astype(k_ref.dtype)
    v_ref[...] = acc[:, A_WIDTH + LANES:A_WIDTH + 2 * LANES].astype(v_ref.dtype)


def _proj_heads_kernel(h_ref, w_ref, o_ref, *, act):
    acc = jnp.dot(h_ref[...], w_ref[...], preferred_element_type=_f32)
    if act == "silu":
        acc = acc * _sigmoid(acc)
    for hh in range(o_ref.shape[0]):
        o_ref[hh] = acc[:, hh * LANES:(hh + 1) * LANES].astype(o_ref.dtype)


def _proj_gate_kernel(h_ref, w_ref, lb_ref, lf_ref, kk_ref):
    z = jnp.dot(h_ref[...], w_ref[0], preferred_element_type=_f32)
    lb = lb_ref[0]
    sg = _sigmoid(z)
    logf = jnp.log2(lb + (1.0 - lb) * sg)
    kk = (1.0 - lb) * _sigmoid(-z)
    for hh in range(lf_ref.shape[1]):
        sl = slice(hh * LANES, (hh + 1) * LANES)
        lf_ref[0, hh] = logf[:, sl]
        kk_ref[0, hh] = kk[:, sl].astype(kk_ref.dtype)


def _proj_sigmoid_kernel(h_ref, w_ref, o_ref):
    acc = jnp.dot(h_ref[...], w_ref[...], preferred_element_type=_f32)
    o_ref[...] = _sigmoid(acc).astype(o_ref.dtype)


def _row_tile(t):
    for tm in (1280, 640, 256, 128):
        if t % tm == 0:
            return tm
    raise ValueError(t)


def _proj_qkv(h, w, cos_t, sin_t):
    t, d = h.shape
    tm = _row_tile(t)
    ncol = w.shape[1]
    return pl.pallas_call(
        _proj_qkv_kernel,
        out_shape=(jax.ShapeDtypeStruct((t, A_WIDTH), _bf16),
                   jax.ShapeDtypeStruct((t, A_KV_WIDTH), _bf16),
                   jax.ShapeDtypeStruct((t, A_KV_WIDTH), _bf16)),
        grid=(t // tm,),
        in_specs=[pl.BlockSpec((tm, d), lambda i: (i, 0)),
                  pl.BlockSpec((d, ncol), lambda i: (0, 0)),
                  pl.BlockSpec((tm, LANES), lambda i: (i, 0)),
                  pl.BlockSpec((tm, LANES), lambda i: (i, 0))],
        out_specs=(pl.BlockSpec((tm, A_WIDTH), lambda i: (i, 0)),
                   pl.BlockSpec((tm, A_KV_WIDTH), lambda i: (i, 0)),
                   pl.BlockSpec((tm, A_KV_WIDTH), lambda i: (i, 0))),
        compiler_params=_cparams(("arbitrary",)),
        name="proj_qkv",
    )(h, w, cos_t, sin_t)


def _proj_heads(h, w, act):
    t, d = h.shape
    tm = _row_tile(t)
    tn = 1024
    nh = tn // LANES
    return pl.pallas_call(
        functools.partial(_proj_heads_kernel, act=act),
        out_shape=jax.ShapeDtypeStruct((w.shape[1] // LANES, t, LANES), _bf16),
        grid=(w.shape[1] // tn, t // tm),
        in_specs=[pl.BlockSpec((tm, d), lambda j, i: (i, 0)),
                  pl.BlockSpec((d, tn), lambda j, i: (0, j))],
        out_specs=pl.BlockSpec((nh, tm, LANES), lambda j, i: (j, i, 0)),
        compiler_params=_cparams(("arbitrary", "arbitrary")),
        name="proj_heads_" + act,
    )(h, w)


def _proj_gate(h, w2, lb2):
    t, d = h.shape
    tm = _row_tile(t)
    tn = 1024
    nh = tn // LANES
    ncb = w2.shape[2] // tn
    out_sds = lambda dt: jax.ShapeDtypeStruct((2, w2.shape[2] // LANES, t, LANES), dt)
    return pl.pallas_call(
        _proj_gate_kernel,
        out_shape=(out_sds(_f32), out_sds(_bf16)),
        grid=(2, ncb, t // tm),
        in_specs=[pl.BlockSpec((tm, d), lambda r, j, i: (i, 0)),
                  pl.BlockSpec((1, d, tn), lambda r, j, i: (r, 0, j)),
                  pl.BlockSpec((1, 1, tn), lambda r, j, i: (r, 0, j))],
        out_specs=(pl.BlockSpec((1, nh, tm, LANES), lambda r, j, i: (r, j, i, 0)),
                   pl.BlockSpec((1, nh, tm, LANES), lambda r, j, i: (r, j, i, 0))),
        compiler_params=_cparams(("arbitrary", "arbitrary", "arbitrary")),
        name="proj_gate",
    )(h, w2, lb2)


def _proj_sigmoid(h, w):
    t, d = h.shape
    tm = _row_tile(t)
    tn = 1024
    return pl.pallas_call(
        _proj_sigmoid_kernel,
        out_shape=jax.ShapeDtypeStruct((t, w.shape[1]), _bf16),
        grid=(w.shape[1] // tn, t // tm),
        in_specs=[pl.BlockSpec((tm, d), lambda j, i: (i, 0)),
                  pl.BlockSpec((d, tn), lambda j, i: (0, j))],
        out_specs=pl.BlockSpec((tm, tn), lambda j, i: (i, j)),
        compiler_params=_cparams(("arbitrary", "arbitrary")),
        name="proj_sigmoid",
    )(h, w)


def _attn_kernel(q_ref, kp_ref, kc_ref, kn_ref, vp_ref, vc_ref, vn_ref,
                 kx_ref, vx_ref, bias_ref, sink_ref, o_ref):
    hd = A_HEAD_DIM
    blk = A_BLOCK
    nt = (((1,), (1,)), ((), ()))
    n_ctx = kx_ref.shape[0]
    ones = jnp.ones((n_ctx + 3 * blk, LANES), _bf16)
    for g in range(A_KV_HEADS):
        gs = slice(g * hd, (g + 1) * hd)
        qg = jnp.concatenate(
            [q_ref[:, (g * A_GROUP + h) * hd:(g * A_GROUP + h + 1) * hd] for h in range(A_GROUP)],
            axis=0)
        k_pc = jnp.concatenate([kp_ref[:, gs], kc_ref[:, gs]], axis=0)
        s_x = lax.dot_general(qg, kx_ref[:, gs], nt, preferred_element_type=_f32)
        s_pc = lax.dot_general(qg, k_pc, nt, preferred_element_type=_f32)
        s_n = lax.dot_general(qg, kn_ref[:, gs], nt, preferred_element_type=_f32) + bias_ref[0, 1]
        s_p = s_pc[:, 0:blk] + bias_ref[0, 0]
        s_c = s_pc[:, blk:2 * blk]
        mx = jnp.maximum(jnp.maximum(s_p, s_c), s_n)
        for j in range(n_ctx // LANES):
            mx = jnp.maximum(mx, s_x[:, j * LANES:(j + 1) * LANES])
        sink = sink_ref[g]
        m = jnp.maximum(jnp.max(mx, axis=-1, keepdims=True), sink)
        p = jnp.concatenate([jnp.exp2(s_x - m), jnp.exp2(s_p - m), jnp.exp2(s_c - m), jnp.exp2(s_n - m)],
                            axis=1).astype(_bf16)
        v_all = jnp.concatenate([vx_ref[:, gs], vp_ref[:, gs], vc_ref[:, gs], vn_ref[:, gs]], axis=0)
        o = jnp.dot(p, v_all, preferred_element_type=_f32)
        den = jnp.dot(p, ones, preferred_element_type=_f32)
        o = o / (den[:, 0:hd] + jnp.exp2(sink - m))
        for h in range(A_GROUP):
            c0 = (g * A_GROUP + h) * hd
            o_ref[:, c0:c0 + hd] = o[h * A_BLOCK:(h + 1) * A_BLOCK, :].astype(o_ref.dtype)


def _attention(q, k, v, sinks, n, l):
    blk = A_BLOCK
    nblk = n // blk
    off = l // blk
    rows = A_GROUP * blk
    r = np.arange(rows)[:, None] % blk
    j = np.arange(blk)[None, :]
    keep_prev = (j - blk - r) >= -WINDOW
    keep_next = (j + blk - r) <= WINDOW
    none = np.zeros_like(keep_prev)
    variants = [(none, keep_next), (keep_prev, keep_next), (keep_prev, none)]
    bias = jnp.asarray(np.where(np.array(variants), 0.0, NEG).astype(np.float32))
    sink_col = jnp.repeat(sinks.astype(_f32).reshape(A_KV_HEADS, A_GROUP) * LOG2E, blk, axis=1)
    sink_col = sink_col.reshape(A_KV_HEADS, rows, 1)

    def bias_idx(i):
        return (jnp.where(i == 0, 0, jnp.where(i == nblk - 1, 2, 1)), 0, 0, 0)

    kv_spec = lambda f: pl.BlockSpec((blk, A_KV_WIDTH), f)
    prev = lambda i: (jnp.maximum(i - 1, 0) + off, 0)
    cur = lambda i: (i + off, 0)
    nxt = lambda i: (jnp.minimum(i + 1, nblk - 1) + off, 0)
    return pl.pallas_call(
        _attn_kernel,
        out_shape=jax.ShapeDtypeStruct((n, A_WIDTH), _bf16),
        grid=(nblk,),
        in_specs=[pl.BlockSpec((blk, A_WIDTH), cur),
                  kv_spec(prev), kv_spec(cur), kv_spec(nxt),
                  kv_spec(prev), kv_spec(cur), kv_spec(nxt),
                  pl.BlockSpec((l, A_KV_WIDTH), lambda i: (0, 0)),
                  pl.BlockSpec((l, A_KV_WIDTH), lambda i: (0, 0)),
                  pl.BlockSpec((1, 2, rows, blk), bias_idx),
                  pl.BlockSpec((A_KV_HEADS, rows, 1), lambda i: (0, 0, 0))],
        out_specs=pl.BlockSpec((blk, A_WIDTH), lambda i: (i, 0)),
        compiler_params=_cparams(("arbitrary",)),
        name="window_attention",
    )(q, k, k, k, v, v, v, k, v, bias, sink_col)


def _split3(x):
    hi = x.astype(_bf16)
    r1 = x - hi.astype(_f32)
    mid = r1.astype(_bf16)
    lo = (r1 - mid.astype(_f32)).astype(_bf16)
    return hi, mid, lo


def _hgrn_intra_scores(q, k, b, emat, reverse):
    c = HG_CHUNK
    nsb = c // SUB
    nt = (((1,), (1,)), ((), ()))
    qb = (q * jnp.exp2(b)).astype(_bf16)
    b_tot = b[0:1, :] if reverse else b[c - 1:c, :]
    kend = (k * jnp.exp2(b_tot - b)).astype(_bf16)

    zero_row = jnp.zeros((1, LANES), _f32)
    zero_blk = jnp.zeros((SUB, LANES), _f32)
    y_rows, z_rows, w_cols = [], [], []
    for ib in range(nsb):
        r0 = ib * SUB
        bi = b[r0:r0 + SUB]
        qi = q[r0:r0 + SUB]
        pieces = []
        for s in range(SUB):
            bs = b[r0 + s:r0 + s + 1]
            ks = k[r0 + s:r0 + s + 1]
            pieces.append(qi * ks * jnp.exp2(jnp.minimum(bi - bs, 0.0)))
        y_rows.append(jnp.concatenate(pieces, axis=1))
        if reverse:
            lo_r, hi_r = r0 + SUB, c
            ref_b = b[r0 + SUB:r0 + SUB + 1] if ib < nsb - 1 else zero_row
        else:
            lo_r, hi_r = 0, r0
            ref_b = b[r0 - 1:r0] if ib > 0 else zero_row
        if hi_r > lo_r:
            qp = qi * jnp.exp2(bi - ref_b)
            kp = [k[lo_r:hi_r] * jnp.exp2(ref_b - b[lo_r:hi_r])]
            if lo_r > 0:
                kp.insert(0, jnp.zeros((lo_r, LANES), _f32))
            if hi_r < c:
                kp.append(jnp.zeros((c - hi_r, LANES), _f32))
            w_cols.append(jnp.concatenate(kp, axis=0) if len(kp) > 1 else kp[0])
        else:
            qp = zero_blk
            w_cols.append(jnp.zeros((c, LANES), _f32))
        z_rows.append(jnp.concatenate([qp if jb == ib else zero_blk for jb in range(nsb)], axis=1))
    y = jnp.concatenate(y_rows, axis=0).astype(_bf16)
    z = jnp.concatenate(z_rows, axis=0).astype(_bf16)
    w = jnp.concatenate(w_cols, axis=1).astype(_bf16)
    a_diag = jnp.dot(y, emat, preferred_element_type=_f32)
    a_off = lax.dot_general(z, w, nt, preferred_element_type=_f32)
    return a_diag, a_off, qb, kend, jnp.exp2(b_tot)


def _hgrn_intra_kernel(q_ref, k_ref, lf_ref, emat_ref, a_ref, qb_ref, ke_ref, dt_ref, *, reverse):
    c = HG_CHUNK
    nch = HG_BLOCK // c
    nh = q_ref.shape[0]
    ri = lax.broadcasted_iota(jnp.int32, (c, c), 0)
    ci = lax.broadcasted_iota(jnp.int32, (c, c), 1)
    tri = jnp.where((ci >= ri) if reverse else (ci <= ri), 1.0, 0.0).astype(_bf16)
    dmask = (ci // SUB == ri // SUB) & ((ci >= ri) if reverse else (ci <= ri))
    emat = emat_ref[...]

    def body(cc, carry):
        rows = pl.ds(pl.multiple_of(cc * c, c), c)
        lf_all = jnp.concatenate([lf_ref[0, h, rows, :] for h in range(nh)], axis=1)
        hi, mid, lo = _split3(lf_all)
        b_all = (jnp.dot(tri, hi, preferred_element_type=_f32)
                 + jnp.dot(tri, mid, preferred_element_type=_f32)
                 + jnp.dot(tri, lo, preferred_element_type=_f32))
        for h in range(nh):
            a_diag, a_off, qb, kend, dtot = _hgrn_intra_scores(
                q_ref[h, rows, :].astype(_f32), k_ref[0, h, rows, :].astype(_f32),
                b_all[:, h * LANES:(h + 1) * LANES], emat, reverse)
            qb_ref[h, rows, :] = qb
            ke_ref[h, rows, :] = kend
            dt_ref[cc, h:h + 1, :] = dtot
            a = a_off + jnp.where(dmask, a_diag[:, 0:c], 0.0)
            a_ref[h, rows, :] = jnp.concatenate([a, jnp.zeros_like(a)], axis=1).astype(a_ref.dtype)
        return carry

    lax.fori_loop(0, nch, body, 0)


def _hgrn_intra(rq, kk, lf, direction):
    nh, t, _ = rq.shape
    br = HG_BLOCK
    nch = br // HG_CHUNK
    reverse = direction == 1
    spec3 = pl.BlockSpec((nh, br, LANES), lambda s: (0, s, 0))
    spec4 = pl.BlockSpec((1, nh, br, LANES), lambda s: (direction, 0, s, 0))
    sds = jax.ShapeDtypeStruct((nh, t, LANES), _bf16)
    emat = (np.arange(SUB * LANES)[:, None] // LANES == np.arange(LANES)[None, :] % SUB)
    emat = jnp.asarray(emat.astype(np.float32), dtype=_bf16)
    return pl.pallas_call(
        functools.partial(_hgrn_intra_kernel, reverse=reverse),
        out_shape=(sds, sds, sds, jax.ShapeDtypeStruct((t // HG_CHUNK, nh, LANES), _f32)),
        grid=(t // br,),
        in_specs=[spec3, spec4, spec4, pl.BlockSpec(emat.shape, lambda s: (0, 0))],
        out_specs=(spec3, spec3, spec3, pl.BlockSpec((nch, nh, LANES), lambda s: (s, 0, 0))),
        compiler_params=_cparams(("arbitrary",)),
        name="hgrn2_intra_" + ("bwd" if reverse else "fwd"),
    )(rq, kk, lf, emat)


def _hgrn_state_kernel(af_ref, qf_ref, kf_ref, df_ref, vf_ref, ab_ref, qbk_ref, kb_ref, db_ref, vb_ref,
                       outf_ref, outb_ref, st_ref):
    c = HG_CHUNK
    nch = HG_BLOCK // c
    nh = qf_ref.shape[0]
    nt = (((1,), (1,)), ((), ()))
    tn = (((0,), (0,)), ((), ()))

    @pl.when(pl.program_id(0) == 0)
    def _():
        st_ref[...] = jnp.zeros(st_ref.shape, st_ref.dtype)

    dirs = ((0, False, af_ref, qf_ref, kf_ref, df_ref, vf_ref, outf_ref),
            (1, True, ab_ref, qbk_ref, kb_ref, db_ref, vb_ref, outb_ref))

    def body(cc, carry):
        for di, reverse, a_ref, q_ref, k_ref, d_ref, v_ref, out_ref in dirs:
            chunk = (nch - 1 - cc) if reverse else cc
            rows = pl.ds(pl.multiple_of(chunk * c, c), c)
            for h in range(nh):
                st = st_ref[di, h]
                v = v_ref[h, rows, :]
                o = (jnp.dot(a_ref[h, rows, 0:c], v, preferred_element_type=_f32)
                     + lax.dot_general(q_ref[h, rows, :], st.astype(_bf16), nt, preferred_element_type=_f32))
                out_ref[h, rows, :] = o.astype(out_ref.dtype)
                upd = lax.dot_general(v, k_ref[h, rows, :], tn, preferred_element_type=_f32)
                st_ref[di, h] = st * d_ref[chunk, pl.ds(h, 1), :] + upd
        return carry

    lax.fori_loop(0, nch, body, 0)


def _hgrn_state(intra_f, intra_b, rv, l):
    a_f, qb_f, ke_f, dt_f = intra_f
    a_b, qb_b, ke_b, dt_b = intra_b
    nh, t, _ = rv.shape
    br = HG_BLOCK
    nch = br // HG_CHUNK
    nb = t // br
    nc = l // br

    def blk_b(s):
        return jnp.where(s < nc, nc - 1 - s, nb - 1 - (s - nc))

    def specs(blk):
        s3 = pl.BlockSpec((nh, br, LANES), lambda s: (0, blk(s), 0))
        sd = pl.BlockSpec((nch, nh, LANES), lambda s: (blk(s), 0, 0))
        return [s3, s3, s3, sd, s3]

    fwd = specs(lambda s: s)
    bwd = specs(blk_b)
    return pl.pallas_call(
        _hgrn_state_kernel,
        out_shape=(jax.ShapeDtypeStruct((nh, t, LANES), _bf16),
                   jax.ShapeDtypeStruct((nh, t, LANES), _bf16)),
        grid=(nb,),
        in_specs=fwd + bwd,
        out_specs=(fwd[0], bwd[0]),
        scratch_shapes=[pltpu.VMEM((2, nh, B_DV, B_DK), _f32)],
        compiler_params=_cparams(("arbitrary",)),
        name="hgrn2_state",
    )(a_f, qb_f, ke_f, dt_f, rv, a_b, qb_b, ke_b, dt_b, rv)


def _merge_kernel(ya_ref, of_ref, ob_ref, rg_ref, ga_ref, gb_ref, x_ref, wa_ref, wb_ref, wo_ref,
                  wr_ref, br_ref, hg_ref, g2_ref, mod_ref,
                  x1_ref, h2_ref, route_ref, cnt_ref, run_ref):
    i = pl.program_id(0)
    tm, d = x_ref.shape

    @pl.when(i == 0)
    def _():
        run_ref[...] = jnp.zeros(run_ref.shape, run_ref.dtype)

    parts = []
    for h in range(B_HEADS):
        o = of_ref[h].astype(_f32) + ob_ref[h].astype(_f32)
        o = o * lax.rsqrt(jnp.mean(o * o, axis=-1, keepdims=True) + EPS)
        o = o * hg_ref[:, h * LANES:(h + 1) * LANES]
        parts.append((o * rg_ref[h].astype(_f32)).astype(_bf16))
    yb = jnp.concatenate(parts, axis=1)
    z0 = jnp.dot(ya_ref[...], wa_ref[...], preferred_element_type=_f32)
    z1 = jnp.dot(yb, wb_ref[...], preferred_element_type=_f32)
    merged = ga_ref[...].astype(_f32) * z0 + gb_ref[...].astype(_f32) * z1
    y = jnp.dot(merged.astype(_bf16), wo_ref[...], preferred_element_type=_f32)
    x1 = x_ref[...] + mod_ref[0:1, 2 * d:3 * d] * y
    x1_ref[...] = x1
    h2 = _rms_mod(x1, g2_ref[...], mod_ref[0:1, 3 * d:4 * d], mod_ref[0:1, 4 * d:5 * d])
    h2_ref[...] = _pack_bf16_pair(h2[:, 0:d // 2], h2[:, d // 2:d])

    hh = h2.astype(_bf16)
    hl = (h2 - hh.astype(_f32)).astype(_bf16)
    both = jnp.dot(hh, wr_ref[...], preferred_element_type=_f32)
    logits = (both[:, 0:LANES] + both[:, LANES:2 * LANES]
              + jnp.dot(hl, wr_ref[:, 0:LANES], preferred_element_type=_f32)) + br_ref[...]
    lane = lax.broadcasted_iota(jnp.int32, (tm, LANES), 1)
    work = jnp.where(lane < N_EXPERTS, logits, NEG)
    vals, idxs = [], []
    onehot = jnp.zeros((tm, LANES), _f32)
    for _ in range(TOP_K):
        m = jnp.max(work, axis=-1, keepdims=True)
        idx = jnp.min(jnp.where(work == m, lane, LANES), axis=-1, keepdims=True)
        sel = lane == idx
        vals.append(m)
        idxs.append(idx)
        onehot = jnp.where(sel, 1.0, onehot)
        work = jnp.where(sel, NEG, work)
    es = [jnp.exp(vv - vals[0]) for vv in vals]
    tot = es[0] + es[1] + es[2] + es[3]
    ri = lax.broadcasted_iota(jnp.int32, (tm, tm), 0)
    ci = lax.broadcasted_iota(jnp.int32, (tm, tm), 1)
    ltri = jnp.where(ci < ri, 1.0, 0.0).astype(_bf16)
    prefix = jnp.dot(ltri, onehot.astype(_bf16), preferred_element_type=_f32) + run_ref[...]
    route = jnp.zeros((tm, LANES), _f32)
    for j in range(TOP_K):
        rank = jnp.sum(jnp.where(lane == idxs[j], prefix, 0.0), axis=-1, keepdims=True)
        route = jnp.where(lane == j, idxs[j].astype(_f32), route)
        route = jnp.where(lane == TOP_K + j, es[j] / tot, route)
        route = jnp.where(lane == 2 * TOP_K + j, rank, route)
    route_ref[...] = route
    run_new = run_ref[...] + jnp.sum(onehot, axis=0, keepdims=True)
    run_ref[...] = run_new
    cnt_ref[...] = run_new


def _merge_route(ya, o_f, o_b, rg, gates, x2, wa, wb, wo, wr_cat, b_router, hg, g2, mod, l):
    n, d = x2.shape
    tm = 256
    off = l // tm
    const = lambda shape: pl.BlockSpec(shape, lambda i: (0,) * len(shape),
                                       pipeline_mode=pl.Buffered(1))
    head_spec = pl.BlockSpec((B_HEADS, tm, LANES), lambda i: (0, i + off, 0))
    return pl.pallas_call(
        _merge_kernel,
        out_shape=(jax.ShapeDtypeStruct((n, d), _f32),
                   jax.ShapeDtypeStruct((n, d // 2), jnp.uint32),
                   jax.ShapeDtypeStruct((n, LANES), _f32),
                   jax.ShapeDtypeStruct((1, LANES), _f32)),
        grid=(n // tm,),
        in_specs=[pl.BlockSpec((tm, A_WIDTH), lambda i: (i, 0)),
                  head_spec, head_spec, head_spec,
                  pl.BlockSpec((tm, d), lambda i: (i + off, 0)),
                  pl.BlockSpec((tm, d), lambda i: (i + off, 1)),
                  pl.BlockSpec((tm, d), lambda i: (i, 0)),
                  const(wa.shape), const(wb.shape), const(wo.shape),
                  const(wr_cat.shape), const((1, LANES)),
                  const((1, B_WIDTH)), const((1, d)), const(mod.shape)],
        out_specs=(pl.BlockSpec((tm, d), lambda i: (i, 0)),
                   pl.BlockSpec((tm, d // 2), lambda i: (i, 0)),
                   pl.BlockSpec((tm, LANES), lambda i: (i, 0)),
                   pl.BlockSpec((1, LANES), lambda i: (0, 0))),
        scratch_shapes=[pltpu.VMEM((1, LANES), _f32)],
        compiler_params=_cparams(("arbitrary",)),
        name="merge_route",
    )(ya, o_f, o_b, rg, gates, gates, x2, wa, wb, wo, wr_cat, b_router, hg, g2, mod)


def _zero_fill_tiles(zero_ref, dst_ref, first, last, sem, cols=(None,)):
    tm = MOE_TM

    def copy(t, col):
        rows = pl.ds(pl.multiple_of(t * tm, tm), tm)
        dst = dst_ref.at[rows] if col is None else dst_ref.at[rows, col]
        return pltpu.make_async_copy(zero_ref, dst, sem)

    def start(t, carry):
        for col in cols:
            copy(t, col).start()
        return carry

    def wait(t, carry):
        for col in cols:
            copy(t, col).wait()
        return carry

    lax.fori_loop(first, last, start, 0)
    lax.fori_loop(first, last, wait, 0)


def _dispatch_kernel(lt_ref, nt_ref, dest_ref, h2_ref, xs_ref, zero_ref, sem, zsem):
    tm = h2_ref.shape[0]

    @pl.when(pl.program_id(0) == 0)
    def _():
        zero_ref[...] = jnp.zeros(zero_ref.shape, zero_ref.dtype)
        for e in range(N_EXPERTS):
            @pl.when(nt_ref[e] > 0)
            def _():
                pltpu.make_async_copy(zero_ref, xs_ref.at[pl.ds(lt_ref[e] * MOE_TM, MOE_TM)], zsem).start()
        for e in range(N_EXPERTS):
            @pl.when(nt_ref[e] > 0)
            def _():
                pltpu.make_async_copy(zero_ref, xs_ref.at[pl.ds(lt_ref[e] * MOE_TM, MOE_TM)], zsem).wait()
        _zero_fill_tiles(zero_ref, xs_ref, lt_ref[N_EXPERTS - 1] + 1, xs_ref.shape[0] // MOE_TM, zsem)

    def issue(t, carry):
        for j in range(TOP_K):
            dst = dest_ref[t * TOP_K + j]
            pltpu.make_async_copy(h2_ref.at[pl.ds(t, 1)], xs_ref.at[pl.ds(dst, 1)], sem).start()
        return carry

    lax.fori_loop(0, tm, issue, 0)
    for _ in range(TOP_K):
        pltpu.make_async_copy(h2_ref, xs_ref.at[pl.ds(0, tm)], sem).wait()


def _dispatch(h2, dest_flat, last_tile, ntile, n_slots):
    n, d = h2.shape
    tm = 256
    return pl.pallas_call(
        _dispatch_kernel,
        out_shape=jax.ShapeDtypeStruct((n_slots, d), h2.dtype),
        grid_spec=pltpu.PrefetchScalarGridSpec(
            num_scalar_prefetch=2,
            grid=(n // tm,),
            in_specs=[pl.BlockSpec((tm * TOP_K,), lambda i, lt, nt: (i,), memory_space=pltpu.SMEM),
                      pl.BlockSpec((tm, d), lambda i, lt, nt: (i, 0))],
            out_specs=pl.BlockSpec(memory_space=pl.ANY),
            scratch_shapes=[pltpu.VMEM((MOE_TM, d), h2.dtype), pltpu.SemaphoreType.DMA,
                            pltpu.SemaphoreType.DMA]),
        compiler_params=_cparams(("arbitrary",)),
        name="moe_dispatch",
    )(last_tile, ntile, dest_flat, h2)


def _cast_rows(src_ref, dst_ref):
    rows = dst_ref.shape[0]
    step = 256

    def body(i, carry):
        r = pl.ds(pl.multiple_of(i * step, step), step)
        dst_ref[r, :] = src_ref[r, :].astype(dst_ref.dtype)
        return carry

    lax.fori_loop(0, rows // step, body, 0)


def _expert_stream(ts_ref, nt_ref, weights, in_copy, out_copy, compute):
    nb = MOE_NBUF
    n_exp = nt_ref.shape[0]
    total = ts_ref[n_exp - 1] + nt_ref[n_exp - 1]
    for k in range(nb - 1):
        @pl.when(total > k)
        def _():
            in_copy(k, k).start(priority=1)
    for fetch, _, _ in weights:
        fetch(0, 0).start()

    def expert_body(e, carry):
        wslot = lax.rem(e, 2)
        for fetch, _, _ in weights:
            fetch(e, wslot).wait()

        @pl.when(e + 1 < n_exp)
        def _():
            for fetch, _, _ in weights:
                fetch(e + 1, 1 - wslot).start()

        for _, stage_ref, bf_ref in weights:
            _cast_rows(stage_ref.at[wslot], bf_ref)
        t0 = ts_ref[e]

        def tile_body(t, c2):
            g = t0 + t
            slot = lax.rem(g, nb)
            in_copy(g, slot).wait()

            @pl.when(g + nb - 1 < total)
            def _():
                in_copy(g + nb - 1, lax.rem(g + nb - 1, nb)).start(priority=1)

            @pl.when(g >= nb)
            def _():
                out_copy(g - nb, slot).wait()

            compute(e, slot)
            out_copy(g, slot).start(priority=1)
            return c2

        lax.fori_loop(0, nt_ref[e], tile_body, 0)
        return carry

    lax.fori_loop(0, n_exp, expert_body, 0)
    for k in range(1, nb + 1):
        @pl.when(total >= k)
        def _():
            out_copy(total - k, lax.rem(total - k, nb)).wait()
    return total


def _tile_rows(g):
    return pl.ds(pl.multiple_of(g * MOE_TM, MOE_TM), MOE_TM)


def _moe_up_kernel(ts_ref, nt_ref, xs_ref, wg_ref, wu_ref, bg_ref, bu_ref, act_ref,
                   wgs_ref, wus_ref, wgb_ref, wub_ref, xbuf, obuf, sem_wg, sem_wu, sem_in, sem_out):
    c = pl.program_id(0)
    tm = MOE_TM
    fc = wgb_ref.shape[1]
    col = pl.ds(pl.multiple_of(c * fc, fc), fc)

    def in_copy(g, slot):
        return pltpu.make_async_copy(xs_ref.at[_tile_rows(g)], xbuf.at[slot], sem_in.at[slot])

    def out_copy(g, slot):
        return pltpu.make_async_copy(obuf.at[slot], act_ref.at[_tile_rows(g), col], sem_out.at[slot])

    def fetch_g(e, slot):
        return pltpu.make_async_copy(wg_ref.at[e, :, col], wgs_ref.at[slot], sem_wg.at[slot])

    def fetch_u(e, slot):
        return pltpu.make_async_copy(wu_ref.at[e, :, col], wus_ref.at[slot], sem_wu.at[slot])

    def compute(e, slot):
        x_lo, x_hi = _unpack_bf16_pair(xbuf[slot])
        x_lo, x_hi = x_lo.astype(_bf16), x_hi.astype(_bf16)
        half = x_lo.shape[1]

        def proj(w_ref, b_ref):
            return (jnp.dot(x_lo, w_ref[0:half, :], preferred_element_type=_f32)
                    + jnp.dot(x_hi, w_ref[half:2 * half, :], preferred_element_type=_f32)
                    + b_ref[pl.ds(e, 1), col])

        g = jnp.minimum(proj(wgb_ref, bg_ref), SWIGLU_LIMIT)
        u = jnp.clip(proj(wub_ref, bu_ref), -SWIGLU_LIMIT, SWIGLU_LIMIT)
        obuf[slot] = (g * _sigmoid(SWIGLU_ALPHA * g) * (u + 1.0)).astype(obuf.dtype)

    total = _expert_stream(ts_ref, nt_ref, [(fetch_g, wgs_ref, wgb_ref), (fetch_u, wus_ref, wub_ref)],
                           in_copy, out_copy, compute)
    obuf[0] = jnp.zeros(obuf.shape[1:], obuf.dtype)
    _zero_fill_tiles(obuf.at[0], act_ref, total, act_ref.shape[0] // tm, sem_out.at[0], cols=(col,))


def _moe_up(xs, wg, wu, bg, bu, tstart, ntile):
    n_slots = xs.shape[0]
    ne, d, ff = wg.shape
    fc = ff // 2
    tm = MOE_TM
    any_spec = pl.BlockSpec(memory_space=pl.ANY)
    b_spec = pl.BlockSpec((ne, ff), lambda c, ts, nt: (0, 0))
    return pl.pallas_call(
        _moe_up_kernel,
        out_shape=jax.ShapeDtypeStruct((n_slots, ff), _bf16),
        grid_spec=pltpu.PrefetchScalarGridSpec(
            num_scalar_prefetch=2,
            grid=(ff // fc,),
            in_specs=[any_spec, any_spec, any_spec, b_spec, b_spec],
            out_specs=any_spec,
            scratch_shapes=[pltpu.VMEM((2, d, fc), _f32), pltpu.VMEM((2, d, fc), _f32),
                            pltpu.VMEM((d, fc), _bf16), pltpu.VMEM((d, fc), _bf16),
                            pltpu.VMEM((MOE_NBUF, tm, xs.shape[1]), xs.dtype),
                            pltpu.VMEM((MOE_NBUF, tm, fc), _bf16),
                            pltpu.SemaphoreType.DMA((2,)), pltpu.SemaphoreType.DMA((2,)),
                            pltpu.SemaphoreType.DMA((MOE_NBUF,)), pltpu.SemaphoreType.DMA((MOE_NBUF,))]),
        compiler_params=_cparams(("arbitrary",)),
        name="moe_gate_up",
    )(tstart, ntile, xs, wg, wu, bg, bu)


def _moe_down_kernel(ts_ref, nt_ref, act_ref, wd_ref, bd_ref, ys_ref,
                     wds_ref, wdb_ref, abuf, ybuf, sem_w, sem_in, sem_out):
    tm = MOE_TM
    d = wdb_ref.shape[1]

    def in_copy(g, slot):
        return pltpu.make_async_copy(act_ref.at[_tile_rows(g)], abuf.at[slot], sem_in.at[slot])

    def out_copy(g, slot):
        return pltpu.make_async_copy(ybuf.at[slot], ys_ref.at[_tile_rows(g)], sem_out.at[slot])

    def fetch(e, slot):
        return pltpu.make_async_copy(wd_ref.at[e], wds_ref.at[slot], sem_w.at[slot])

    def compute(e, slot):
        y = jnp.dot(abuf[slot], wdb_ref[...], preferred_element_type=_f32) + bd_ref[pl.ds(e, 1), :]
        ybuf[slot] = _pack_bf16_pair(y[:, 0:d // 2], y[:, d // 2:d])

    total = _expert_stream(ts_ref, nt_ref, [(fetch, wds_ref, wdb_ref)], in_copy, out_copy, compute)
    ybuf[0] = jnp.zeros(ybuf.shape[1:], ybuf.dtype)
    _zero_fill_tiles(ybuf.at[0], ys_ref, total, ys_ref.shape[0] // tm, sem_out.at[0])


def _moe_down(act, wd, bd, tstart, ntile):
    n_slots, ff = act.shape
    ne, _, d = wd.shape
    tm = MOE_TM
    any_spec = pl.BlockSpec(memory_space=pl.ANY)
    return pl.pallas_call(
        _moe_down_kernel,
        out_shape=jax.ShapeDtypeStruct((n_slots, d // 2), jnp.uint32),
        grid_spec=pltpu.PrefetchScalarGridSpec(
            num_scalar_prefetch=2,
            grid=(1,),
            in_specs=[any_spec, any_spec, pl.BlockSpec((ne, d), lambda i, ts, nt: (0, 0))],
            out_specs=any_spec,
            scratch_shapes=[pltpu.VMEM((2, ff, d), _f32), pltpu.VMEM((ff, d), _bf16),
                            pltpu.VMEM((MOE_NBUF, tm, ff), _bf16), pltpu.VMEM((MOE_NBUF, tm, d // 2), jnp.uint32),
                            pltpu.SemaphoreType.DMA((2,)),
                            pltpu.SemaphoreType.DMA((MOE_NBUF,)), pltpu.SemaphoreType.DMA((MOE_NBUF,))]),
        compiler_params=_cparams(("arbitrary",)),
        name="moe_down",
    )(tstart, ntile, act, wd, bd)


def _combine_kernel(dcur_ref, dnxt_ref, ys_ref, route_ref, x1_ref, mod_ref, gf_ref, o_ref, buf_ref, sem):
    tm, d = x1_ref.shape
    i = pl.program_id(0)
    slot = lax.rem(i, 2)

    def gather(dest_ref, s):
        def issue(t, carry):
            for j in range(TOP_K):
                src = dest_ref[t * TOP_K + j]
                pltpu.make_async_copy(ys_ref.at[pl.ds(src, 1)], buf_ref.at[s, j, pl.ds(t, 1)], sem.at[s]).start()
            return carry

        lax.fori_loop(0, tm, issue, 0)

    @pl.when(i == 0)
    def _():
        gather(dcur_ref, slot)

    @pl.when(i + 1 < pl.num_programs(0))
    def _():
        gather(dnxt_ref, 1 - slot)

    for j in range(TOP_K):
        pltpu.make_async_copy(ys_ref.at[pl.ds(0, tm)], buf_ref.at[slot, j], sem.at[slot]).wait()
    acc_lo = jnp.zeros((tm, d // 2), _f32)
    acc_hi = jnp.zeros((tm, d // 2), _f32)
    for j in range(TOP_K):
        gate = route_ref[:, TOP_K + j:TOP_K + j + 1]
        y_lo, y_hi = _unpack_bf16_pair(buf_ref[slot, j])
        acc_lo = acc_lo + gate * y_lo
        acc_hi = acc_hi + gate * y_hi
    x2 = x1_ref[...] + mod_ref[0:1, 5 * d:6 * d] * jnp.concatenate([acc_lo, acc_hi], axis=1)
    y = x2 * lax.rsqrt(jnp.mean(x2 * x2, axis=-1, keepdims=True) + EPS)
    o_ref[...] = y * gf_ref[...]


def _combine(ys, dest_flat, route, x1, mod, gf):
    n, d = x1.shape
    tm = 256
    nsteps = n // tm
    dest_spec = lambda f: pl.BlockSpec((tm * TOP_K,), f, memory_space=pltpu.SMEM)
    return pl.pallas_call(
        _combine_kernel,
        out_shape=jax.ShapeDtypeStruct((n, d), _f32),
        grid=(nsteps,),
        in_specs=[dest_spec(lambda i: (i,)), dest_spec(lambda i: (jnp.minimum(i + 1, nsteps - 1),)),
                  pl.BlockSpec(memory_space=pl.ANY),
                  pl.BlockSpec((tm, LANES), lambda i: (i, 0)),
                  pl.BlockSpec((tm, d), lambda i: (i, 0)),
                  pl.BlockSpec(mod.shape, lambda i: (0, 0)),
                  pl.BlockSpec((1, d), lambda i: (0, 0))],
        out_specs=pl.BlockSpec((tm, d), lambda i: (i, 0)),
        scratch_shapes=[pltpu.VMEM((2, TOP_K, tm, d // 2), jnp.uint32), pltpu.SemaphoreType.DMA((2,))],
        compiler_params=_cparams(("arbitrary",)),
        name="moe_combine",
    )(dest_flat, dest_flat, ys, route, x1, mod, gf.reshape(1, d))


def _rope_tables(n, l):
    rows = n // GRID_W
    inv_freq = ROPE_BASE ** (-jnp.arange(ROPE_PAIRS_PER_AXIS, dtype=_f32) / ROPE_PAIRS_PER_AXIS)
    ang_r = jnp.arange(rows, dtype=_f32)[:, None] * inv_freq
    ang_c = jnp.arange(GRID_W, dtype=_f32)[:, None] * inv_freq
    rep = lambda t_r, t_c: jnp.concatenate(
        [jnp.repeat(t_r, GRID_W, axis=0), jnp.tile(t_c, (rows, 1))], axis=-1)
    cos, sin = rep(jnp.cos(ang_r), jnp.cos(ang_c)), rep(jnp.sin(ang_r), jnp.sin(ang_c))
    cos_t = jnp.tile(cos, (1, LANES // 32))
    sin_t = jnp.tile(jnp.concatenate([-sin, sin], axis=-1), (1, LANES // A_HEAD_DIM))
    cos_t = jnp.concatenate([jnp.ones((l, LANES), _f32), cos_t], axis=0)
    sin_t = jnp.concatenate([jnp.zeros((l, LANES), _f32), sin_t], axis=0)
    return cos_t, sin_t


def _routing_tables(route, counts):
    tm = MOE_TM
    idx = route[:, 0:TOP_K].astype(jnp.int32)
    rank = route[:, 2 * TOP_K:3 * TOP_K].astype(jnp.int32)
    cnt = counts[0, :N_EXPERTS].astype(jnp.int32)
    ntile = (cnt + tm - 1) // tm
    tstart = jnp.cumsum(ntile) - ntile
    slot0 = tstart * tm
    onehot = idx[:, :, None] == jnp.arange(N_EXPERTS, dtype=jnp.int32)
    dest = jnp.sum(jnp.where(onehot, slot0, 0), axis=-1) + rank
    return dest.reshape(-1), tstart, ntile


def kernel(x, c, ctx, c_ctx, norm_mix_g, norm_ffn_g, w_mod, b_mod, w_in, attn_sinks,
           lb_fwd_logits, lb_bwd_logits, hgrn_norm_g, w_branch, w_out, w_router, b_router,
           w_e_gate, b_e_gate, w_e_up, b_e_up, w_e_down, b_e_down, final_norm_g):
    bsz, n, d = x.shape
    l = ctx.shape[1]
    assert bsz == 1 and d == D_MODEL and norm_mix_g.shape[0] == 1
    assert n % HG_BLOCK == 0 and l % HG_BLOCK == 0 and n // A_BLOCK >= 2
    x2 = x.reshape(n, d)
    ctx2 = ctx.reshape(l, d)

    mod = _mod_vectors(c, c_ctx, w_mod[0], b_mod[0])
    h_all = _norm_all(x2, ctx2, norm_mix_g[0], mod)

    w = w_in[0].astype(_bf16)
    c0 = 0
    segs = []
    for width in (A_WIDTH + 2 * A_KV_WIDTH, B_WIDTH, B_WIDTH, B_WIDTH, B_WIDTH, B_WIDTH, 2 * d):
        segs.append(w[:, c0:c0 + width])
        c0 += width
    w_qkv, w_rq, w_zf, w_zb, w_rv, w_rg, w_gates = segs
    cos_t, sin_t = _rope_tables(n, l)
    q, k, v = _proj_qkv(h_all, w_qkv, cos_t, sin_t)
    rq = _proj_heads(h_all, w_rq, "silu")
    rv = _proj_heads(h_all, w_rv, "none")
    rg = _proj_heads(h_all, w_rg, "silu")
    lb_f = jax.nn.softmax(lb_fwd_logits.astype(_f32), axis=0)[0]
    lb_b = jax.nn.softmax(lb_bwd_logits.astype(_f32), axis=0)[0]
    lb2 = jnp.stack([lb_f, lb_b]).reshape(2, 1, B_WIDTH)
    lf, kk = _proj_gate(h_all, jnp.stack([w_zf, w_zb]), lb2)
    gates = _proj_sigmoid(h_all, w_gates)

    ya = _attention(q, k, v, attn_sinks[0], n, l)
    o_f, o_b = _hgrn_state(_hgrn_intra(rq, kk, lf, 0), _hgrn_intra(rq, kk, lf, 1), rv, l)

    wr = jnp.zeros((d, LANES), _f32).at[:, :N_EXPERTS].set(w_router[0].astype(_f32))
    wr_hi = wr.astype(_bf16)
    wr_lo = (wr - wr_hi.astype(_f32)).astype(_bf16)
    br = jnp.zeros((1, LANES), _f32).at[0, :N_EXPERTS].set(b_router[0].astype(_f32))
    x1, h2, route, counts = _merge_route(
        ya, o_f, o_b, rg, gates, x2,
        w_branch[0, 0].astype(_bf16), w_branch[0, 1].astype(_bf16), w_out[0].astype(_bf16),
        jnp.concatenate([wr_hi, wr_lo], axis=1), br, hgrn_norm_g[0].reshape(1, B_WIDTH).astype(_f32),
        norm_ffn_g[0].reshape(1, d), mod, l)

    n_tiles = (n * TOP_K) // MOE_TM + N_EXPERTS
    dest, tstart, ntile = _routing_tables(route, counts)
    xs = _dispatch(h2, dest, tstart + ntile - 1, ntile, n_tiles * MOE_TM)
    act = _moe_up(xs, w_e_gate[0], w_e_up[0], b_e_gate[0], b_e_up[0], tstart, ntile)
    ys = _moe_down(act, w_e_down[0], b_e_down[0], tstart, ntile)
    out = _combine(ys, dest, route, x1, mod, final_norm_g)
    return out.reshape(bsz, n, d)
```

```python
import functools

import jax
import jax.numpy as jnp
import numpy as np
from jax import lax
from jax.experimental import pallas as pl
from jax.experimental.pallas import tpu as pltpu

D_MODEL = 2048
GRID_W = 64
EPS = 1e-6
A_HEADS = 16
A_KV_HEADS = 2
A_GROUP = A_HEADS // A_KV_HEADS
A_HEAD_DIM = 64
A_WIDTH = A_HEADS * A_HEAD_DIM
A_KV_WIDTH = A_KV_HEADS * A_HEAD_DIM
WINDOW = 128
A_BLOCK = 128
ROPE_BASE = 10000.0
ROPE_PAIRS_PER_AXIS = A_HEAD_DIM // 4
B_HEADS = 8
B_DK = 128
B_DV = 128
B_WIDTH = B_HEADS * B_DV
N_EXPERTS = 32
TOP_K = 4
EXPERT_FF = 2048
SWIGLU_LIMIT = 7.0
SWIGLU_ALPHA = 1.702

LANES = 128
SUB = 8
HG_CHUNK = 64
HG_BLOCK = 256
ATTN_UNIT_HEADS = 4
MOE_TM = 256
MOE_NBUF = 5
NEG = -1e30
LOG2E = 1.4426950408889634
VMEM_LIMIT = 56 * 1024 * 1024

_f32 = jnp.float32
_bf16 = jnp.bfloat16


def _cparams(sem):
    return pltpu.CompilerParams(dimension_semantics=sem, vmem_limit_bytes=VMEM_LIMIT)


def _sigmoid(x):
    return 1.0 / (1.0 + jnp.exp(-x))


def _pack_bf16_pair(lo, hi):
    lo_bits = lax.bitcast_convert_type(lo.astype(_bf16).astype(_f32), jnp.uint32)
    hi_bits = lax.bitcast_convert_type(hi.astype(_bf16).astype(_f32), jnp.uint32)
    return (lo_bits >> 16) | (hi_bits & jnp.uint32(0xFFFF0000))


def _unpack_bf16_pair(w):
    lo = lax.bitcast_convert_type(w << 16, _f32)
    hi = lax.bitcast_convert_type(w & jnp.uint32(0xFFFF0000), _f32)
    return lo, hi


def _mod_kernel(s_ref, w_ref, b_ref, o_ref):
    tn = w_ref.shape[1]
    for r in range(2):
        s = s_ref[r]
        s = s * _sigmoid(s)
        for j in range(tn // LANES):
            sl = slice(j * LANES, (j + 1) * LANES)
            acc = jnp.sum(w_ref[:, sl] * s, axis=0, keepdims=True)
            o_ref[r:r + 1, sl] = acc + b_ref[:, sl]


def _mod_vectors(c, c_ctx, w_mod, b_mod):
    d, n_out = w_mod.shape
    tn = 1024
    s = jnp.stack([c.reshape(d), c_ctx.reshape(d)]).astype(_f32)
    s = jnp.broadcast_to(s[:, :, None], (2, d, LANES))
    return pl.pallas_call(
        _mod_kernel,
        out_shape=jax.ShapeDtypeStruct((2, n_out), _f32),
        grid=(n_out // tn,),
        in_specs=[pl.BlockSpec((2, d, LANES), lambda j: (0, 0, 0)),
                  pl.BlockSpec((d, tn), lambda j: (0, j)),
                  pl.BlockSpec((1, tn), lambda j: (0, j))],
        out_specs=pl.BlockSpec((2, tn), lambda j: (0, j)),
        compiler_params=_cparams(("arbitrary",)),
        name="mod_vectors",
    )(s, w_mod, b_mod.reshape(1, n_out))


def _rms_mod(xf, g, shift, scale):
    y = xf * lax.rsqrt(jnp.mean(xf * xf, axis=-1, keepdims=True) + EPS)
    return (y * g) * (1.0 + scale) + shift


def _norm_kernel(x_ref, ctx_ref, g_ref, mod_ref, o_ref, *, n_ctx_tiles):
    i = pl.program_id(0)
    d = x_ref.shape[1]

    @pl.when(i < n_ctx_tiles)
    def _():
        o_ref[...] = _rms_mod(ctx_ref[...], g_ref[...], mod_ref[1:2, 0:d],
                              mod_ref[1:2, d:2 * d]).astype(o_ref.dtype)

    @pl.when(i >= n_ctx_tiles)
    def _():
        o_ref[...] = _rms_mod(x_ref[...], g_ref[...], mod_ref[0:1, 0:d],
                              mod_ref[0:1, d:2 * d]).astype(o_ref.dtype)


def _norm_all(x2, ctx2, g, mod):
    n, d = x2.shape
    l = ctx2.shape[0]
    tm = 256
    nct = l // tm
    return pl.pallas_call(
        functools.partial(_norm_kernel, n_ctx_tiles=nct),
        out_shape=jax.ShapeDtypeStruct((l + n, d), _bf16),
        grid=((l + n) // tm,),
        in_specs=[pl.BlockSpec((tm, d), lambda i: (jnp.maximum(i - nct, 0), 0)),
                  pl.BlockSpec((tm, d), lambda i: (jnp.minimum(i, nct - 1), 0)),
                  pl.BlockSpec((1, d), lambda i: (0, 0)),
                  pl.BlockSpec(mod.shape, lambda i: (0, 0))],
        out_specs=pl.BlockSpec((tm, d), lambda i: (i, 0)),
        compiler_params=_cparams(("arbitrary",)),
        name="adaln_norm",
    )(x2, ctx2, g.reshape(1, d), mod)


def _rope(a, cos, sin):
    lane = lax.broadcasted_iota(jnp.int32, a.shape, 1)
    first = (lane % A_HEAD_DIM) < (A_HEAD_DIM // 2)
    rot = jnp.where(first, pltpu.roll(a, LANES - 32, 1), pltpu.roll(a, 32, 1))
    return a * cos + rot * sin


def _row_block_dots(h_ref, w):
    sub = h_ref.shape[0] // 4
    blocks = [slice(r * sub, (r + 1) * sub) for r in range(4)]
    return [(rs, jnp.dot(h_ref[rs, :], w, preferred_element_type=_f32)) for rs in blocks]


def _proj_qkv_kernel(h_ref, w_ref, cos_ref, sin_ref, q_ref, k_ref, v_ref):
    scale = A_HEAD_DIM ** -0.5 * LOG2E
    for rs, acc in _row_block_dots(h_ref, w_ref[...]):
        cos = cos_ref[rs, :]
        sin = sin_ref[rs, :]
        for j in range(A_WIDTH // LANES):
            sl = slice(j * LANES, (j + 1) * LANES)
            q_ref[rs, sl] = (_rope(acc[:, sl], cos, sin) * scale).astype(q_ref.dtype)
        k_ref[rs, :] = _rope(acc[:, A_WIDTH:A_WIDTH + LANES], cos, sin).astype(k_ref.dtype)
        v_ref[rs, :] = acc[:, A_WIDTH + LANES:A_WIDTH + 2 * LANES].astype(v_ref.dtype)


def _proj_heads_kernel(h_ref, w_ref, o_ref, *, act):
    for rs, acc in _row_block_dots(h_ref, w_ref[...]):
        if act == "silu":
            acc = acc * _sigmoid(acc)
        for hh in range(o_ref.shape[0]):
            o_ref[hh, rs, :] = acc[:, hh * LANES:(hh + 1) * LANES].astype(o_ref.dtype)


def _proj_gate_kernel(h_ref, w_ref, lb_ref, lf_ref, kk_ref):
    lb = lb_ref[0]
    for rs, z in _row_block_dots(h_ref, w_ref[0]):
        sg = _sigmoid(z)
        logf = jnp.log2(lb + (1.0 - lb) * sg)
        kk = (1.0 - lb) * _sigmoid(-z)
        for hh in range(lf_ref.shape[1]):
            sl = slice(hh * LANES, (hh + 1) * LANES)
            lf_ref[0, hh, rs, :] = logf[:, sl]
            kk_ref[0, hh, rs, :] = kk[:, sl].astype(kk_ref.dtype)


def _proj_sigmoid_kernel(h_ref, w_ref, o_ref):
    for rs, acc in _row_block_dots(h_ref, w_ref[...]):
        o_ref[rs, :] = _sigmoid(acc).astype(o_ref.dtype)


def _row_tile(t):
    for tm in (1280, 640, 256, 128):
        if t % tm == 0:
            return tm
    raise ValueError(t)


def _proj_qkv(h, w, cos_t, sin_t):
    t, d = h.shape
    tm = _row_tile(t)
    ncol = w.shape[1]
    return pl.pallas_call(
        _proj_qkv_kernel,
        out_shape=(jax.ShapeDtypeStruct((t, A_WIDTH), _bf16),
                   jax.ShapeDtypeStruct((t, A_KV_WIDTH), _bf16),
                   jax.ShapeDtypeStruct((t, A_KV_WIDTH), _bf16)),
        grid=(t // tm,),
        in_specs=[pl.BlockSpec((tm, d), lambda i: (i, 0)),
                  pl.BlockSpec((d, ncol), lambda i: (0, 0)),
                  pl.BlockSpec((tm, LANES), lambda i: (i, 0)),
                  pl.BlockSpec((tm, LANES), lambda i: (i, 0))],
        out_specs=(pl.BlockSpec((tm, A_WIDTH), lambda i: (i, 0)),
                   pl.BlockSpec((tm, A_KV_WIDTH), lambda i: (i, 0)),
                   pl.BlockSpec((tm, A_KV_WIDTH), lambda i: (i, 0))),
        compiler_params=_cparams(("arbitrary",)),
        name="proj_qkv",
    )(h, w, cos_t, sin_t)


def _proj_heads(h, w, act):
    t, d = h.shape
    tm = _row_tile(t)
    tn = 1024
    nh = tn // LANES
    return pl.pallas_call(
        functools.partial(_proj_heads_kernel, act=act),
        out_shape=jax.ShapeDtypeStruct((w.shape[1] // LANES, t, LANES), _bf16),
        grid=(w.shape[1] // tn, t // tm),
        in_specs=[pl.BlockSpec((tm, d), lambda j, i: (i, 0)),
                  pl.BlockSpec((d, tn), lambda j, i: (0, j))],
        out_specs=pl.BlockSpec((nh, tm, LANES), lambda j, i: (j, i, 0)),
        compiler_params=_cparams(("arbitrary", "arbitrary")),
        name="proj_heads_" + act,
    )(h, w)


def _proj_gate(h, w2, lb2):
    t, d = h.shape
    tm = _row_tile(t)
    tn = 1024
    nh = tn // LANES
    ncb = w2.shape[2] // tn
    out_sds = lambda dt: jax.ShapeDtypeStruct((2, w2.shape[2] // LANES, t, LANES), dt)
    return pl.pallas_call(
        _proj_gate_kernel,
        out_shape=(out_sds(_f32), out_sds(_bf16)),
        grid=(2, ncb, t // tm),
        in_specs=[pl.BlockSpec((tm, d), lambda r, j, i: (i, 0)),
                  pl.BlockSpec((1, d, tn), lambda r, j, i: (r, 0, j)),
                  pl.BlockSpec((1, 1, tn), lambda r, j, i: (r, 0, j))],
        out_specs=(pl.BlockSpec((1, nh, tm, LANES), lambda r, j, i: (r, j, i, 0)),
                   pl.BlockSpec((1, nh, tm, LANES), lambda r, j, i: (r, j, i, 0))),
        compiler_params=_cparams(("arbitrary", "arbitrary", "arbitrary")),
        name="proj_gate",
    )(h, w2, lb2)


def _proj_sigmoid(h, w):
    t, d = h.shape
    tm = _row_tile(t)
    tn = 1024
    return pl.pallas_call(
        _proj_sigmoid_kernel,
        out_shape=jax.ShapeDtypeStruct((t, w.shape[1]), _bf16),
        grid=(w.shape[1] // tn, t // tm),
        in_specs=[pl.BlockSpec((tm, d), lambda j, i: (i, 0)),
                  pl.BlockSpec((d, tn), lambda j, i: (0, j))],
        out_specs=pl.BlockSpec((tm, tn), lambda j, i: (i, j)),
        compiler_params=_cparams(("arbitrary", "arbitrary")),
        name="proj_sigmoid",
    )(h, w)


def _attn_kernel(q_ref, kp_ref, kc_ref, kn_ref, vp_ref, vc_ref, vn_ref,
                 kx_ref, vx_ref, bias_ref, sink_ref, o_ref):
    hd = A_HEAD_DIM
    blk = A_BLOCK
    nt = (((1,), (1,)), ((), ()))
    n_ctx = kx_ref.shape[0]
    ones = jnp.ones((n_ctx + 3 * blk, LANES), _bf16)
    units = [(g, h0) for g in range(A_KV_HEADS) for h0 in range(0, A_GROUP, ATTN_UNIT_HEADS)]
    rows = ATTN_UNIT_HEADS * blk
    scores = []
    for g, h0 in units:
        gs = slice(g * hd, (g + 1) * hd)
        heads = [g * A_GROUP + h0 + h for h in range(ATTN_UNIT_HEADS)]
        qg = jnp.concatenate([q_ref[:, hh * hd:(hh + 1) * hd] for hh in heads], axis=0)
        k_pc = jnp.concatenate([kp_ref[:, gs], kc_ref[:, gs]], axis=0)
        s_x = lax.dot_general(qg, kx_ref[:, gs], nt, preferred_element_type=_f32)
        s_pc = lax.dot_general(qg, k_pc, nt, preferred_element_type=_f32)
        s_n = lax.dot_general(qg, kn_ref[:, gs], nt, preferred_element_type=_f32)
        scores.append((s_x, s_pc, s_n))
    probs = []
    for (g, h0), (s_x, s_pc, s_n) in zip(units, scores):
        s_n = s_n + bias_ref[0, 1, 0:rows, :]
        s_p = s_pc[:, 0:blk] + bias_ref[0, 0, 0:rows, :]
        s_c = s_pc[:, blk:2 * blk]
        mx = jnp.maximum(jnp.maximum(s_p, s_c), s_n)
        for j in range(n_ctx // LANES):
            mx = jnp.maximum(mx, s_x[:, j * LANES:(j + 1) * LANES])
        sink = sink_ref[g, h0 * blk:h0 * blk + rows, :]
        m = jnp.maximum(jnp.max(mx, axis=-1, keepdims=True), sink)
        p = jnp.concatenate([jnp.exp2(s_x - m), jnp.exp2(s_p - m), jnp.exp2(s_c - m), jnp.exp2(s_n - m)],
                            axis=1).astype(_bf16)
        probs.append((p, jnp.exp2(sink - m)))
    for (g, h0), (p, sink_term) in zip(units, probs):
        gs = slice(g * hd, (g + 1) * hd)
        v_all = jnp.concatenate([vx_ref[:, gs], vp_ref[:, gs], vc_ref[:, gs], vn_ref[:, gs]], axis=0)
        o = jnp.dot(p, v_all, preferred_element_type=_f32)
        den = jnp.dot(p, ones, preferred_element_type=_f32)
        o = o / (den[:, 0:hd] + sink_term)
        for h in range(ATTN_UNIT_HEADS):
            c0 = (g * A_GROUP + h0 + h) * hd
            o_ref[:, c0:c0 + hd] = o[h * blk:(h + 1) * blk, :].astype(o_ref.dtype)


def _attention(q, k, v, sinks, n, l):
    blk = A_BLOCK
    nblk = n // blk
    off = l // blk
    rows = A_GROUP * blk
    r = np.arange(rows)[:, None] % blk
    j = np.arange(blk)[None, :]
    keep_prev = (j - blk - r) >= -WINDOW
    keep_next = (j + blk - r) <= WINDOW
    none = np.zeros_like(keep_prev)
    variants = [(none, keep_next), (keep_prev, keep_next), (keep_prev, none)]
    bias = jnp.asarray(np.where(np.array(variants), 0.0, NEG).astype(np.float32))
    sink_col = jnp.repeat(sinks.astype(_f32).reshape(A_KV_HEADS, A_GROUP) * LOG2E, blk, axis=1)
    sink_col = sink_col.reshape(A_KV_HEADS, rows, 1)

    def bias_idx(i):
        return (jnp.where(i == 0, 0, jnp.where(i == nblk - 1, 2, 1)), 0, 0, 0)

    kv_spec = lambda f: pl.BlockSpec((blk, A_KV_WIDTH), f)
    prev = lambda i: (jnp.maximum(i - 1, 0) + off, 0)
    cur = lambda i: (i + off, 0)
    nxt = lambda i: (jnp.minimum(i + 1, nblk - 1) + off, 0)
    return pl.pallas_call(
        _attn_kernel,
        out_shape=jax.ShapeDtypeStruct((n, A_WIDTH), _bf16),
        grid=(nblk,),
        in_specs=[pl.BlockSpec((blk, A_WIDTH), cur),
                  kv_spec(prev), kv_spec(cur), kv_spec(nxt),
                  kv_spec(prev), kv_spec(cur), kv_spec(nxt),
                  pl.BlockSpec((l, A_KV_WIDTH), lambda i: (0, 0)),
                  pl.BlockSpec((l, A_KV_WIDTH), lambda i: (0, 0)),
                  pl.BlockSpec((1, 2, rows, blk), bias_idx),
                  pl.BlockSpec((A_KV_HEADS, rows, 1), lambda i: (0, 0, 0))],
        out_specs=pl.BlockSpec((blk, A_WIDTH), lambda i: (i, 0)),
        compiler_params=_cparams(("arbitrary",)),
        name="window_attention",
    )(q, k, k, k, v, v, v, k, v, bias, sink_col)


def _split3(x):
    hi = x.astype(_bf16)
    r1 = x - hi.astype(_f32)
    mid = r1.astype(_bf16)
    lo = (r1 - mid.astype(_f32)).astype(_bf16)
    return hi, mid, lo


def _hgrn_intra_scores(q, k, b, emat, reverse):
    c = HG_CHUNK
    nsb = c // SUB
    nt = (((1,), (1,)), ((), ()))
    qb = (q * jnp.exp2(b)).astype(_bf16)
    b_tot = b[0:1, :] if reverse else b[c - 1:c, :]
    kend = (k * jnp.exp2(b_tot - b)).astype(_bf16)

    zero_row = jnp.zeros((1, LANES), _f32)
    zero_blk = jnp.zeros((SUB, LANES), _f32)
    y_rows, z_rows, w_cols = [], [], []
    for ib in range(nsb):
        r0 = ib * SUB
        bi = b[r0:r0 + SUB]
        qi = q[r0:r0 + SUB]
        pieces = []
        for s in range(SUB):
            bs = b[r0 + s:r0 + s + 1]
            ks = k[r0 + s:r0 + s + 1]
            pieces.append(qi * ks * jnp.exp2(jnp.minimum(bi - bs, 0.0)))
        y_rows.append(jnp.concatenate(pieces, axis=1))
        if reverse:
            lo_r, hi_r = r0 + SUB, c
            ref_b = b[r0 + SUB:r0 + SUB + 1] if ib < nsb - 1 else zero_row
        else:
            lo_r, hi_r = 0, r0
            ref_b = b[r0 - 1:r0] if ib > 0 else zero_row
        if hi_r > lo_r:
            qp = qi * jnp.exp2(bi - ref_b)
            kp = [k[lo_r:hi_r] * jnp.exp2(ref_b - b[lo_r:hi_r])]
            if lo_r > 0:
                kp.insert(0, jnp.zeros((lo_r, LANES), _f32))
            if hi_r < c:
                kp.append(jnp.zeros((c - hi_r, LANES), _f32))
            w_cols.append(jnp.concatenate(kp, axis=0) if len(kp) > 1 else kp[0])
        else:
            qp = zero_blk
            w_cols.append(jnp.zeros((c, LANES), _f32))
        z_rows.append(jnp.concatenate([qp if jb == ib else zero_blk for jb in range(nsb)], axis=1))
    y = jnp.concatenate(y_rows, axis=0).astype(_bf16)
    z = jnp.concatenate(z_rows, axis=0).astype(_bf16)
    w = jnp.concatenate(w_cols, axis=1).astype(_bf16)
    a_diag = jnp.dot(y, emat, preferred_element_type=_f32)
    a_off = lax.dot_general(z, w, nt, preferred_element_type=_f32)
    return a_diag, a_off, qb, kend, jnp.exp2(b_tot)


def _hgrn_intra_kernel(q_ref, k_ref, lf_ref, emat_ref, a_ref, qb_ref, ke_ref, dt_ref, *, reverse):
    c = HG_CHUNK
    nch = HG_BLOCK // c
    nh = q_ref.shape[0]
    ri = lax.broadcasted_iota(jnp.int32, (c, c), 0)
    ci = lax.broadcasted_iota(jnp.int32, (c, c), 1)
    tri = jnp.where((ci >= ri) if reverse else (ci <= ri), 1.0, 0.0).astype(_bf16)
    dmask = (ci // SUB == ri // SUB) & ((ci >= ri) if reverse else (ci <= ri))
    emat = emat_ref[...]

    def body(cc, carry):
        rows = pl.ds(pl.multiple_of(cc * c, c), c)
        lf_all = jnp.concatenate([lf_ref[0, h, rows, :] for h in range(nh)], axis=1)
        hi, mid, lo = _split3(lf_all)
        b_all = (jnp.dot(tri, hi, preferred_element_type=_f32)
                 + jnp.dot(tri, mid, preferred_element_type=_f32)
                 + jnp.dot(tri, lo, preferred_element_type=_f32))
        for h in range(nh):
            a_diag, a_off, qb, kend, dtot = _hgrn_intra_scores(
                q_ref[h, rows, :].astype(_f32), k_ref[0, h, rows, :].astype(_f32),
                b_all[:, h * LANES:(h + 1) * LANES], emat, reverse)
            qb_ref[h, rows, :] = qb
            ke_ref[h, rows, :] = kend
            dt_ref[cc, h:h + 1, :] = dtot
            a = a_off + jnp.where(dmask, a_diag[:, 0:c], 0.0)
            a_ref[h, rows, :] = jnp.concatenate([a, jnp.zeros_like(a)], axis=1).astype(a_ref.dtype)
        return carry

    lax.fori_loop(0, nch, body, 0)


def _hgrn_intra(rq, kk, lf, direction):
    nh, t, _ = rq.shape
    br = HG_BLOCK
    nch = br // HG_CHUNK
    reverse = direction == 1
    spec3 = pl.BlockSpec((nh, br, LANES), lambda s: (0, s, 0))
    spec4 = pl.BlockSpec((1, nh, br, LANES), lambda s: (direction, 0, s, 0))
    sds = jax.ShapeDtypeStruct((nh, t, LANES), _bf16)
    emat = (np.arange(SUB * LANES)[:, None] // LANES == np.arange(LANES)[None, :] % SUB)
    emat = jnp.asarray(emat.astype(np.float32), dtype=_bf16)
    return pl.pallas_call(
        functools.partial(_hgrn_intra_kernel, reverse=reverse),
        out_shape=(sds, sds, sds, jax.ShapeDtypeStruct((t // HG_CHUNK, nh, LANES), _f32)),
        grid=(t // br,),
        in_specs=[spec3, spec4, spec4, pl.BlockSpec(emat.shape, lambda s: (0, 0))],
        out_specs=(spec3, spec3, spec3, pl.BlockSpec((nch, nh, LANES), lambda s: (s, 0, 0))),
        compiler_params=_cparams(("arbitrary",)),
        name="hgrn2_intra_" + ("bwd" if reverse else "fwd"),
    )(rq, kk, lf, emat)


def _hgrn_state_kernel(af_ref, qf_ref, kf_ref, df_ref, vf_ref, ab_ref, qbk_ref, kb_ref, db_ref, vb_ref,
                       outf_ref, outb_ref, st_ref):
    c = HG_CHUNK
    nch = HG_BLOCK // c
    nh = qf_ref.shape[0]
    nt = (((1,), (1,)), ((), ()))
    tn = (((0,), (0,)), ((), ()))

    @pl.when(pl.program_id(0) == 0)
    def _():
        st_ref[...] = jnp.zeros(st_ref.shape, st_ref.dtype)

    dirs = ((0, False, af_ref, qf_ref, kf_ref, df_ref, vf_ref, outf_ref),
            (1, True, ab_ref, qbk_ref, kb_ref, db_ref, vb_ref, outb_ref))

    def body(cc, carry):
        for di, reverse, a_ref, q_ref, k_ref, d_ref, v_ref, out_ref in dirs:
            chunk = (nch - 1 - cc) if reverse else cc
            rows = pl.ds(pl.multiple_of(chunk * c, c), c)
            for h in range(nh):
                st = st_ref[di, h]
                v = v_ref[h, rows, :]
                o = (jnp.dot(a_ref[h, rows, 0:c], v, preferred_element_type=_f32)
                     + lax.dot_general(q_ref[h, rows, :], st.astype(_bf16), nt, preferred_element_type=_f32))
                out_ref[h, rows, :] = o.astype(out_ref.dtype)
                upd = lax.dot_general(v, k_ref[h, rows, :], tn, preferred_element_type=_f32)
                st_ref[di, h] = st * d_ref[chunk, pl.ds(h, 1), :] + upd
        return carry

    lax.fori_loop(0, nch, body, 0)


def _hgrn_state(intra_f, intra_b, rv, l):
    a_f, qb_f, ke_f, dt_f = intra_f
    a_b, qb_b, ke_b, dt_b = intra_b
    nh, t, _ = rv.shape
    br = HG_BLOCK
    nch = br // HG_CHUNK
    nb = t // br
    nc = l // br

    def blk_b(s):
        return jnp.where(s < nc, nc - 1 - s, nb - 1 - (s - nc))

    def specs(blk):
        s3 = pl.BlockSpec((nh, br, LANES), lambda s: (0, blk(s), 0))
        sd = pl.BlockSpec((nch, nh, LANES), lambda s: (blk(s), 0, 0))
        return [s3, s3, s3, sd, s3]

    fwd = specs(lambda s: s)
    bwd = specs(blk_b)
    return pl.pallas_call(
        _hgrn_state_kernel,
        out_shape=(jax.ShapeDtypeStruct((nh, t, LANES), _bf16),
                   jax.ShapeDtypeStruct((nh, t, LANES), _bf16)),
        grid=(nb,),
        in_specs=fwd + bwd,
        out_specs=(fwd[0], bwd[0]),
        scratch_shapes=[pltpu.VMEM((2, nh, B_DV, B_DK), _f32)],
        compiler_params=_cparams(("arbitrary",)),
        name="hgrn2_state",
    )(a_f, qb_f, ke_f, dt_f, rv, a_b, qb_b, ke_b, dt_b, rv)


def _merge_kernel(ya_ref, of_ref, ob_ref, rg_ref, ga_ref, gb_ref, x_ref, wa_ref, wb_ref, wo_ref,
                  wr_ref, br_ref, hg_ref, g2_ref, mod_ref,
                  x1_ref, h2_ref, route_ref, cnt_ref, run_ref):
    i = pl.program_id(0)
    tm, d = x_ref.shape

    @pl.when(i == 0)
    def _():
        run_ref[...] = jnp.zeros(run_ref.shape, run_ref.dtype)

    parts = []
    for h in range(B_HEADS):
        o = of_ref[h].astype(_f32) + ob_ref[h].astype(_f32)
        o = o * lax.rsqrt(jnp.mean(o * o, axis=-1, keepdims=True) + EPS)
        o = o * hg_ref[:, h * LANES:(h + 1) * LANES]
        parts.append((o * rg_ref[h].astype(_f32)).astype(_bf16))
    yb = jnp.concatenate(parts, axis=1)
    z0 = jnp.dot(ya_ref[...], wa_ref[...], preferred_element_type=_f32)
    z1 = jnp.dot(yb, wb_ref[...], preferred_element_type=_f32)
    merged = ga_ref[...].astype(_f32) * z0 + gb_ref[...].astype(_f32) * z1
    y = jnp.dot(merged.astype(_bf16), wo_ref[...], preferred_element_type=_f32)
    x1 = x_ref[...] + mod_ref[0:1, 2 * d:3 * d] * y
    x1_ref[...] = x1
    h2 = _rms_mod(x1, g2_ref[...], mod_ref[0:1, 3 * d:4 * d], mod_ref[0:1, 4 * d:5 * d])
    h2_ref[...] = _pack_bf16_pair(h2[:, 0:d // 2], h2[:, d // 2:d])

    hh = h2.astype(_bf16)
    hl = (h2 - hh.astype(_f32)).astype(_bf16)
    both = jnp.dot(hh, wr_ref[...], preferred_element_type=_f32)
    logits = (both[:, 0:LANES] + both[:, LANES:2 * LANES]
              + jnp.dot(hl, wr_ref[:, 0:LANES], preferred_element_type=_f32)) + br_ref[...]
    lane = lax.broadcasted_iota(jnp.int32, (tm, LANES), 1)
    work = jnp.where(lane < N_EXPERTS, logits, NEG)
    vals, idxs = [], []
    onehot = jnp.zeros((tm, LANES), _f32)
    for _ in range(TOP_K):
        m = jnp.max(work, axis=-1, keepdims=True)
        idx = jnp.min(jnp.where(work == m, lane, LANES), axis=-1, keepdims=True)
        sel = lane == idx
        vals.append(m)
        idxs.append(idx)
        onehot = jnp.where(sel, 1.0, onehot)
        work = jnp.where(sel, NEG, work)
    es = [jnp.exp(vv - vals[0]) for vv in vals]
    tot = es[0] + es[1] + es[2] + es[3]
    ri = lax.broadcasted_iota(jnp.int32, (tm, tm), 0)
    ci = lax.broadcasted_iota(jnp.int32, (tm, tm), 1)
    ltri = jnp.where(ci < ri, 1.0, 0.0).astype(_bf16)
    prefix = jnp.dot(ltri, onehot.astype(_bf16), preferred_element_type=_f32) + run_ref[...]
    route = jnp.zeros((tm, LANES), _f32)
    for j in range(TOP_K):
        rank = jnp.sum(jnp.where(lane == idxs[j], prefix, 0.0), axis=-1, keepdims=True)
        route = jnp.where(lane == j, idxs[j].astype(_f32), route)
        route = jnp.where(lane == TOP_K + j, es[j] / tot, route)
        route = jnp.where(lane == 2 * TOP_K + j, rank, route)
    route_ref[...] = route
    run_new = run_ref[...] + jnp.sum(onehot, axis=0, keepdims=True)
    run_ref[...] = run_new
    cnt_ref[...] = run_new


def _merge_route(ya, o_f, o_b, rg, gates, x2, wa, wb, wo, wr_cat, b_router, hg, g2, mod, l):
    n, d = x2.shape
    tm = 256
    off = l // tm
    const = lambda shape: pl.BlockSpec(shape, lambda i: (0,) * len(shape),
                                       pipeline_mode=pl.Buffered(1))
    head_spec = pl.BlockSpec((B_HEADS, tm, LANES), lambda i: (0, i + off, 0))
    return pl.pallas_call(
        _merge_kernel,
        out_shape=(jax.ShapeDtypeStruct((n, d), _f32),
                   jax.ShapeDtypeStruct((n, d // 2), jnp.uint32),
                   jax.ShapeDtypeStruct((n, LANES), _f32),
                   jax.ShapeDtypeStruct((1, LANES), _f32)),
        grid=(n // tm,),
        in_specs=[pl.BlockSpec((tm, A_WIDTH), lambda i: (i, 0)),
                  head_spec, head_spec, head_spec,
                  pl.BlockSpec((tm, d), lambda i: (i + off, 0)),
                  pl.BlockSpec((tm, d), lambda i: (i + off, 1)),
                  pl.BlockSpec((tm, d), lambda i: (i, 0)),
                  const(wa.shape), const(wb.shape), const(wo.shape),
                  const(wr_cat.shape), const((1, LANES)),
                  const((1, B_WIDTH)), const((1, d)), const(mod.shape)],
        out_specs=(pl.BlockSpec((tm, d), lambda i: (i, 0)),
                   pl.BlockSpec((tm, d // 2), lambda i: (i, 0)),
                   pl.BlockSpec((tm, LANES), lambda i: (i, 0)),
                   pl.BlockSpec((1, LANES), lambda i: (0, 0))),
        scratch_shapes=[pltpu.VMEM((1, LANES), _f32)],
        compiler_params=_cparams(("arbitrary",)),
        name="merge_route",
    )(ya, o_f, o_b, rg, gates, gates, x2, wa, wb, wo, wr_cat, b_router, hg, g2, mod)


def _zero_fill_tiles(zero_ref, dst_ref, first, last, sem, cols=(None,)):
    tm = MOE_TM

    def copy(t, col):
        rows = pl.ds(pl.multiple_of(t * tm, tm), tm)
        dst = dst_ref.at[rows] if col is None else dst_ref.at[rows, col]
        return pltpu.make_async_copy(zero_ref, dst, sem)

    def start(t, carry):
        for col in cols:
            copy(t, col).start()
        return carry

    def wait(t, carry):
        for col in cols:
            copy(t, col).wait()
        return carry

    lax.fori_loop(first, last, start, 0)
    lax.fori_loop(first, last, wait, 0)


def _dispatch_kernel(lt_ref, nt_ref, dest_ref, h2_ref, xs_ref, zero_ref, sem, zsem):
    tm = h2_ref.shape[0]

    @pl.when(pl.program_id(0) == 0)
    def _():
        zero_ref[...] = jnp.zeros(zero_ref.shape, zero_ref.dtype)
        for e in range(N_EXPERTS):
            @pl.when(nt_ref[e] > 0)
            def _():
                pltpu.make_async_copy(zero_ref, xs_ref.at[pl.ds(lt_ref[e] * MOE_TM, MOE_TM)], zsem).start()
        for e in range(N_EXPERTS):
            @pl.when(nt_ref[e] > 0)
            def _():
                pltpu.make_async_copy(zero_ref, xs_ref.at[pl.ds(lt_ref[e] * MOE_TM, MOE_TM)], zsem).wait()
        _zero_fill_tiles(zero_ref, xs_ref, lt_ref[N_EXPERTS - 1] + 1, xs_ref.shape[0] // MOE_TM, zsem)

    def issue(t, carry):
        for j in range(TOP_K):
            dst = dest_ref[t * TOP_K + j]
            pltpu.make_async_copy(h2_ref.at[pl.ds(t, 1)], xs_ref.at[pl.ds(dst, 1)], sem).start()
        return carry

    lax.fori_loop(0, tm, issue, 0)
    for _ in range(TOP_K):
        pltpu.make_async_copy(h2_ref, xs_ref.at[pl.ds(0, tm)], sem).wait()


def _dispatch(h2, dest_flat, last_tile, ntile, n_slots):
    n, d = h2.shape
    tm = 256
    return pl.pallas_call(
        _dispatch_kernel,
        out_shape=jax.ShapeDtypeStruct((n_slots, d), h2.dtype),
        grid_spec=pltpu.PrefetchScalarGridSpec(
            num_scalar_prefetch=2,
            grid=(n // tm,),
            in_specs=[pl.BlockSpec((tm * TOP_K,), lambda i, lt, nt: (i,), memory_space=pltpu.SMEM),
                      pl.BlockSpec((tm, d), lambda i, lt, nt: (i, 0))],
            out_specs=pl.BlockSpec(memory_space=pl.ANY),
            scratch_shapes=[pltpu.VMEM((MOE_TM, d), h2.dtype), pltpu.SemaphoreType.DMA,
                            pltpu.SemaphoreType.DMA]),
        compiler_params=_cparams(("arbitrary",)),
        name="moe_dispatch",
    )(last_tile, ntile, dest_flat, h2)


def _cast_rows(src_ref, dst_ref):
    rows = dst_ref.shape[0]
    step = 256

    def body(i, carry):
        r = pl.ds(pl.multiple_of(i * step, step), step)
        dst_ref[r, :] = src_ref[r, :].astype(dst_ref.dtype)
        return carry

    lax.fori_loop(0, rows // step, body, 0)


def _expert_stream(ts_ref, nt_ref, weights, in_copy, out_copy, compute):
    nb = MOE_NBUF
    n_exp = nt_ref.shape[0]
    total = ts_ref[n_exp - 1] + nt_ref[n_exp - 1]
    for k in range(nb - 1):
        @pl.when(total > k)
        def _():
            in_copy(k, k).start(priority=1)
    for fetch, _, _ in weights:
        fetch(0, 0).start()

    def expert_body(e, carry):
        wslot = lax.rem(e, 2)
        for fetch, _, _ in weights:
            fetch(e, wslot).wait()

        @pl.when(e + 1 < n_exp)
        def _():
            for fetch, _, _ in weights:
                fetch(e + 1, 1 - wslot).start()

        for _, stage_ref, bf_ref in weights:
            _cast_rows(stage_ref.at[wslot], bf_ref)
        t0 = ts_ref[e]

        def tile_body(t, c2):
            g = t0 + t
            slot = lax.rem(g, nb)
            in_copy(g, slot).wait()

            @pl.when(g + nb - 1 < total)
            def _():
                in_copy(g + nb - 1, lax.rem(g + nb - 1, nb)).start(priority=1)

            @pl.when(g >= nb)
            def _():
                out_copy(g - nb, slot).wait()

            compute(e, slot)
            out_copy(g, slot).start(priority=1)
            return c2

        lax.fori_loop(0, nt_ref[e], tile_body, 0)
        return carry

    lax.fori_loop(0, n_exp, expert_body, 0)
    for k in range(1, nb + 1):
        @pl.when(total >= k)
        def _():
            out_copy(total - k, lax.rem(total - k, nb)).wait()
    return total


def _tile_rows(g):
    return pl.ds(pl.multiple_of(g * MOE_TM, MOE_TM), MOE_TM)


def _moe_up_kernel(ts_ref, nt_ref, xs_ref, wg_ref, wu_ref, bg_ref, bu_ref, act_ref,
                   wgs_ref, wus_ref, wgb_ref, wub_ref, xbuf, obuf, sem_wg, sem_wu, sem_in, sem_out):
    c = pl.program_id(0)
    tm = MOE_TM
    fc = wgb_ref.shape[1]
    col = pl.ds(pl.multiple_of(c * fc, fc), fc)

    def in_copy(g, slot):
        return pltpu.make_async_copy(xs_ref.at[_tile_rows(g)], xbuf.at[slot], sem_in.at[slot])

    def out_copy(g, slot):
        return pltpu.make_async_copy(obuf.at[slot], act_ref.at[_tile_rows(g), col], sem_out.at[slot])

    def fetch_g(e, slot):
        return pltpu.make_async_copy(wg_ref.at[e, :, col], wgs_ref.at[slot], sem_wg.at[slot])

    def fetch_u(e, slot):
        return pltpu.make_async_copy(wu_ref.at[e, :, col], wus_ref.at[slot], sem_wu.at[slot])

    def compute(e, slot):
        x_lo, x_hi = _unpack_bf16_pair(xbuf[slot])
        x_lo, x_hi = x_lo.astype(_bf16), x_hi.astype(_bf16)
        half = x_lo.shape[1]

        def proj(w_ref, b_ref):
            return (jnp.dot(x_lo, w_ref[0:half, :], preferred_element_type=_f32)
                    + jnp.dot(x_hi, w_ref[half:2 * half, :], preferred_element_type=_f32)
                    + b_ref[pl.ds(e, 1), col])

        g = jnp.minimum(proj(wgb_ref, bg_ref), SWIGLU_LIMIT)
        u = jnp.clip(proj(wub_ref, bu_ref), -SWIGLU_LIMIT, SWIGLU_LIMIT)
        obuf[slot] = (g * _sigmoid(SWIGLU_ALPHA * g) * (u + 1.0)).astype(obuf.dtype)

    total = _expert_stream(ts_ref, nt_ref, [(fetch_g, wgs_ref, wgb_ref), (fetch_u, wus_ref, wub_ref)],
                           in_copy, out_copy, compute)
    obuf[0] = jnp.zeros(obuf.shape[1:], obuf.dtype)
    _zero_fill_tiles(obuf.at[0], act_ref, total, act_ref.shape[0] // tm, sem_out.at[0], cols=(col,))


def _moe_up(xs, wg, wu, bg, bu, tstart, ntile):
    n_slots = xs.shape[0]
    ne, d, ff = wg.shape
    fc = ff // 2
    tm = MOE_TM
    any_spec = pl.BlockSpec(memory_space=pl.ANY)
    b_spec = pl.BlockSpec((ne, ff), lambda c, ts, nt: (0, 0))
    return pl.pallas_call(
        _moe_up_kernel,
        out_shape=jax.ShapeDtypeStruct((n_slots, ff), _bf16),
        grid_spec=pltpu.PrefetchScalarGridSpec(
            num_scalar_prefetch=2,
            grid=(ff // fc,),
            in_specs=[any_spec, any_spec, any_spec, b_spec, b_spec],
            out_specs=any_spec,
            scratch_shapes=[pltpu.VMEM((2, d, fc), _f32), pltpu.VMEM((2, d, fc), _f32),
                            pltpu.VMEM((d, fc), _bf16), pltpu.VMEM((d, fc), _bf16),
                            pltpu.VMEM((MOE_NBUF, tm, xs.shape[1]), xs.dtype),
                            pltpu.VMEM((MOE_NBUF, tm, fc), _bf16),
                            pltpu.SemaphoreType.DMA((2,)), pltpu.SemaphoreType.DMA((2,)),
                            pltpu.SemaphoreType.DMA((MOE_NBUF,)), pltpu.SemaphoreType.DMA((MOE_NBUF,))]),
        compiler_params=_cparams(("arbitrary",)),
        name="moe_gate_up",
    )(tstart, ntile, xs, wg, wu, bg, bu)


def _moe_down_kernel(ts_ref, nt_ref, act_ref, wd_ref, bd_ref, ys_ref,
                     wds_ref, wdb_ref, abuf, ybuf, sem_w, sem_in, sem_out):
    tm = MOE_TM
    d = wdb_ref.shape[1]

    def in_copy(g, slot):
        return pltpu.make_async_copy(act_ref.at[_tile_rows(g)], abuf.at[slot], sem_in.at[slot])

    def out_copy(g, slot):
        return pltpu.make_async_copy(ybuf.at[slot], ys_ref.at[_tile_rows(g)], sem_out.at[slot])

    def fetch(e, slot):
        return pltpu.make_async_copy(wd_ref.at[e], wds_ref.at[slot], sem_w.at[slot])

    def compute(e, slot):
        y = jnp.dot(abuf[slot], wdb_ref[...], preferred_element_type=_f32) + bd_ref[pl.ds(e, 1), :]
        ybuf[slot] = _pack_bf16_pair(y[:, 0:d // 2], y[:, d // 2:d])

    total = _expert_stream(ts_ref, nt_ref, [(fetch, wds_ref, wdb_ref)], in_copy, out_copy, compute)
    ybuf[0] = jnp.zeros(ybuf.shape[1:], ybuf.dtype)
    _zero_fill_tiles(ybuf.at[0], ys_ref, total, ys_ref.shape[0] // tm, sem_out.at[0])


def _moe_down(act, wd, bd, tstart, ntile):
    n_slots, ff = act.shape
    ne, _, d = wd.shape
    tm = MOE_TM
    any_spec = pl.BlockSpec(memory_space=pl.ANY)
    return pl.pallas_call(
        _moe_down_kernel,
        out_shape=jax.ShapeDtypeStruct((n_slots, d // 2), jnp.uint32),
        grid_spec=pltpu.PrefetchScalarGridSpec(
            num_scalar_prefetch=2,
            grid=(1,),
            in_specs=[any_spec, any_spec, pl.BlockSpec((ne, d), lambda i, ts, nt: (0, 0))],
            out_specs=any_spec,
            scratch_shapes=[pltpu.VMEM((2, ff, d), _f32), pltpu.VMEM((ff, d), _bf16),
                            pltpu.VMEM((MOE_NBUF, tm, ff), _bf16), pltpu.VMEM((MOE_NBUF, tm, d // 2), jnp.uint32),
                            pltpu.SemaphoreType.DMA((2,)),
                            pltpu.SemaphoreType.DMA((MOE_NBUF,)), pltpu.SemaphoreType.DMA((MOE_NBUF,))]),
        compiler_params=_cparams(("arbitrary",)),
        name="moe_down",
    )(tstart, ntile, act, wd, bd)


def _combine_kernel(dcur_ref, dnxt_ref, ys_ref, route_ref, x1_ref, mod_ref, gf_ref, o_ref, buf_ref, sem):
    tm, d = x1_ref.shape
    i = pl.program_id(0)
    slot = lax.rem(i, 2)

    def gather(dest_ref, s):
        def issue(t, carry):
            for j in range(TOP_K):
                src = dest_ref[t * TOP_K + j]
                pltpu.make_async_copy(ys_ref.at[pl.ds(src, 1)], buf_ref.at[s, j, pl.ds(t, 1)], sem.at[s]).start()
            return carry

        lax.fori_loop(0, tm, issue, 0)

    @pl.when(i == 0)
    def _():
        gather(dcur_ref, slot)

    @pl.when(i + 1 < pl.num_programs(0))
    def _():
        gather(dnxt_ref, 1 - slot)

    for j in range(TOP_K):
        pltpu.make_async_copy(ys_ref.at[pl.ds(0, tm)], buf_ref.at[slot, j], sem.at[slot]).wait()
    acc_lo = jnp.zeros((tm, d // 2), _f32)
    acc_hi = jnp.zeros((tm, d // 2), _f32)
    for j in range(TOP_K):
        gate = route_ref[:, TOP_K + j:TOP_K + j + 1]
        y_lo, y_hi = _unpack_bf16_pair(buf_ref[slot, j])
        acc_lo = acc_lo + gate * y_lo
        acc_hi = acc_hi + gate * y_hi
    x2 = x1_ref[...] + mod_ref[0:1, 5 * d:6 * d] * jnp.concatenate([acc_lo, acc_hi], axis=1)
    y = x2 * lax.rsqrt(jnp.mean(x2 * x2, axis=-1, keepdims=True) + EPS)
    o_ref[...] = y * gf_ref[...]


def _combine(ys, dest_flat, route, x1, mod, gf):
    n, d = x1.shape
    tm = 256
    nsteps = n // tm
    dest_spec = lambda f: pl.BlockSpec((tm * TOP_K,), f, memory_space=pltpu.SMEM)
    return pl.pallas_call(
        _combine_kernel,
        out_shape=jax.ShapeDtypeStruct((n, d), _f32),
        grid=(nsteps,),
        in_specs=[dest_spec(lambda i: (i,)), dest_spec(lambda i: (jnp.minimum(i + 1, nsteps - 1),)),
                  pl.BlockSpec(memory_space=pl.ANY),
                  pl.BlockSpec((tm, LANES), lambda i: (i, 0)),
                  pl.BlockSpec((tm, d), lambda i: (i, 0)),
                  pl.BlockSpec(mod.shape, lambda i: (0, 0)),
                  pl.BlockSpec((1, d), lambda i: (0, 0))],
        out_specs=pl.BlockSpec((tm, d), lambda i: (i, 0)),
        scratch_shapes=[pltpu.VMEM((2, TOP_K, tm, d // 2), jnp.uint32), pltpu.SemaphoreType.DMA((2,))],
        compiler_params=_cparams(("arbitrary",)),
        name="moe_combine",
    )(dest_flat, dest_flat, ys, route, x1, mod, gf.reshape(1, d))


def _rope_tables(n, l):
    rows = n // GRID_W
    inv_freq = ROPE_BASE ** (-jnp.arange(ROPE_PAIRS_PER_AXIS, dtype=_f32) / ROPE_PAIRS_PER_AXIS)
    ang_r = jnp.arange(rows, dtype=_f32)[:, None] * inv_freq
    ang_c = jnp.arange(GRID_W, dtype=_f32)[:, None] * inv_freq
    rep = lambda t_r, t_c: jnp.concatenate(
        [jnp.repeat(t_r, GRID_W, axis=0), jnp.tile(t_c, (rows, 1))], axis=-1)
    cos, sin = rep(jnp.cos(ang_r), jnp.cos(ang_c)), rep(jnp.sin(ang_r), jnp.sin(ang_c))
    cos_t = jnp.tile(cos, (1, LANES // 32))
    sin_t = jnp.tile(jnp.concatenate([-sin, sin], axis=-1), (1, LANES // A_HEAD_DIM))
    cos_t = jnp.concatenate([jnp.ones((l, LANES), _f32), cos_t], axis=0)
    sin_t = jnp.concatenate([jnp.zeros((l, LANES), _f32), sin_t], axis=0)
    return cos_t, sin_t


def _routing_tables(route, counts):
    tm = MOE_TM
    idx = route[:, 0:TOP_K].astype(jnp.int32)
    rank = route[:, 2 * TOP_K:3 * TOP_K].astype(jnp.int32)
    cnt = counts[0, :N_EXPERTS].astype(jnp.int32)
    ntile = (cnt + tm - 1) // tm
    tstart = jnp.cumsum(ntile) - ntile
    slot0 = tstart * tm
    onehot = idx[:, :, None] == jnp.arange(N_EXPERTS, dtype=jnp.int32)
    dest = jnp.sum(jnp.where(onehot, slot0, 0), axis=-1) + rank
    return dest.reshape(-1), tstart, ntile


def kernel(x, c, ctx, c_ctx, norm_mix_g, norm_ffn_g, w_mod, b_mod, w_in, attn_sinks,
           lb_fwd_logits, lb_bwd_logits, hgrn_norm_g, w_branch, w_out, w_router, b_router,
           w_e_gate, b_e_gate, w_e_up, b_e_up, w_e_down, b_e_down, final_norm_g):
    bsz, n, d = x.shape
    l = ctx.shape[1]
    assert bsz == 1 and d == D_MODEL and norm_mix_g.shape[0] == 1
    assert n % HG_BLOCK == 0 and l % HG_BLOCK == 0 and n // A_BLOCK >= 2
    x2 = x.reshape(n, d)
    ctx2 = ctx.reshape(l, d)

    mod = _mod_vectors(c, c_ctx, w_mod[0], b_mod[0])
    h_all = _norm_all(x2, ctx2, norm_mix_g[0], mod)

    w = w_in[0].astype(_bf16)
    c0 = 0
    segs = []
    for width in (A_WIDTH + 2 * A_KV_WIDTH, B_WIDTH, B_WIDTH, B_WIDTH, B_WIDTH, B_WIDTH, 2 * d):
        segs.append(w[:, c0:c0 + width])
        c0 += width
    w_qkv, w_rq, w_zf, w_zb, w_rv, w_rg, w_gates = segs
    cos_t, sin_t = _rope_tables(n, l)
    q, k, v = _proj_qkv(h_all, w_qkv, cos_t, sin_t)
    rq = _proj_heads(h_all, w_rq, "silu")
    rv = _proj_heads(h_all, w_rv, "none")
    rg = _proj_heads(h_all, w_rg, "silu")
    lb_f = jax.nn.softmax(lb_fwd_logits.astype(_f32), axis=0)[0]
    lb_b = jax.nn.softmax(lb_bwd_logits.astype(_f32), axis=0)[0]
    lb2 = jnp.stack([lb_f, lb_b]).reshape(2, 1, B_WIDTH)
    lf, kk = _proj_gate(h_all, jnp.stack([w_zf, w_zb]), lb2)
    gates = _proj_sigmoid(h_all, w_gates)

    ya = _attention(q, k, v, attn_sinks[0], n, l)
    o_f, o_b = _hgrn_state(_hgrn_intra(rq, kk, lf, 0), _hgrn_intra(rq, kk, lf, 1), rv, l)

    wr = jnp.zeros((d, LANES), _f32).at[:, :N_EXPERTS].set(w_router[0].astype(_f32))
    wr_hi = wr.astype(_bf16)
    wr_lo = (wr - wr_hi.astype(_f32)).astype(_bf16)
    br = jnp.zeros((1, LANES), _f32).at[0, :N_EXPERTS].set(b_router[0].astype(_f32))
    x1, h2, route, counts = _merge_route(
        ya, o_f, o_b, rg, gates, x2,
        w_branch[0, 0].astype(_bf16), w_branch[0, 1].astype(_bf16), w_out[0].astype(_bf16),
        jnp.concatenate([wr_hi, wr_lo], axis=1), br, hgrn_norm_g[0].reshape(1, B_WIDTH).astype(_f32),
        norm_ffn_g[0].reshape(1, d), mod, l)

    n_tiles = (n * TOP_K) // MOE_TM + N_EXPERTS
    dest, tstart, ntile = _routing_tables(route, counts)
    xs = _dispatch(h2, dest, tstart + ntile - 1, ntile, n_tiles * MOE_TM)
    act = _moe_up(xs, w_e_gate[0], w_e_up[0], b_e_gate[0], b_e_up[0], tstart, ntile)
    ys = _moe_down(act, w_e_down[0], b_e_down[0], tstart, ntile)
    out = _combine(ys, dest, route, x1, mod, final_norm_g)
    return out.reshape(bsz, n, d)
```

```python
import functools

import jax
import jax.numpy as jnp
import numpy as np
from jax import lax
from jax.experimental import pallas as pl
from jax.experimental.pallas import tpu as pltpu

D_MODEL = 2048
GRID_W = 64
EPS = 1e-6
A_HEADS = 16
A_KV_HEADS = 2
A_GROUP = A_HEADS // A_KV_HEADS
A_HEAD_DIM = 64
A_WIDTH = A_HEADS * A_HEAD_DIM
A_KV_WIDTH = A_KV_HEADS * A_HEAD_DIM
WINDOW = 128
A_BLOCK = 128
ROPE_BASE = 10000.0
ROPE_PAIRS_PER_AXIS = A_HEAD_DIM // 4
B_HEADS = 8
B_DK = 128
B_DV = 128
B_WIDTH = B_HEADS * B_DV
N_EXPERTS = 32
TOP_K = 4
EXPERT_FF = 2048
SWIGLU_LIMIT = 7.0
SWIGLU_ALPHA = 1.702

LANES = 128
SUB = 8
HG_CHUNK = 64
HG_BLOCK = 256
ATTN_UNIT_HEADS = 4
MOE_TM = 256
MOE_NBUF = 5
NEG = -1e30
LOG2E = 1.4426950408889634
VMEM_LIMIT = 56 * 1024 * 1024

_f32 = jnp.float32
_bf16 = jnp.bfloat16


def _cparams(sem):
    return pltpu.CompilerParams(dimension_semantics=sem, vmem_limit_bytes=VMEM_LIMIT)


def _sigmoid(x):
    return 1.0 / (1.0 + jnp.exp(-x))


def _pack_bf16_pair(lo, hi):
    lo_bits = lax.bitcast_convert_type(lo.astype(_bf16).astype(_f32), jnp.uint32)
    hi_bits = lax.bitcast_convert_type(hi.astype(_bf16).astype(_f32), jnp.uint32)
    return (lo_bits >> 16) | (hi_bits & jnp.uint32(0xFFFF0000))


def _unpack_bf16_pair(w):
    lo = lax.bitcast_convert_type(w << 16, _f32)
    hi = lax.bitcast_convert_type(w & jnp.uint32(0xFFFF0000), _f32)
    return lo, hi


def _mod_kernel(s_ref, w_ref, b_ref, o_ref):
    tn = w_ref.shape[1]
    for r in range(2):
        s = s_ref[r]
        s = s * _sigmoid(s)
        for j in range(tn // LANES):
            sl = slice(j * LANES, (j + 1) * LANES)
            acc = jnp.sum(w_ref[:, sl] * s, axis=0, keepdims=True)
            o_ref[r:r + 1, sl] = acc + b_ref[:, sl]


def _mod_vectors(c, c_ctx, w_mod, b_mod):
    d, n_out = w_mod.shape
    tn = 1024
    s = jnp.stack([c.reshape(d), c_ctx.reshape(d)]).astype(_f32)
    s = jnp.broadcast_to(s[:, :, None], (2, d, LANES))
    return pl.pallas_call(
        _mod_kernel,
        out_shape=jax.ShapeDtypeStruct((2, n_out), _f32),
        grid=(n_out // tn,),
        in_specs=[pl.BlockSpec((2, d, LANES), lambda j: (0, 0, 0)),
                  pl.BlockSpec((d, tn), lambda j: (0, j)),
                  pl.BlockSpec((1, tn), lambda j: (0, j))],
        out_specs=pl.BlockSpec((2, tn), lambda j: (0, j)),
        compiler_params=_cparams(("arbitrary",)),
        name="mod_vectors",
    )(s, w_mod, b_mod.reshape(1, n_out))


def _rms_mod(xf, g, shift, scale):
    y = xf * lax.rsqrt(jnp.mean(xf * xf, axis=-1, keepdims=True) + EPS)
    return (y * g) * (1.0 + scale) + shift


def _norm_kernel(x_ref, ctx_ref, g_ref, mod_ref, o_ref, *, n_ctx_tiles):
    i = pl.program_id(0)
    d = x_ref.shape[1]

    @pl.when(i < n_ctx_tiles)
    def _():
        o_ref[...] = _rms_mod(ctx_ref[...], g_ref[...], mod_ref[1:2, 0:d],
                              mod_ref[1:2, d:2 * d]).astype(o_ref.dtype)

    @pl.when(i >= n_ctx_tiles)
    def _():
        o_ref[...] = _rms_mod(x_ref[...], g_ref[...], mod_ref[0:1, 0:d],
                              mod_ref[0:1, d:2 * d]).astype(o_ref.dtype)


def _norm_all(x2, ctx2, g, mod):
    n, d = x2.shape
    l = ctx2.shape[0]
    tm = 256
    nct = l // tm
    return pl.pallas_call(
        functools.partial(_norm_kernel, n_ctx_tiles=nct),
        out_shape=jax.ShapeDtypeStruct((l + n, d), _bf16),
        grid=((l + n) // tm,),
        in_specs=[pl.BlockSpec((tm, d), lambda i: (jnp.maximum(i - nct, 0), 0)),
                  pl.BlockSpec((tm, d), lambda i: (jnp.minimum(i, nct - 1), 0)),
                  pl.BlockSpec((1, d), lambda i: (0, 0)),
                  pl.BlockSpec(mod.shape, lambda i: (0, 0))],
        out_specs=pl.BlockSpec((tm, d), lambda i: (i, 0)),
        compiler_params=_cparams(("arbitrary",)),
        name="adaln_norm",
    )(x2, ctx2, g.reshape(1, d), mod)


def _rope(a, cos, sin):
    lane = lax.broadcasted_iota(jnp.int32, a.shape, 1)
    first = (lane % A_HEAD_DIM) < (A_HEAD_DIM // 2)
    rot = jnp.where(first, pltpu.roll(a, LANES - 32, 1), pltpu.roll(a, 32, 1))
    return a * cos + rot * sin


def _row_block_dots(h_ref, w):
    sub = h_ref.shape[0] // 8
    blocks = [slice(r * sub, (r + 1) * sub) for r in range(8)]
    return [(rs, jnp.dot(h_ref[rs, :], w, preferred_element_type=_f32)) for rs in blocks]


def _proj_qkv_kernel(h_ref, w_ref, cos_ref, sin_ref, q_ref, k_ref, v_ref):
    scale = A_HEAD_DIM ** -0.5 * LOG2E
    for rs, acc in _row_block_dots(h_ref, w_ref[...]):
        cos = cos_ref[rs, :]
        sin = sin_ref[rs, :]
        for j in range(A_WIDTH // LANES):
            sl = slice(j * LANES, (j + 1) * LANES)
            q_ref[rs, sl] = (_rope(acc[:, sl], cos, sin) * scale).astype(q_ref.dtype)
        k_ref[rs, :] = _rope(acc[:, A_WIDTH:A_WIDTH + LANES], cos, sin).astype(k_ref.dtype)
        v_ref[rs, :] = acc[:, A_WIDTH + LANES:A_WIDTH + 2 * LANES].astype(v_ref.dtype)


def _proj_heads_kernel(h_ref, w_ref, o_ref, *, act):
    for rs, acc in _row_block_dots(h_ref, w_ref[...]):
        if act == "silu":
            acc = acc * _sigmoid(acc)
        for hh in range(o_ref.shape[0]):
            o_ref[hh, rs, :] = acc[:, hh * LANES:(hh + 1) * LANES].astype(o_ref.dtype)


def _proj_gate_kernel(h_ref, w_ref, lb_ref, lf_ref, kk_ref):
    lb = lb_ref[0]
    for rs, z in _row_block_dots(h_ref, w_ref[0]):
        sg = _sigmoid(z)
        logf = jnp.log2(lb + (1.0 - lb) * sg)
        kk = (1.0 - lb) * _sigmoid(-z)
        for hh in range(lf_ref.shape[1]):
            sl = slice(hh * LANES, (hh + 1) * LANES)
            lf_ref[0, hh, rs, :] = logf[:, sl]
            kk_ref[0, hh, rs, :] = kk[:, sl].astype(kk_ref.dtype)


def _proj_sigmoid_kernel(h_ref, w_ref, o_ref):
    for rs, acc in _row_block_dots(h_ref, w_ref[...]):
        o_ref[rs, :] = _sigmoid(acc).astype(o_ref.dtype)


def _row_tile(t):
    for tm in (1280, 640, 256, 128):
        if t % tm == 0:
            return tm
    raise ValueError(t)


def _proj_qkv(h, w, cos_t, sin_t):
    t, d = h.shape
    tm = _row_tile(t)
    ncol = w.shape[1]
    return pl.pallas_call(
        _proj_qkv_kernel,
        out_shape=(jax.ShapeDtypeStruct((t, A_WIDTH), _bf16),
                   jax.ShapeDtypeStruct((t, A_KV_WIDTH), _bf16),
                   jax.ShapeDtypeStruct((t, A_KV_WIDTH), _bf16)),
        grid=(t // tm,),
        in_specs=[pl.BlockSpec((tm, d), lambda i: (i, 0)),
                  pl.BlockSpec((d, ncol), lambda i: (0, 0)),
                  pl.BlockSpec((tm, LANES), lambda i: (i, 0)),
                  pl.BlockSpec((tm, LANES), lambda i: (i, 0))],
        out_specs=(pl.BlockSpec((tm, A_WIDTH), lambda i: (i, 0)),
                   pl.BlockSpec((tm, A_KV_WIDTH), lambda i: (i, 0)),
                   pl.BlockSpec((tm, A_KV_WIDTH), lambda i: (i, 0))),
        compiler_params=_cparams(("arbitrary",)),
        name="proj_qkv",
    )(h, w, cos_t, sin_t)


def _proj_heads(h, w, act):
    t, d = h.shape
    tm = _row_tile(t)
    tn = 1024
    nh = tn // LANES
    return pl.pallas_call(
        functools.partial(_proj_heads_kernel, act=act),
        out_shape=jax.ShapeDtypeStruct((w.shape[1] // LANES, t, LANES), _bf16),
        grid=(w.shape[1] // tn, t // tm),
        in_specs=[pl.BlockSpec((tm, d), lambda j, i: (i, 0)),
                  pl.BlockSpec((d, tn), lambda j, i: (0, j))],
        out_specs=pl.BlockSpec((nh, tm, LANES), lambda j, i: (j, i, 0)),
        compiler_params=_cparams(("arbitrary", "arbitrary")),
        name="proj_heads_" + act,
    )(h, w)


def _proj_gate(h, w2, lb2):
    t, d = h.shape
    tm = _row_tile(t)
    tn = 1024
    nh = tn // LANES
    ncb = w2.shape[2] // tn
    out_sds = lambda dt: jax.ShapeDtypeStruct((2, w2.shape[2] // LANES, t, LANES), dt)
    return pl.pallas_call(
        _proj_gate_kernel,
        out_shape=(out_sds(_f32), out_sds(_bf16)),
        grid=(2, ncb, t // tm),
        in_specs=[pl.BlockSpec((tm, d), lambda r, j, i: (i, 0)),
                  pl.BlockSpec((1, d, tn), lambda r, j, i: (r, 0, j)),
                  pl.BlockSpec((1, 1, tn), lambda r, j, i: (r, 0, j))],
        out_specs=(pl.BlockSpec((1, nh, tm, LANES), lambda r, j, i: (r, j, i, 0)),
                   pl.BlockSpec((1, nh, tm, LANES), lambda r, j, i: (r, j, i, 0))),
        compiler_params=_cparams(("arbitrary", "arbitrary", "arbitrary")),
        name="proj_gate",
    )(h, w2, lb2)


def _proj_sigmoid(h, w):
    t, d = h.shape
    tm = _row_tile(t)
    tn = 1024
    return pl.pallas_call(
        _proj_sigmoid_kernel,
        out_shape=jax.ShapeDtypeStruct((t, w.shape[1]), _bf16),
        grid=(w.shape[1] // tn, t // tm),
        in_specs=[pl.BlockSpec((tm, d), lambda j, i: (i, 0)),
                  pl.BlockSpec((d, tn), lambda j, i: (0, j))],
        out_specs=pl.BlockSpec((tm, tn), lambda j, i: (i, j)),
        compiler_params=_cparams(("arbitrary", "arbitrary")),
        name="proj_sigmoid",
    )(h, w)


def _attn_kernel(q_ref, kp_ref, kc_ref, kn_ref, vp_ref, vc_ref, vn_ref,
                 kx_ref, vx_ref, bias_ref, sink_ref, o_ref):
    hd = A_HEAD_DIM
    blk = A_BLOCK
    nt = (((1,), (1,)), ((), ()))
    n_ctx = kx_ref.shape[0]
    ones = jnp.ones((n_ctx + 3 * blk, LANES), _bf16)
    units = [(g, h0) for g in range(A_KV_HEADS) for h0 in range(0, A_GROUP, ATTN_UNIT_HEADS)]
    rows = ATTN_UNIT_HEADS * blk
    def stage_scores(u):
        g, h0 = u
        gs = slice(g * hd, (g + 1) * hd)
        heads = [g * A_GROUP + h0 + h for h in range(ATTN_UNIT_HEADS)]
        qg = jnp.concatenate([q_ref[:, hh * hd:(hh + 1) * hd] for hh in heads], axis=0)
        k_pc = jnp.concatenate([kp_ref[:, gs], kc_ref[:, gs]], axis=0)
        return (lax.dot_general(qg, kx_ref[:, gs], nt, preferred_element_type=_f32),
                lax.dot_general(qg, k_pc, nt, preferred_element_type=_f32),
                lax.dot_general(qg, kn_ref[:, gs], nt, preferred_element_type=_f32))

    def stage_softmax(u, sc):
        g, h0 = u
        s_x, s_pc, s_n = sc
        s_n = s_n + bias_ref[0, 1, 0:rows, :]
        s_p = s_pc[:, 0:blk] + bias_ref[0, 0, 0:rows, :]
        s_c = s_pc[:, blk:2 * blk]
        mx = jnp.maximum(jnp.maximum(s_p, s_c), s_n)
        for j in range(n_ctx // LANES):
            mx = jnp.maximum(mx, s_x[:, j * LANES:(j + 1) * LANES])
        sink = sink_ref[g, h0 * blk:h0 * blk + rows, :]
        m = jnp.maximum(jnp.max(mx, axis=-1, keepdims=True), sink)
        p = jnp.concatenate([jnp.exp2(s_x - m), jnp.exp2(s_p - m), jnp.exp2(s_c - m), jnp.exp2(s_n - m)],
                            axis=1).astype(_bf16)
        return p, jnp.exp2(sink - m)

    def stage_out(u, pr):
        g, h0 = u
        p, sink_term = pr
        gs = slice(g * hd, (g + 1) * hd)
        v_all = jnp.concatenate([vx_ref[:, gs], vp_ref[:, gs], vc_ref[:, gs], vn_ref[:, gs]], axis=0)
        o = jnp.dot(p, v_all, preferred_element_type=_f32)
        den = jnp.dot(p, ones, preferred_element_type=_f32)
        o = o / (den[:, 0:hd] + sink_term)
        for h in range(ATTN_UNIT_HEADS):
            c0 = (g * A_GROUP + h0 + h) * hd
            o_ref[:, c0:c0 + hd] = o[h * blk:(h + 1) * blk, :].astype(o_ref.dtype)

    scores = [stage_scores(u) for u in units]
    probs = [stage_softmax(u, sc) for u, sc in zip(units, scores)]
    for u, pr in zip(units, probs):
        stage_out(u, pr)


def _attention(q, k, v, sinks, n, l):
    blk = A_BLOCK
    nblk = n // blk
    off = l // blk
    rows = A_GROUP * blk
    r = np.arange(rows)[:, None] % blk
    j = np.arange(blk)[None, :]
    keep_prev = (j - blk - r) >= -WINDOW
    keep_next = (j + blk - r) <= WINDOW
    none = np.zeros_like(keep_prev)
    variants = [(none, keep_next), (keep_prev, keep_next), (keep_prev, none)]
    bias = jnp.asarray(np.where(np.array(variants), 0.0, NEG).astype(np.float32))
    sink_col = jnp.repeat(sinks.astype(_f32).reshape(A_KV_HEADS, A_GROUP) * LOG2E, blk, axis=1)
    sink_col = sink_col.reshape(A_KV_HEADS, rows, 1)

    def bias_idx(i):
        return (jnp.where(i == 0, 0, jnp.where(i == nblk - 1, 2, 1)), 0, 0, 0)

    kv_spec = lambda f: pl.BlockSpec((blk, A_KV_WIDTH), f)
    prev = lambda i: (jnp.maximum(i - 1, 0) + off, 0)
    cur = lambda i: (i + off, 0)
    nxt = lambda i: (jnp.minimum(i + 1, nblk - 1) + off, 0)
    return pl.pallas_call(
        _attn_kernel,
        out_shape=jax.ShapeDtypeStruct((n, A_WIDTH), _bf16),
        grid=(nblk,),
        in_specs=[pl.BlockSpec((blk, A_WIDTH), cur),
                  kv_spec(prev), kv_spec(cur), kv_spec(nxt),
                  kv_spec(prev), kv_spec(cur), kv_spec(nxt),
                  pl.BlockSpec((l, A_KV_WIDTH), lambda i: (0, 0)),
                  pl.BlockSpec((l, A_KV_WIDTH), lambda i: (0, 0)),
                  pl.BlockSpec((1, 2, rows, blk), bias_idx),
                  pl.BlockSpec((A_KV_HEADS, rows, 1), lambda i: (0, 0, 0))],
        out_specs=pl.BlockSpec((blk, A_WIDTH), lambda i: (i, 0)),
        compiler_params=_cparams(("arbitrary",)),
        name="window_attention",
    )(q, k, k, k, v, v, v, k, v, bias, sink_col)


def _split3(x):
    hi = x.astype(_bf16)
    r1 = x - hi.astype(_f32)
    mid = r1.astype(_bf16)
    lo = (r1 - mid.astype(_f32)).astype(_bf16)
    return hi, mid, lo


def _hgrn_intra_scores(q, k, b, emat, reverse):
    c = HG_CHUNK
    nsb = c // SUB
    nt = (((1,), (1,)), ((), ()))
    qb = (q * jnp.exp2(b)).astype(_bf16)
    b_tot = b[0:1, :] if reverse else b[c - 1:c, :]
    kend = (k * jnp.exp2(b_tot - b)).astype(_bf16)

    zero_row = jnp.zeros((1, LANES), _f32)
    zero_blk = jnp.zeros((SUB, LANES), _f32)
    y_rows, z_rows, w_cols = [], [], []
    for ib in range(nsb):
        r0 = ib * SUB
        bi = b[r0:r0 + SUB]
        qi = q[r0:r0 + SUB]
        pieces = []
        for s in range(SUB):
            bs = b[r0 + s:r0 + s + 1]
            ks = k[r0 + s:r0 + s + 1]
            pieces.append(qi * ks * jnp.exp2(jnp.minimum(bi - bs, 0.0)))
        y_rows.append(jnp.concatenate(pieces, axis=1))
        if reverse:
            lo_r, hi_r = r0 + SUB, c
            ref_b = b[r0 + SUB:r0 + SUB + 1] if ib < nsb - 1 else zero_row
        else:
            lo_r, hi_r = 0, r0
            ref_b = b[r0 - 1:r0] if ib > 0 else zero_row
        if hi_r > lo_r:
            qp = qi * jnp.exp2(bi - ref_b)
            kp = [k[lo_r:hi_r] * jnp.exp2(ref_b - b[lo_r:hi_r])]
            if lo_r > 0:
                kp.insert(0, jnp.zeros((lo_r, LANES), _f32))
            if hi_r < c:
                kp.append(jnp.zeros((c - hi_r, LANES), _f32))
            w_cols.append(jnp.concatenate(kp, axis=0) if len(kp) > 1 else kp[0])
        else:
            qp = zero_blk
            w_cols.append(jnp.zeros((c, LANES), _f32))
        z_rows.append(jnp.concatenate([qp if jb == ib else zero_blk for jb in range(nsb)], axis=1))
    y = jnp.concatenate(y_rows, axis=0).astype(_bf16)
    z = jnp.concatenate(z_rows, axis=0).astype(_bf16)
    w = jnp.concatenate(w_cols, axis=1).astype(_bf16)
    a_diag = jnp.dot(y, emat, preferred_element_type=_f32)
    a_off = lax.dot_general(z, w, nt, preferred_element_type=_f32)
    return a_diag, a_off, qb, kend, jnp.exp2(b_tot)


def _hgrn_intra_kernel(q_ref, k_ref, lf_ref, emat_ref, a_ref, qb_ref, ke_ref, dt_ref, *, reverse):
    c = HG_CHUNK
    nch = HG_BLOCK // c
    nh = q_ref.shape[0]
    ri = lax.broadcasted_iota(jnp.int32, (c, c), 0)
    ci = lax.broadcasted_iota(jnp.int32, (c, c), 1)
    tri = jnp.where((ci >= ri) if reverse else (ci <= ri), 1.0, 0.0).astype(_bf16)
    dmask = (ci // SUB == ri // SUB) & ((ci >= ri) if reverse else (ci <= ri))
    emat = emat_ref[...]

    def body(cc, carry):
        rows = pl.ds(pl.multiple_of(cc * c, c), c)
        lf_all = jnp.concatenate([lf_ref[0, h, rows, :] for h in range(nh)], axis=1)
        hi, mid, lo = _split3(lf_all)
        b_all = (jnp.dot(tri, hi, preferred_element_type=_f32)
                 + jnp.dot(tri, mid, preferred_element_type=_f32)
                 + jnp.dot(tri, lo, preferred_element_type=_f32))
        for h in range(nh):
            a_diag, a_off, qb, kend, dtot = _hgrn_intra_scores(
                q_ref[h, rows, :].astype(_f32), k_ref[0, h, rows, :].astype(_f32),
                b_all[:, h * LANES:(h + 1) * LANES], emat, reverse)
            qb_ref[h, rows, :] = qb
            ke_ref[h, rows, :] = kend
            dt_ref[cc, h:h + 1, :] = dtot
            a = a_off + jnp.where(dmask, a_diag[:, 0:c], 0.0)
            a_ref[h, rows, :] = jnp.concatenate([a, jnp.zeros_like(a)], axis=1).astype(a_ref.dtype)
        return carry

    lax.fori_loop(0, nch, body, 0)


def _hgrn_intra(rq, kk, lf, direction):
    nh, t, _ = rq.shape
    br = HG_BLOCK
    nch = br // HG_CHUNK
    reverse = direction == 1
    spec3 = pl.BlockSpec((nh, br, LANES), lambda s: (0, s, 0))
    spec4 = pl.BlockSpec((1, nh, br, LANES), lambda s: (direction, 0, s, 0))
    sds = jax.ShapeDtypeStruct((nh, t, LANES), _bf16)
    emat = (np.arange(SUB * LANES)[:, None] // LANES == np.arange(LANES)[None, :] % SUB)
    emat = jnp.asarray(emat.astype(np.float32), dtype=_bf16)
    return pl.pallas_call(
        functools.partial(_hgrn_intra_kernel, reverse=reverse),
        out_shape=(sds, sds, sds, jax.ShapeDtypeStruct((t // HG_CHUNK, nh, LANES), _f32)),
        grid=(t // br,),
        in_specs=[spec3, spec4, spec4, pl.BlockSpec(emat.shape, lambda s: (0, 0))],
        out_specs=(spec3, spec3, spec3, pl.BlockSpec((nch, nh, LANES), lambda s: (s, 0, 0))),
        compiler_params=_cparams(("arbitrary",)),
        name="hgrn2_intra_" + ("bwd" if reverse else "fwd"),
    )(rq, kk, lf, emat)


def _hgrn_state_kernel(af_ref, qf_ref, kf_ref, df_ref, vf_ref, ab_ref, qbk_ref, kb_ref, db_ref, vb_ref,
                       outf_ref, outb_ref, st_ref):
    c = HG_CHUNK
    nch = HG_BLOCK // c
    nh = qf_ref.shape[0]
    nt = (((1,), (1,)), ((), ()))
    tn = (((0,), (0,)), ((), ()))

    @pl.when(pl.program_id(0) == 0)
    def _():
        st_ref[...] = jnp.zeros(st_ref.shape, st_ref.dtype)

    dirs = ((0, False, af_ref, qf_ref, kf_ref, df_ref, vf_ref, outf_ref),
            (1, True, ab_ref, qbk_ref, kb_ref, db_ref, vb_ref, outb_ref))

    def body(cc, carry):
        for di, reverse, a_ref, q_ref, k_ref, d_ref, v_ref, out_ref in dirs:
            chunk = (nch - 1 - cc) if reverse else cc
            rows = pl.ds(pl.multiple_of(chunk * c, c), c)
            for h in range(nh):
                st = st_ref[di, h]
                v = v_ref[h, rows, :]
                o = (jnp.dot(a_ref[h, rows, 0:c], v, preferred_element_type=_f32)
                     + lax.dot_general(q_ref[h, rows, :], st.astype(_bf16), nt, preferred_element_type=_f32))
                out_ref[h, rows, :] = o.astype(out_ref.dtype)
                upd = lax.dot_general(v, k_ref[h, rows, :], tn, preferred_element_type=_f32)
                st_ref[di, h] = st * d_ref[chunk, pl.ds(h, 1), :] + upd
        return carry

    lax.fori_loop(0, nch, body, 0)


def _hgrn_state(intra_f, intra_b, rv, l):
    a_f, qb_f, ke_f, dt_f = intra_f
    a_b, qb_b, ke_b, dt_b = intra_b
    nh, t, _ = rv.shape
    br = HG_BLOCK
    nch = br // HG_CHUNK
    nb = t // br
    nc = l // br

    def blk_b(s):
        return jnp.where(s < nc, nc - 1 - s, nb - 1 - (s - nc))

    def specs(blk):
        s3 = pl.BlockSpec((nh, br, LANES), lambda s: (0, blk(s), 0))
        sd = pl.BlockSpec((nch, nh, LANES), lambda s: (blk(s), 0, 0))
        return [s3, s3, s3, sd, s3]

    fwd = specs(lambda s: s)
    bwd = specs(blk_b)
    return pl.pallas_call(
        _hgrn_state_kernel,
        out_shape=(jax.ShapeDtypeStruct((nh, t, LANES), _bf16),
                   jax.ShapeDtypeStruct((nh, t, LANES), _bf16)),
        grid=(nb,),
        in_specs=fwd + bwd,
        out_specs=(fwd[0], bwd[0]),
        scratch_shapes=[pltpu.VMEM((2, nh, B_DV, B_DK), _f32)],
        compiler_params=_cparams(("arbitrary",)),
        name="hgrn2_state",
    )(a_f, qb_f, ke_f, dt_f, rv, a_b, qb_b, ke_b, dt_b, rv)


def _merge_kernel(ya_ref, of_ref, ob_ref, rg_ref, ga_ref, gb_ref, x_ref, wa_ref, wb_ref, wo_ref,
                  wr_ref, br_ref, hg_ref, g2_ref, mod_ref,
                  x1_ref, h2_ref, route_ref, cnt_ref, run_ref):
    i = pl.program_id(0)
    tm, d = x_ref.shape

    @pl.when(i == 0)
    def _():
        run_ref[...] = jnp.zeros(run_ref.shape, run_ref.dtype)

    parts = []
    for h in range(B_HEADS):
        o = of_ref[h].astype(_f32) + ob_ref[h].astype(_f32)
        o = o * lax.rsqrt(jnp.mean(o * o, axis=-1, keepdims=True) + EPS)
        o = o * hg_ref[:, h * LANES:(h + 1) * LANES]
        parts.append((o * rg_ref[h].astype(_f32)).astype(_bf16))
    yb = jnp.concatenate(parts, axis=1)
    z0 = jnp.dot(ya_ref[...], wa_ref[...], preferred_element_type=_f32)
    z1 = jnp.dot(yb, wb_ref[...], preferred_element_type=_f32)
    merged = ga_ref[...].astype(_f32) * z0 + gb_ref[...].astype(_f32) * z1
    y = jnp.dot(merged.astype(_bf16), wo_ref[...], preferred_element_type=_f32)
    x1 = x_ref[...] + mod_ref[0:1, 2 * d:3 * d] * y
    x1_ref[...] = x1
    h2 = _rms_mod(x1, g2_ref[...], mod_ref[0:1, 3 * d:4 * d], mod_ref[0:1, 4 * d:5 * d])
    h2_ref[...] = _pack_bf16_pair(h2[:, 0:d // 2], h2[:, d // 2:d])

    hh = h2.astype(_bf16)
    hl = (h2 - hh.astype(_f32)).astype(_bf16)
    both = jnp.dot(hh, wr_ref[...], preferred_element_type=_f32)
    logits = (both[:, 0:LANES] + both[:, LANES:2 * LANES]
              + jnp.dot(hl, wr_ref[:, 0:LANES], preferred_element_type=_f32)) + br_ref[...]
    lane = lax.broadcasted_iota(jnp.int32, (tm, LANES), 1)
    work = jnp.where(lane < N_EXPERTS, logits, NEG)
    vals, idxs = [], []
    onehot = jnp.zeros((tm, LANES), _f32)
    for _ in range(TOP_K):
        m = jnp.max(work, axis=-1, keepdims=True)
        idx = jnp.min(jnp.where(work == m, lane, LANES), axis=-1, keepdims=True)
        sel = lane == idx
        vals.append(m)
        idxs.append(idx)
        onehot = jnp.where(sel, 1.0, onehot)
        work = jnp.where(sel, NEG, work)
    es = [jnp.exp(vv - vals[0]) for vv in vals]
    tot = es[0] + es[1] + es[2] + es[3]
    ri = lax.broadcasted_iota(jnp.int32, (tm, tm), 0)
    ci = lax.broadcasted_iota(jnp.int32, (tm, tm), 1)
    ltri = jnp.where(ci < ri, 1.0, 0.0).astype(_bf16)
    prefix = jnp.dot(ltri, onehot.astype(_bf16), preferred_element_type=_f32) + run_ref[...]
    route = jnp.zeros((tm, LANES), _f32)
    for j in range(TOP_K):
        rank = jnp.sum(jnp.where(lane == idxs[j], prefix, 0.0), axis=-1, keepdims=True)
        route = jnp.where(lane == j, idxs[j].astype(_f32), route)
        route = jnp.where(lane == TOP_K + j, es[j] / tot, route)
        route = jnp.where(lane == 2 * TOP_K + j, rank, route)
    route_ref[...] = route
    run_new = run_ref[...] + jnp.sum(onehot, axis=0, keepdims=True)
    run_ref[...] = run_new
    cnt_ref[...] = run_new


def _merge_route(ya, o_f, o_b, rg, gates, x2, wa, wb, wo, wr_cat, b_router, hg, g2, mod, l):
    n, d = x2.shape
    tm = 256
    off = l // tm
    const = lambda shape: pl.BlockSpec(shape, lambda i: (0,) * len(shape),
                                       pipeline_mode=pl.Buffered(1))
    head_spec = pl.BlockSpec((B_HEADS, tm, LANES), lambda i: (0, i + off, 0))
    return pl.pallas_call(
        _merge_kernel,
        out_shape=(jax.ShapeDtypeStruct((n, d), _f32),
                   jax.ShapeDtypeStruct((n, d // 2), jnp.uint32),
                   jax.ShapeDtypeStruct((n, LANES), _f32),
                   jax.ShapeDtypeStruct((1, LANES), _f32)),
        grid=(n // tm,),
        in_specs=[pl.BlockSpec((tm, A_WIDTH), lambda i: (i, 0)),
                  head_spec, head_spec, head_spec,
                  pl.BlockSpec((tm, d), lambda i: (i + off, 0)),
                  pl.BlockSpec((tm, d), lambda i: (i + off, 1)),
                  pl.BlockSpec((tm, d), lambda i: (i, 0)),
                  const(wa.shape), const(wb.shape), const(wo.shape),
                  const(wr_cat.shape), const((1, LANES)),
                  const((1, B_WIDTH)), const((1, d)), const(mod.shape)],
        out_specs=(pl.BlockSpec((tm, d), lambda i: (i, 0)),
                   pl.BlockSpec((tm, d // 2), lambda i: (i, 0)),
                   pl.BlockSpec((tm, LANES), lambda i: (i, 0)),
                   pl.BlockSpec((1, LANES), lambda i: (0, 0))),
        scratch_shapes=[pltpu.VMEM((1, LANES), _f32)],
        compiler_params=_cparams(("arbitrary",)),
        name="merge_route",
    )(ya, o_f, o_b, rg, gates, gates, x2, wa, wb, wo, wr_cat, b_router, hg, g2, mod)


def _zero_fill_tiles(zero_ref, dst_ref, first, last, sem, cols=(None,)):
    tm = MOE_TM

    def copy(t, col):
        rows = pl.ds(pl.multiple_of(t * tm, tm), tm)
        dst = dst_ref.at[rows] if col is None else dst_ref.at[rows, col]
        return pltpu.make_async_copy(zero_ref, dst, sem)

    def start(t, carry):
        for col in cols:
            copy(t, col).start()
        return carry

    def wait(t, carry):
        for col in cols:
            copy(t, col).wait()
        return carry

    lax.fori_loop(first, last, start, 0)
    lax.fori_loop(first, last, wait, 0)


def _dispatch_kernel(lt_ref, nt_ref, dest_ref, h2_ref, xs_ref, zero_ref, sem, zsem):
    tm = h2_ref.shape[0]

    @pl.when(pl.program_id(0) == 0)
    def _():
        zero_ref[...] = jnp.zeros(zero_ref.shape, zero_ref.dtype)
        for e in range(N_EXPERTS):
            @pl.when(nt_ref[e] > 0)
            def _():
                pltpu.make_async_copy(zero_ref, xs_ref.at[pl.ds(lt_ref[e] * MOE_TM, MOE_TM)], zsem).start()
        for e in range(N_EXPERTS):
            @pl.when(nt_ref[e] > 0)
            def _():
                pltpu.make_async_copy(zero_ref, xs_ref.at[pl.ds(lt_ref[e] * MOE_TM, MOE_TM)], zsem).wait()
        _zero_fill_tiles(zero_ref, xs_ref, lt_ref[N_EXPERTS - 1] + 1, xs_ref.shape[0] // MOE_TM, zsem)

    def issue(t, carry):
        for j in range(TOP_K):
            dst = dest_ref[t * TOP_K + j]
            pltpu.make_async_copy(h2_ref.at[pl.ds(t, 1)], xs_ref.at[pl.ds(dst, 1)], sem).start()
        return carry

    lax.fori_loop(0, tm, issue, 0)
    for _ in range(TOP_K):
        pltpu.make_async_copy(h2_ref, xs_ref.at[pl.ds(0, tm)], sem).wait()


def _dispatch(h2, dest_flat, last_tile, ntile, n_slots):
    n, d = h2.shape
    tm = 512
    return pl.pallas_call(
        _dispatch_kernel,
        out_shape=jax.ShapeDtypeStruct((n_slots, d), h2.dtype),
        grid_spec=pltpu.PrefetchScalarGridSpec(
            num_scalar_prefetch=2,
            grid=(n // tm,),
            in_specs=[pl.BlockSpec((tm * TOP_K,), lambda i, lt, nt: (i,), memory_space=pltpu.SMEM),
                      pl.BlockSpec((tm, d), lambda i, lt, nt: (i, 0))],
            out_specs=pl.BlockSpec(memory_space=pl.ANY),
            scratch_shapes=[pltpu.VMEM((MOE_TM, d), h2.dtype), pltpu.SemaphoreType.DMA,
                            pltpu.SemaphoreType.DMA]),
        compiler_params=_cparams(("arbitrary",)),
        name="moe_dispatch",
    )(last_tile, ntile, dest_flat, h2)


def _cast_rows(src_ref, dst_ref):
    rows = dst_ref.shape[0]
    step = 256

    def body(i, carry):
        r = pl.ds(pl.multiple_of(i * step, step), step)
        dst_ref[r, :] = src_ref[r, :].astype(dst_ref.dtype)
        return carry

    lax.fori_loop(0, rows // step, body, 0)


def _expert_stream(ts_ref, nt_ref, weights, in_copy, out_copy, compute):
    nb = MOE_NBUF
    n_exp = nt_ref.shape[0]
    total = ts_ref[n_exp - 1] + nt_ref[n_exp - 1]
    for k in range(nb - 1):
        @pl.when(total > k)
        def _():
            in_copy(k, k).start(priority=1)
    for fetch, _, _ in weights:
        fetch(0, 0).start()

    def expert_body(e, carry):
        wslot = lax.rem(e, 2)
        for fetch, _, _ in weights:
            fetch(e, wslot).wait()

        @pl.when(e + 1 < n_exp)
        def _():
            for fetch, _, _ in weights:
                fetch(e + 1, 1 - wslot).start()

        for _, stage_ref, bf_ref in weights:
            _cast_rows(stage_ref.at[wslot], bf_ref)
        t0 = ts_ref[e]

        def tile_body(t, c2):
            g = t0 + t
            slot = lax.rem(g, nb)
            in_copy(g, slot).wait()

            @pl.when(g + nb - 1 < total)
            def _():
                in_copy(g + nb - 1, lax.rem(g + nb - 1, nb)).start(priority=1)

            @pl.when(g >= nb)
            def _():
                out_copy(g - nb, slot).wait()

            compute(e, slot)
            out_copy(g, slot).start(priority=1)
            return c2

        lax.fori_loop(0, nt_ref[e], tile_body, 0)
        return carry

    lax.fori_loop(0, n_exp, expert_body, 0)
    for k in range(1, nb + 1):
        @pl.when(total >= k)
        def _():
            out_copy(total - k, lax.rem(total - k, nb)).wait()
    return total


def _tile_rows(g):
    return pl.ds(pl.multiple_of(g * MOE_TM, MOE_TM), MOE_TM)


def _moe_up_kernel(ts_ref, nt_ref, xs_ref, wg_ref, wu_ref, bg_ref, bu_ref, act_ref,
                   wgs_ref, wus_ref, wgb_ref, wub_ref, xbuf, obuf, sem_wg, sem_wu, sem_in, sem_out):
    c = pl.program_id(0)
    tm = MOE_TM
    fc = wgb_ref.shape[1]
    col = pl.ds(pl.multiple_of(c * fc, fc), fc)

    def in_copy(g, slot):
        return pltpu.make_async_copy(xs_ref.at[_tile_rows(g)], xbuf.at[slot], sem_in.at[slot])

    def out_copy(g, slot):
        return pltpu.make_async_copy(obuf.at[slot], act_ref.at[_tile_rows(g), col], sem_out.at[slot])

    def fetch_g(e, slot):
        return pltpu.make_async_copy(wg_ref.at[e, :, col], wgs_ref.at[slot], sem_wg.at[slot])

    def fetch_u(e, slot):
        return pltpu.make_async_copy(wu_ref.at[e, :, col], wus_ref.at[slot], sem_wu.at[slot])

    def compute(e, slot):
        x_lo, x_hi = _unpack_bf16_pair(xbuf[slot])
        x_lo, x_hi = x_lo.astype(_bf16), x_hi.astype(_bf16)
        half = x_lo.shape[1]

        def proj(w_ref, b_ref):
            return (jnp.dot(x_lo, w_ref[0:half, :], preferred_element_type=_f32)
                    + jnp.dot(x_hi, w_ref[half:2 * half, :], preferred_element_type=_f32)
                    + b_ref[pl.ds(e, 1), col])

        g = jnp.minimum(proj(wgb_ref, bg_ref), SWIGLU_LIMIT)
        u = jnp.clip(proj(wub_ref, bu_ref), -SWIGLU_LIMIT, SWIGLU_LIMIT)
        obuf[slot] = (g * _sigmoid(SWIGLU_ALPHA * g) * (u + 1.0)).astype(obuf.dtype)

    total = _expert_stream(ts_ref, nt_ref, [(fetch_g, wgs_ref, wgb_ref), (fetch_u, wus_ref, wub_ref)],
                           in_copy, out_copy, compute)
    obuf[0] = jnp.zeros(obuf.shape[1:], obuf.dtype)
    _zero_fill_tiles(obuf.at[0], act_ref, total, act_ref.shape[0] // tm, sem_out.at[0], cols=(col,))


def _moe_up(xs, wg, wu, bg, bu, tstart, ntile):
    n_slots = xs.shape[0]
    ne, d, ff = wg.shape
    fc = ff // 2
    tm = MOE_TM
    any_spec = pl.BlockSpec(memory_space=pl.ANY)
    b_spec = pl.BlockSpec((ne, ff), lambda c, ts, nt: (0, 0))
    return pl.pallas_call(
        _moe_up_kernel,
        out_shape=jax.ShapeDtypeStruct((n_slots, ff), _bf16),
        grid_spec=pltpu.PrefetchScalarGridSpec(
            num_scalar_prefetch=2,
            grid=(ff // fc,),
            in_specs=[any_spec, any_spec, any_spec, b_spec, b_spec],
            out_specs=any_spec,
            scratch_shapes=[pltpu.VMEM((2, d, fc), _f32), pltpu.VMEM((2, d, fc), _f32),
                            pltpu.VMEM((d, fc), _bf16), pltpu.VMEM((d, fc), _bf16),
                            pltpu.VMEM((MOE_NBUF, tm, xs.shape[1]), xs.dtype),
                            pltpu.VMEM((MOE_NBUF, tm, fc), _bf16),
                            pltpu.SemaphoreType.DMA((2,)), pltpu.SemaphoreType.DMA((2,)),
                            pltpu.SemaphoreType.DMA((MOE_NBUF,)), pltpu.SemaphoreType.DMA((MOE_NBUF,))]),
        compiler_params=_cparams(("arbitrary",)),
        name="moe_gate_up",
    )(tstart, ntile, xs, wg, wu, bg, bu)


def _moe_down_kernel(ts_ref, nt_ref, act_ref, wd_ref, bd_ref, ys_ref,
                     wds_ref, wdb_ref, abuf, ybuf, sem_w, sem_in, sem_out):
    tm = MOE_TM
    d = wdb_ref.shape[1]

    def in_copy(g, slot):
        return pltpu.make_async_copy(act_ref.at[_tile_rows(g)], abuf.at[slot], sem_in.at[slot])

    def out_copy(g, slot):
        return pltpu.make_async_copy(ybuf.at[slot], ys_ref.at[_tile_rows(g)], sem_out.at[slot])

    def fetch(e, slot):
        return pltpu.make_async_copy(wd_ref.at[e], wds_ref.at[slot], sem_w.at[slot])

    def compute(e, slot):
        y = jnp.dot(abuf[slot], wdb_ref[...], preferred_element_type=_f32) + bd_ref[pl.ds(e, 1), :]
        ybuf[slot] = _pack_bf16_pair(y[:, 0:d // 2], y[:, d // 2:d])

    total = _expert_stream(ts_ref, nt_ref, [(fetch, wds_ref, wdb_ref)], in_copy, out_copy, compute)
    ybuf[0] = jnp.zeros(ybuf.shape[1:], ybuf.dtype)
    _zero_fill_tiles(ybuf.at[0], ys_ref, total, ys_ref.shape[0] // tm, sem_out.at[0])


def _moe_down(act, wd, bd, tstart, ntile):
    n_slots, ff = act.shape
    ne, _, d = wd.shape
    tm = MOE_TM
    any_spec = pl.BlockSpec(memory_space=pl.ANY)
    return pl.pallas_call(
        _moe_down_kernel,
        out_shape=jax.ShapeDtypeStruct((n_slots, d // 2), jnp.uint32),
        grid_spec=pltpu.PrefetchScalarGridSpec(
            num_scalar_prefetch=2,
            grid=(1,),
            in_specs=[any_spec, any_spec, pl.BlockSpec((ne, d), lambda i, ts, nt: (0, 0))],
            out_specs=any_spec,
            scratch_shapes=[pltpu.VMEM((2, ff, d), _f32), pltpu.VMEM((ff, d), _bf16),
                            pltpu.VMEM((MOE_NBUF, tm, ff), _bf16), pltpu.VMEM((MOE_NBUF, tm, d // 2), jnp.uint32),
                            pltpu.SemaphoreType.DMA((2,)),
                            pltpu.SemaphoreType.DMA((MOE_NBUF,)), pltpu.SemaphoreType.DMA((MOE_NBUF,))]),
        compiler_params=_cparams(("arbitrary",)),
        name="moe_down",
    )(tstart, ntile, act, wd, bd)


def _combine_kernel(dcur_ref, dnxt_ref, ys_ref, route_ref, x1_ref, mod_ref, gf_ref, o_ref, buf_ref, sem):
    tm, d = x1_ref.shape
    i = pl.program_id(0)
    slot = lax.rem(i, 2)

    def gather(dest_ref, s):
        def issue(t, carry):
            for j in range(TOP_K):
                src = dest_ref[t * TOP_K + j]
                pltpu.make_async_copy(ys_ref.at[pl.ds(src, 1)], buf_ref.at[s, j, pl.ds(t, 1)], sem.at[s]).start()
            return carry

        lax.fori_loop(0, tm, issue, 0)

    @pl.when(i == 0)
    def _():
        gather(dcur_ref, slot)

    @pl.when(i + 1 < pl.num_programs(0))
    def _():
        gather(dnxt_ref, 1 - slot)

    for j in range(TOP_K):
        pltpu.make_async_copy(ys_ref.at[pl.ds(0, tm)], buf_ref.at[slot, j], sem.at[slot]).wait()
    acc_lo = jnp.zeros((tm, d // 2), _f32)
    acc_hi = jnp.zeros((tm, d // 2), _f32)
    for j in range(TOP_K):
        gate = route_ref[:, TOP_K + j:TOP_K + j + 1]
        y_lo, y_hi = _unpack_bf16_pair(buf_ref[slot, j])
        acc_lo = acc_lo + gate * y_lo
        acc_hi = acc_hi + gate * y_hi
    x2 = x1_ref[...] + mod_ref[0:1, 5 * d:6 * d] * jnp.concatenate([acc_lo, acc_hi], axis=1)
    y = x2 * lax.rsqrt(jnp.mean(x2 * x2, axis=-1, keepdims=True) + EPS)
    o_ref[...] = y * gf_ref[...]


def _combine(ys, dest_flat, route, x1, mod, gf):
    n, d = x1.shape
    tm = 512
    nsteps = n // tm
    dest_spec = lambda f: pl.BlockSpec((tm * TOP_K,), f, memory_space=pltpu.SMEM)
    return pl.pallas_call(
        _combine_kernel,
        out_shape=jax.ShapeDtypeStruct((n, d), _f32),
        grid=(nsteps,),
        in_specs=[dest_spec(lambda i: (i,)), dest_spec(lambda i: (jnp.minimum(i + 1, nsteps - 1),)),
                  pl.BlockSpec(memory_space=pl.ANY),
                  pl.BlockSpec((tm, LANES), lambda i: (i, 0)),
                  pl.BlockSpec((tm, d), lambda i: (i, 0)),
                  pl.BlockSpec(mod.shape, lambda i: (0, 0)),
                  pl.BlockSpec((1, d), lambda i: (0, 0))],
        out_specs=pl.BlockSpec((tm, d), lambda i: (i, 0)),
        scratch_shapes=[pltpu.VMEM((2, TOP_K, tm, d // 2), jnp.uint32), pltpu.SemaphoreType.DMA((2,))],
        compiler_params=_cparams(("arbitrary",)),
        name="moe_combine",
    )(dest_flat, dest_flat, ys, route, x1, mod, gf.reshape(1, d))


def _rope_tables(n, l):
    rows = n // GRID_W
    inv_freq = ROPE_BASE ** (-jnp.arange(ROPE_PAIRS_PER_AXIS, dtype=_f32) / ROPE_PAIRS_PER_AXIS)
    ang_r = jnp.arange(rows, dtype=_f32)[:, None] * inv_freq
    ang_c = jnp.arange(GRID_W, dtype=_f32)[:, None] * inv_freq
    rep = lambda t_r, t_c: jnp.concatenate(
        [jnp.repeat(t_r, GRID_W, axis=0), jnp.tile(t_c, (rows, 1))], axis=-1)
    cos, sin = rep(jnp.cos(ang_r), jnp.cos(ang_c)), rep(jnp.sin(ang_r), jnp.sin(ang_c))
    cos_t = jnp.tile(cos, (1, LANES // 32))
    sin_t = jnp.tile(jnp.concatenate([-sin, sin], axis=-1), (1, LANES // A_HEAD_DIM))
    cos_t = jnp.concatenate([jnp.ones((l, LANES), _f32), cos_t], axis=0)
    sin_t = jnp.concatenate([jnp.zeros((l, LANES), _f32), sin_t], axis=0)
    return cos_t, sin_t


def _routing_tables(route, counts):
    tm = MOE_TM
    idx = route[:, 0:TOP_K].astype(jnp.int32)
    rank = route[:, 2 * TOP_K:3 * TOP_K].astype(jnp.int32)
    cnt = counts[0, :N_EXPERTS].astype(jnp.int32)
    ntile = (cnt + tm - 1) // tm
    tstart = jnp.cumsum(ntile) - ntile
    slot0 = tstart * tm
    onehot = idx[:, :, None] == jnp.arange(N_EXPERTS, dtype=jnp.int32)
    dest = jnp.sum(jnp.where(onehot, slot0, 0), axis=-1) + rank
    return dest.reshape(-1), tstart, ntile


def kernel(x, c, ctx, c_ctx, norm_mix_g, norm_ffn_g, w_mod, b_mod, w_in, attn_sinks,
           lb_fwd_logits, lb_bwd_logits, hgrn_norm_g, w_branch, w_out, w_router, b_router,
           w_e_gate, b_e_gate, w_e_up, b_e_up, w_e_down, b_e_down, final_norm_g):
    bsz, n, d = x.shape
    l = ctx.shape[1]
    assert bsz == 1 and d == D_MODEL and norm_mix_g.shape[0] == 1
    assert n % HG_BLOCK == 0 and l % HG_BLOCK == 0 and n // A_BLOCK >= 2
    x2 = x.reshape(n, d)
    ctx2 = ctx.reshape(l, d)

    mod = _mod_vectors(c, c_ctx, w_mod[0], b_mod[0])
    h_all = _norm_all(x2, ctx2, norm_mix_g[0], mod)

    w = w_in[0].astype(_bf16)
    c0 = 0
    segs = []
    for width in (A_WIDTH + 2 * A_KV_WIDTH, B_WIDTH, B_WIDTH, B_WIDTH, B_WIDTH, B_WIDTH, 2 * d):
        segs.append(w[:, c0:c0 + width])
        c0 += width
    w_qkv, w_rq, w_zf, w_zb, w_rv, w_rg, w_gates = segs
    cos_t, sin_t = _rope_tables(n, l)
    q, k, v = _proj_qkv(h_all, w_qkv, cos_t, sin_t)
    rq = _proj_heads(h_all, w_rq, "silu")
    rv = _proj_heads(h_all, w_rv, "none")
    rg = _proj_heads(h_all, w_rg, "silu")
    lb_f = jax.nn.softmax(lb_fwd_logits.astype(_f32), axis=0)[0]
    lb_b = jax.nn.softmax(lb_bwd_logits.astype(_f32), axis=0)[0]
    lb2 = jnp.stack([lb_f, lb_b]).reshape(2, 1, B_WIDTH)
    lf, kk = _proj_gate(h_all, jnp.stack([w_zf, w_zb]), lb2)
    gates = _proj_sigmoid(h_all, w_gates)

    ya = _attention(q, k, v, attn_sinks[0], n, l)
    o_f, o_b = _hgrn_state(_hgrn_intra(rq, kk, lf, 0), _hgrn_intra(rq, kk, lf, 1), rv, l)

    wr = jnp.zeros((d, LANES), _f32).at[:, :N_EXPERTS].set(w_router[0].astype(_f32))
    wr_hi = wr.astype(_bf16)
    wr_lo = (wr - wr_hi.astype(_f32)).astype(_bf16)
    br = jnp.zeros((1, LANES), _f32).at[0, :N_EXPERTS].set(b_router[0].astype(_f32))
    x1, h2, route, counts = _merge_route(
        ya, o_f, o_b, rg, gates, x2,
        w_branch[0, 0].astype(_bf16), w_branch[0, 1].astype(_bf16), w_out[0].astype(_bf16),
        jnp.concatenate([wr_hi, wr_lo], axis=1), br, hgrn_norm_g[0].reshape(1, B_WIDTH).astype(_f32),
        norm_ffn_g[0].reshape(1, d), mod, l)

    n_tiles = (n * TOP_K) // MOE_TM + N_EXPERTS
    dest, tstart, ntile = _routing_tables(route, counts)
    xs = _dispatch(h2, dest, tstart + ntile - 1, ntile, n_tiles * MOE_TM)
    act = _moe_up(xs, w_e_gate[0], w_e_up[0], b_e_gate[0], b_e_up[0], tstart, ntile)
    ys = _moe_down(act, w_e_down[0], b_e_down[0], tstart, ntile)
    out = _combine(ys, dest, route, x1, mod, final_norm_g)
    return out.reshape(bsz, n, d)
```
